```python
import math
import jax, jax.numpy as jnp
from jax import lax
import numpy as np

D_MODEL = 1024
BATCH = 4
SEQ = 4096
DEPTH = 2
DEC_BATCH = 32
DEC_SEQ = 8
PAST_LEN = 8192
PAGE_SIZE = 128

N_EVEN = (DEPTH + 1) // 2
N_ODD = DEPTH // 2
MIX_W = D_MODEL
HALF_W = MIX_W // 2
A_HEADS = 4
A_V = HALF_W // A_HEADS
A_QK = A_V // 2
QK_W = A_HEADS * 2 * A_QK
V_W = A_HEADS * A_V
Q_BLOCK = 128
B_W = HALF_W
B_BLOCKS = 8
B_BS = B_W // B_BLOCKS
B_CONV = 4
C_RG = 8.0
C_W = HALF_W
C_GROUP = 16
C_GROUPS = C_W // C_GROUP
C_STATE = 64
D_W = HALF_W
D_CONV = 31
D_FF = 2752
IN_EVEN = 2 * QK_W + V_W + 2 * B_W
IN_ODD = C_W + 2 * D_W
EPS = 1e-6

kernel_name = 'hybrid_diffattn_rglru_s5_conformer_step'


def _rmsnorm(x, g):
    x32 = x.astype(jnp.float32)
    y = x32 * lax.rsqrt(jnp.mean(x32 * x32, axis=-1, keepdims=True) + EPS)
    return (y * g.astype(jnp.float32)).astype(x.dtype)


def _layernorm(x, g, b):
    x32 = x.astype(jnp.float32)
    xc = x32 - jnp.mean(x32, axis=-1, keepdims=True)
    y = xc * lax.rsqrt(jnp.mean(xc * xc, axis=-1, keepdims=True) + EPS)
    return (y * g.astype(jnp.float32) + b.astype(jnp.float32)).astype(x.dtype)


def _swiglu_ffn(x, g, w_gu, w_down):
    gate, up = jnp.split(_rmsnorm(x, g) @ w_gu, 2, axis=-1)
    return (jax.nn.silu(gate) * up) @ w_down


def _causal_dwconv(x, buf, w, b):
    inp = jnp.concatenate([buf.astype(x.dtype), x], axis=1)
    out = lax.conv_general_dilated(inp, w[:, None, :].astype(x.dtype), (1,), 'VALID',
                                   dimension_numbers=('NWC', 'WIO', 'NWC'),
                                   feature_group_count=x.shape[-1])
    return out + b.astype(x.dtype), inp[:, inp.shape[1] - (w.shape[0] - 1):]


def _diff_attn_core(q, q_pos, keys, vals, k_pos, lam):
    s = jnp.einsum('nqhjd,nkhjd->nhjqk', q, keys).astype(jnp.float32) * (A_QK ** -0.5)
    s = jnp.where(k_pos[None, :] <= q_pos[:, None], s, -jnp.inf)
    p = jax.nn.softmax(s, axis=-1)
    w = p[:, :, 0] - lam * p[:, :, 1]
    return jnp.einsum('nhqk,nkhd->nqhd', w.astype(vals.dtype), vals)


def _diff_attn_prompt(q, k, v, lam):
    n, t = q.shape[:2]
    k_pos = jnp.arange(t)

    def block(i):
        start = i * Q_BLOCK
        qb = lax.dynamic_slice_in_dim(q, start, Q_BLOCK, axis=1)
        return _diff_attn_core(qb, start + jnp.arange(Q_BLOCK), k, v, k_pos, lam)

    out = lax.map(block, jnp.arange(t // Q_BLOCK))
    return jnp.moveaxis(out, 0, 1).reshape(n, t, A_HEADS, A_V)


def _linear_scan(a, b):
    def comb(e1, e2):
        return (e1[0] * e2[0], e2[0] * e1[1] + e2[1])
    _, h = lax.associative_scan(comb, (a, b), axis=1)
    return h


def _complex_scan(ar, ai, br, bi):
    def comb(e1, e2):
        a1r, a1i, b1r, b1i = e1
        a2r, a2i, b2r, b2i = e2
        return (a2r * a1r - a2i * a1i, a2r * a1i + a2i * a1r,
                a2r * b1r - a2i * b1i + b2r, a2r * b1i + a2i * b1r + b2i)
    _, _, hr, hi = lax.associative_scan(comb, (ar, ai, br, bi), axis=1)
    return hr, hi


def _rglru(xb, gb, conv0, h0, conv_w, conv_b, gate_w, gate_b, lam_p):
    f32 = jnp.float32
    n, t, _ = xb.shape
    xc, conv_new = _causal_dwconv(xb, conv0, conv_w, conv_b)
    xblk = xc.reshape(n, t, B_BLOCKS, B_BS)
    pre = jnp.einsum('ntbi,gbij->gntbj', xblk, gate_w).reshape(2, n, t, B_W)
    gates = jax.nn.sigmoid(pre.astype(f32) + gate_b.astype(f32)[:, None, None, :])
    log_a = -C_RG * gates[0] * jax.nn.softplus(-lam_p.astype(f32))
    a = jnp.exp(log_a)
    b = jnp.sqrt(-jnp.expm1(2.0 * log_a)) * (gates[1] * xc.astype(f32))
    b = b.at[:, 0].add(a[:, 0] * h0.astype(f32))
    h = _linear_scan(a, b)
    y = jax.nn.gelu(gb.astype(f32)) * h
    return y.astype(xb.dtype), conv_new, h[:, -1].astype(xb.dtype)


def _s5(u, re0, im0, a_re, a_im, log_dt, b_re, b_im, c_re, c_im, d, glu_w, glu_b):
    f32 = jnp.float32
    n, t, _ = u.shape
    u32 = u.astype(f32).reshape(n, t, C_GROUPS, C_GROUP)
    a_re = a_re.astype(f32)
    a_im = a_im.astype(f32)
    dt = jnp.exp(log_dt.astype(f32))[:, None]
    mag = jnp.exp(dt * a_re)
    ab_re = mag * jnp.cos(dt * a_im)
    ab_im = mag * jnp.sin(dt * a_im)
    den = a_re * a_re + a_im * a_im
    nr = ab_re - 1.0
    coef_re = (nr * a_re + ab_im * a_im) / den
    coef_im = (ab_im * a_re - nr * a_im) / den
    bu_re = jnp.einsum('ntgc,gpc->ntgp', u32, b_re.astype(f32))
    bu_im = jnp.einsum('ntgc,gpc->ntgp', u32, b_im.astype(f32))
    v_re = coef_re * bu_re - coef_im * bu_im
    v_im = coef_re * bu_im + coef_im * bu_re
    h0r = re0.astype(f32)
    h0i = im0.astype(f32)
    v_re = v_re.at[:, 0].add(ab_re * h0r - ab_im * h0i)
    v_im = v_im.at[:, 0].add(ab_re * h0i + ab_im * h0r)
    s_re, s_im = _complex_scan(jnp.broadcast_to(ab_re, v_re.shape), jnp.broadcast_to(ab_im, v_re.shape), v_re, v_im)
    y = (jnp.einsum('ntgp,gcp->ntgc', s_re, c_re.astype(f32))
         - jnp.einsum('ntgp,gcp->ntgc', s_im, c_im.astype(f32))
         + d.astype(f32).reshape(C_GROUPS, C_GROUP) * u32).reshape(n, t, C_W)
    z = jax.nn.gelu(y)
    out = z * jax.nn.sigmoid(z @ glu_w.astype(f32) + glu_b.astype(f32))
    return out.astype(u.dtype), s_re[:, -1].astype(u.dtype), s_im[:, -1].astype(u.dtype)


def _conformer_conv(dp, conv0, w, b, ln_g, ln_b):
    val, gate = jnp.split(dp, 2, axis=-1)
    g = val * jax.nn.sigmoid(gate)
    c, conv_new = _causal_dwconv(g, conv0, w, b)
    return jax.nn.silu(_layernorm(c, ln_g, ln_b)), conv_new


def _run_group(x, cache_k, cache_v, page_table, conv_b0, h_b0, c_re0, c_im0, conv_d0, p):
    n, t, _ = x.shape
    ks, vs, cbs, hbs, cres, cims, cds = [], [], [], [], [], [], []
    for l in range(DEPTH):
        x = x + 0.5 * _swiglu_ffn(x, p['ffn1_g'][l], p['ffn1_w_gu'][l], p['ffn1_w_down'][l])
        h = _rmsnorm(x, p['mix_g'][l])
        if l % 2 == 0:
            e = l // 2
            proj = h @ p['even_w_in'][e]
            q, k, v, xb, gb = jnp.split(proj, [QK_W, 2 * QK_W, 2 * QK_W + V_W, 2 * QK_W + V_W + B_W], axis=-1)
            q = _rmsnorm(q.reshape(n, t, A_HEADS, 2, A_QK), p['a_q_g'][e])
            k = _rmsnorm(k.reshape(n, t, A_HEADS, 2, A_QK), p['a_k_g'][e])
            v = v.reshape(n, t, A_HEADS, A_V)
            lam_init = 0.8 - 0.6 * math.exp(-0.3 * l)
            lv = p['a_lambda'][e].astype(jnp.float32)
            lam = jnp.exp(jnp.sum(lv[0] * lv[1])) - jnp.exp(jnp.sum(lv[2] * lv[3])) + lam_init
            if cache_k is None:
                att = _diff_attn_prompt(q, k, v, lam)
            else:
                past_len = page_table.shape[1] * cache_k.shape[2]
                pk = cache_k[e][page_table].reshape(n, past_len, A_HEADS, 2, A_QK).astype(k.dtype)
                pv = cache_v[e][page_table].reshape(n, past_len, A_HEADS, A_V).astype(v.dtype)
                att = _diff_attn_core(q, past_len + jnp.arange(t), jnp.concatenate([pk, k], axis=1),
                                      jnp.concatenate([pv, v], axis=1), jnp.arange(past_len + t), lam)
            att = (_rmsnorm(att, p['a_head_g'][e]) * (1.0 - lam_init)).reshape(n, t, V_W)
            rec, cb, hb = _rglru(xb, gb, conv_b0[e], h_b0[e], p['b_conv_w'][e], p['b_conv_b'][e],
                                 p['b_gate_w'][e], p['b_gate_b'][e], p['b_lambda'][e])
            y = jnp.concatenate([att, rec], axis=-1) @ p['even_w_out'][e]
            ks.append(k.reshape(n, t, A_HEADS, 2 * A_QK))
            vs.append(v)
            cbs.append(cb)
            hbs.append(hb)
        else:
            o = l // 2
            proj = h @ p['odd_w_in'][o]
            u, dp = jnp.split(proj, [C_W], axis=-1)
            c_out, cre, cim = _s5(u, c_re0[o], c_im0[o], p['c_a_re'][o], p['c_a_im'][o], p['c_log_dt'][o],
                                  p['c_b_re'][o], p['c_b_im'][o], p['c_c_re'][o], p['c_c_im'][o],
                                  p['c_d'][o], p['c_glu_w'][o], p['c_glu_b'][o])
            d_out, cd = _conformer_conv(dp, conv_d0[o], p['d_conv_w'][o], p['d_conv_b'][o],
                                        p['d_ln_g'][o], p['d_ln_b'][o])
            y = jnp.concatenate([c_out, d_out], axis=-1) @ p['odd_w_out'][o]
            cres.append(cre)
            cims.append(cim)
            cds.append(cd)
        x = x + y
        x = x + 0.5 * _swiglu_ffn(x, p['ffn2_g'][l], p['ffn2_w_gu'][l], p['ffn2_w_down'][l])
    return (x, jnp.stack(ks), jnp.stack(vs), jnp.stack(cbs), jnp.stack(hbs),
            jnp.stack(cres), jnp.stack(cims), jnp.stack(cds))


def setup_inputs(seed: int = 0) -> dict:
    key = jax.random.key(seed)
    ks = iter(jax.random.split(key, 64))
    f32 = jnp.float32

    def nrm(shape, scale):
        return scale * jax.random.normal(next(ks), shape, f32)

    def gain(shape):
        return 1.0 + nrm(shape, 0.01)

    n_pages = PAST_LEN // PAGE_SIZE
    n_used = DEC_BATCH * n_pages
    n_pool = n_used + n_used // 4
    page_table = jax.random.permutation(next(ks), n_pool)[:n_used].reshape(DEC_BATCH, n_pages).astype(jnp.int32)

    u_a = jax.random.uniform(next(ks), (N_EVEN, B_W), f32, 0.9, 0.999)
    a_base = u_a ** (1.0 / C_RG)
    b_lambda = jnp.log(a_base) - jnp.log1p(-a_base)

    c_a_re = -0.5 + nrm((N_ODD, C_GROUPS, C_STATE), 0.01)
    c_a_im = math.pi * jnp.arange(C_STATE, dtype=f32) + nrm((N_ODD, C_GROUPS, C_STATE), 0.01)
    c_log_dt = jax.random.uniform(next(ks), (N_ODD, C_GROUPS), f32, math.log(1e-3), math.log(1e-1))

    return {
        'x_prompt': nrm((BATCH, SEQ, D_MODEL), 1.0),
        'x_sample': nrm((DEC_BATCH, DEC_SEQ, D_MODEL), 1.0),
        'cache_k': nrm((N_EVEN, n_pool, PAGE_SIZE, A_HEADS, 2 * A_QK), 1.0),
        'cache_v': nrm((N_EVEN, n_pool, PAGE_SIZE, A_HEADS, A_V), 1.0),
        'state_conv_b': nrm((N_EVEN, DEC_BATCH, B_CONV - 1, B_W), 1.0),
        'state_h_b': nrm((N_EVEN, DEC_BATCH, B_W), 0.5),
        'state_c_re': nrm((N_ODD, DEC_BATCH, C_GROUPS, C_STATE), 0.1),
        'state_c_im': nrm((N_ODD, DEC_BATCH, C_GROUPS, C_STATE), 0.1),
        'state_conv_d': nrm((N_ODD, DEC_BATCH, D_CONV - 1, D_W), 1.0),
        'page_table': page_table,
        'ffn1_g': gain((DEPTH, D_MODEL)),
        'ffn1_w_gu': nrm((DEPTH, D_MODEL, 2 * D_FF), D_MODEL ** -0.5),
        'ffn1_w_down': nrm((DEPTH, D_FF, D_MODEL), D_FF ** -0.5),
        'mix_g': gain((DEPTH, D_MODEL)),
        'ffn2_g': gain((DEPTH, D_MODEL)),
        'ffn2_w_gu': nrm((DEPTH, D_MODEL, 2 * D_FF), D_MODEL ** -0.5),
        'ffn2_w_down': nrm((DEPTH, D_FF, D_MODEL), D_FF ** -0.5),
        'even_w_in': nrm((N_EVEN, D_MODEL, IN_EVEN), D_MODEL ** -0.5),
        'even_w_out': nrm((N_EVEN, MIX_W, D_MODEL), MIX_W ** -0.5),
        'a_q_g': gain((N_EVEN, A_QK)),
        'a_k_g': gain((N_EVEN, A_QK)),
        'a_lambda': nrm((N_EVEN, 4, A_QK), 0.1),
        'a_head_g': gain((N_EVEN, A_V)),
        'b_conv_w': nrm((N_EVEN, B_CONV, B_W), B_CONV ** -0.5),
        'b_conv_b': nrm((N_EVEN, B_W), 0.01),
        'b_gate_w': nrm((N_EVEN, 2, B_BLOCKS, B_BS, B_BS), B_BS ** -0.5),
        'b_gate_b': nrm((N_EVEN, 2, B_W), 0.01),
        'b_lambda': b_lambda,
        'odd_w_in': nrm((N_ODD, D_MODEL, IN_ODD), D_MODEL ** -0.5),
        'odd_w_out': nrm((N_ODD, MIX_W, D_MODEL), MIX_W ** -0.5),
        'c_a_re': c_a_re,
        'c_a_im': c_a_im,
        'c_log_dt': c_log_dt,
        'c_b_re': nrm((N_ODD, C_GROUPS, C_STATE, C_GROUP), (2 * C_GROUP) ** -0.5),
        'c_b_im': nrm((N_ODD, C_GROUPS, C_STATE, C_GROUP), (2 * C_GROUP) ** -0.5),
        'c_c_re': nrm((N_ODD, C_GROUPS, C_GROUP, C_STATE), (2 * C_STATE) ** -0.5),
        'c_c_im': nrm((N_ODD, C_GROUPS, C_GROUP, C_STATE), (2 * C_STATE) ** -0.5),
        'c_d': nrm((N_ODD, C_W), 0.5),
        'c_glu_w': nrm((N_ODD, C_W, C_W), C_W ** -0.5),
        'c_glu_b': nrm((N_ODD, C_W), 0.01),
        'd_conv_w': nrm((N_ODD, D_CONV, D_W), D_CONV ** -0.5),
        'd_conv_b': nrm((N_ODD, D_W), 0.01),
        'd_ln_g': gain((N_ODD, D_W)),
        'd_ln_b': nrm((N_ODD, D_W), 0.01),
    }


def reference(x_prompt, x_sample, cache_k, cache_v, state_conv_b, state_h_b, state_c_re, state_c_im,
              state_conv_d, page_table, ffn1_g, ffn1_w_gu, ffn1_w_down, mix_g, ffn2_g, ffn2_w_gu,
              ffn2_w_down, even_w_in, even_w_out, a_q_g, a_k_g, a_lambda, a_head_g, b_conv_w, b_conv_b,
              b_gate_w, b_gate_b, b_lambda, odd_w_in, odd_w_out, c_a_re, c_a_im, c_log_dt, c_b_re, c_b_im,
              c_c_re, c_c_im, c_d, c_glu_w, c_glu_b, d_conv_w, d_conv_b, d_ln_g, d_ln_b):
    p = dict(ffn1_g=ffn1_g, ffn1_w_gu=ffn1_w_gu, ffn1_w_down=ffn1_w_down, mix_g=mix_g,
             ffn2_g=ffn2_g, ffn2_w_gu=ffn2_w_gu, ffn2_w_down=ffn2_w_down,
             even_w_in=even_w_in, even_w_out=even_w_out, a_q_g=a_q_g, a_k_g=a_k_g,
             a_lambda=a_lambda, a_head_g=a_head_g, b_conv_w=b_conv_w, b_conv_b=b_conv_b,
             b_gate_w=b_gate_w, b_gate_b=b_gate_b, b_lambda=b_lambda,
             odd_w_in=odd_w_in, odd_w_out=odd_w_out, c_a_re=c_a_re, c_a_im=c_a_im,
             c_log_dt=c_log_dt, c_b_re=c_b_re, c_b_im=c_b_im, c_c_re=c_c_re, c_c_im=c_c_im,
             c_d=c_d, c_glu_w=c_glu_w, c_glu_b=c_glu_b, d_conv_w=d_conv_w, d_conv_b=d_conv_b,
             d_ln_g=d_ln_g, d_ln_b=d_ln_b)
    b = x_prompt.shape[0]
    dt = x_prompt.dtype
    (y_prompt, k_p, v_p, cb_p, hb_p, cre_p, cim_p, cd_p) = _run_group(
        x_prompt, None, None, None,
        jnp.zeros((N_EVEN, b, B_CONV - 1, B_W), dt), jnp.zeros((N_EVEN, b, B_W), dt),
        jnp.zeros((N_ODD, b, C_GROUPS, C_STATE), dt), jnp.zeros((N_ODD, b, C_GROUPS, C_STATE), dt),
        jnp.zeros((N_ODD, b, D_CONV - 1, D_W), dt), p)
    (y_sample, k_s, v_s, cb_s, hb_s, cre_s, cim_s, cd_s) = _run_group(
        x_sample, cache_k, cache_v, page_table, state_conv_b, state_h_b, state_c_re, state_c_im,
        state_conv_d, p)
    return (y_prompt, y_sample, k_p, v_p, cb_p, hb_p, cre_p, cim_p, cd_p,
            k_s, v_s, cb_s, hb_s, cre_s, cim_s, cd_s)
```

```python
import functools
import math

import jax
import jax.numpy as jnp
from jax import lax
from jax.experimental import pallas as pl
from jax.experimental.pallas import tpu as pltpu

F32 = jnp.float32
BF16 = jnp.bfloat16
EPS = 1e-6
NEG = -1e30

V7X_LANES = 128
V7X_MXU_DIM = 256
V7X_VMEM_LIMIT = 56 * 1024 * 1024

A_HEADS = 4
A_QK = 64
A_V = 128
B_BLOCKS = 8
B_CONV = 4
C_RG = 8.0
C_GROUP = 16
C_STATE = 64
D_CONV = 31


def _cparams(n_axes):
    return pltpu.CompilerParams(dimension_semantics=("arbitrary",) * n_axes,
                                vmem_limit_bytes=V7X_VMEM_LIMIT)


def _const_spec(shape):
    nd = len(shape)
    return pl.BlockSpec(shape, lambda *_: (0,) * nd)


def _rms(x, g):
    return x * lax.rsqrt(jnp.mean(x * x, axis=-1, keepdims=True) + EPS) * g


def _gelu_tanh(x):
    return 0.5 * x * (1.0 + jnp.tanh(math.sqrt(2.0 / math.pi) * (x + 0.044715 * (x * x * x))))


def _dot(a, b):
    return jnp.dot(a, b, preferred_element_type=F32)


def _row_tile(m, want):
    t = min(m, want)
    while m % t:
        t //= 2
    return t


def _ffn_body(x_ref, g_ref, wg_ref, wu_ref, wd_ref, o_ref, *, ck):
    x = x_ref[...]
    h = _rms(x, g_ref[...]).astype(BF16)
    acc = jnp.zeros(x.shape, F32)
    for c in range(wg_ref.shape[1] // ck):
        sl = slice(c * ck, (c + 1) * ck)
        gate = _dot(h, wg_ref[:, sl])
        up = _dot(h, wu_ref[:, sl])
        a = (gate * jax.nn.sigmoid(gate) * up).astype(BF16)
        acc = acc + _dot(a, wd_ref[sl, :])
    o_ref[...] = x + 0.5 * acc


def _prep_ffn(w_gu, w_down):
    f = w_down.shape[0]
    fp = -(-f // V7X_MXU_DIM) * V7X_MXU_DIM
    pad_c = ((0, 0), (0, fp - f))
    wg = jnp.pad(w_gu[:, :f], pad_c).astype(BF16)
    wu = jnp.pad(w_gu[:, f:], pad_c).astype(BF16)
    wd = jnp.pad(w_down, ((0, fp - f), (0, 0))).astype(BF16)
    return wg, wu, wd


def _ffn(x, g, w):
    wg, wu, wd = w
    m, d = x.shape
    fp = wg.shape[1]
    tm = _row_tile(m, 512)
    row = pl.BlockSpec((tm, d), lambda i: (i, 0))
    return pl.pallas_call(
        functools.partial(_ffn_body, ck=V7X_MXU_DIM),
        grid=(m // tm,),
        in_specs=[row, _const_spec((1, d)), _const_spec((d, fp)), _const_spec((d, fp)),
                  _const_spec((fp, d))],
        out_specs=row,
        out_shape=jax.ShapeDtypeStruct((m, d), F32),
        compiler_params=_cparams(1),
        name="ffn",
    )(x, g.reshape(1, d), wg, wu, wd)


def _inproj_even_body(x_ref, g_ref, w_ref, gq_ref, gk_ref, ones_ref,
                      q_ref, k_ref, kb_ref, v_ref, vb_ref, xb_ref, gb_ref, *, hw):
    x = x_ref[...]
    h = _rms(x, g_ref[...]).astype(BF16)
    proj = _dot(h, w_ref[...])

    def group_rms(t, gain):
        ss = t * t
        hi = ss.astype(BF16)
        lo = (ss - hi.astype(F32)).astype(BF16)
        parts = []
        for c in range(hw // V7X_MXU_DIM):
            sl = slice(c * V7X_MXU_DIM, (c + 1) * V7X_MXU_DIM)
            parts.append(_dot(hi[:, sl], ones_ref[...]) + _dot(lo[:, sl], ones_ref[...]))
        gs = jnp.concatenate(parts, axis=-1)
        return t * lax.rsqrt(gs * (1.0 / A_QK) + EPS) * gain

    q = group_rms(proj[:, 0:hw], gq_ref[...]) * (A_QK ** -0.5)
    k = group_rms(proj[:, hw:2 * hw], gk_ref[...])
    v = proj[:, 2 * hw:3 * hw]
    q_ref[...] = q.astype(BF16)
    k_ref[...] = k
    kb_ref[...] = k.astype(BF16)
    v_ref[...] = v
    vb_ref[...] = v.astype(BF16)
    xb_ref[...] = proj[:, 3 * hw:4 * hw]
    gb_ref[...] = proj[:, 4 * hw:5 * hw]


def _inproj_even(x, g, w_in, gq, gk):
    m, d = x.shape
    hw = w_in.shape[1] // 5
    tm = _row_tile(m, 512)
    row = pl.BlockSpec((tm, d), lambda i: (i, 0))
    orow = pl.BlockSpec((tm, hw), lambda i: (i, 0))
    idx = jnp.arange(V7X_MXU_DIM) // A_QK
    ones_bd = (idx[:, None] == idx[None, :]).astype(BF16)
    tile = lambda t: jnp.tile(t, hw // A_QK).reshape(1, hw)
    sds = lambda dt: jax.ShapeDtypeStruct((m, hw), dt)
    return pl.pallas_call(
        functools.partial(_inproj_even_body, hw=hw),
        grid=(m // tm,),
        in_specs=[row, _const_spec((1, d)), _const_spec(w_in.shape), _const_spec((1, hw)),
                  _const_spec((1, hw)), _const_spec(ones_bd.shape)],
        out_specs=[orow] * 7,
        out_shape=[sds(BF16), sds(F32), sds(BF16), sds(F32), sds(BF16), sds(F32), sds(F32)],
        compiler_params=_cparams(1),
        name="inproj_even",
    )(x, g.reshape(1, d), w_in, tile(gq), tile(gk), ones_bd)


def _diff_lambda(lv, lam_init):
    s01 = jnp.sum(lv[0:1, :] * lv[1:2, :], axis=-1, keepdims=True)
    s23 = jnp.sum(lv[2:3, :] * lv[3:4, :], axis=-1, keepdims=True)
    return jnp.exp(s01) - jnp.exp(s23) + lam_init


def _split_maps(q):
    lane = lax.broadcasted_iota(jnp.int32, q.shape, 1)
    zero = jnp.zeros_like(q)
    return jnp.where(lane < A_QK, q, zero), jnp.where(lane >= A_QK, q, zero)


def _scores(q, k):
    return lax.dot_general(q, k, (((1,), (1,)), ((), ())), preferred_element_type=F32)


def _attn_prompt_body(lam_ref, q_ref, k_ref, v_ref, hg_ref, o_ref, *, tq, lam_init):
    qi = pl.program_id(2)
    lam = _diff_lambda(lam_ref[...], lam_init)
    qs = _split_maps(q_ref[...])

    def step(kb, carry, diag):
        k = k_ref[pl.ds(pl.multiple_of(kb * tq, tq), tq), :]
        v = v_ref[pl.ds(pl.multiple_of(kb * tq, tq), tq), :]
        new = []
        for j in range(2):
            m, l, acc = carry[j]
            s = _scores(qs[j], k)
            if diag:
                row = lax.broadcasted_iota(jnp.int32, s.shape, 0)
                col = lax.broadcasted_iota(jnp.int32, s.shape, 1)
                s = jnp.where(col <= row, s, NEG)
            m_new = jnp.maximum(m, jnp.max(s, axis=-1, keepdims=True))
            alpha = jnp.exp(m - m_new)
            p = jnp.exp(s - m_new)
            l = alpha * l + jnp.sum(p, axis=-1, keepdims=True)
            acc = alpha * acc + _dot(p.astype(BF16), v)
            new.append((m_new, l, acc))
        return tuple(new)

    init_one = (jnp.full((tq, 1), NEG, F32), jnp.zeros((tq, 1), F32), jnp.zeros((tq, A_V), F32))
    carry = lax.fori_loop(0, qi, lambda kb, c: step(kb, c, False), (init_one, init_one))
    (_, l0, a0), (_, l1, a1) = step(qi, carry, True)
    o = a0 / l0 - lam * (a1 / l1)
    o_ref[...] = (_rms(o, hg_ref[...]) * (1.0 - lam_init)).astype(o_ref.dtype)


def _attn_prompt(q, k, v, lam_p, head_g, n, t, lam_init):
    hw = q.shape[1]
    heads = hw // A_V
    tq = _row_tile(t, 512)
    nq = t // tq
    qspec = pl.BlockSpec((tq, A_V), lambda b, h, i: (b * nq + i, h))
    kspec = pl.BlockSpec((t, A_V), lambda b, h, i: (b, h))
    return pl.pallas_call(
        functools.partial(_attn_prompt_body, tq=tq, lam_init=lam_init),
        grid=(n, heads, nq),
        in_specs=[_const_spec(lam_p.shape), qspec, kspec, kspec, _const_spec((1, A_V))],
        out_specs=qspec,
        out_shape=jax.ShapeDtypeStruct((n * t, hw), BF16),
        compiler_params=_cparams(3),
        name="attn_prompt",
    )(lam_p, q, k, v, head_g.reshape(1, A_V))


def _attn_sample_body(pt_ref, lam_ref, q_ref, kn_ref, vn_ref, hg_ref, *rest,
                      pps, heads, t, lam_init):
    kp, vp = rest[:pps], rest[pps:2 * pps]
    o_ref = rest[2 * pps]
    m_sc, l_sc, acc_sc = rest[2 * pps + 1:]
    step = pl.program_id(1)
    q = q_ref[0]

    def q_rows(h):
        q0, q1 = _split_maps(q[:, h * A_V:(h + 1) * A_V])
        return jnp.concatenate([q0, q1], axis=0)

    def update(h, s, v):
        m = m_sc[h]
        m_new = jnp.maximum(m, jnp.max(s, axis=-1, keepdims=True))
        alpha = jnp.exp(m - m_new)
        p = jnp.exp(s - m_new)
        l_sc[h] = alpha * l_sc[h] + jnp.sum(p, axis=-1, keepdims=True)
        acc_sc[h] = alpha * acc_sc[h] + _dot(p.astype(BF16), v)
        m_sc[h] = m_new

    @pl.when(step == 0)
    def _():
        m_sc[...] = jnp.full(m_sc.shape, NEG, F32)
        l_sc[...] = jnp.zeros(l_sc.shape, F32)
        acc_sc[...] = jnp.zeros(acc_sc.shape, F32)
        kn, vn = kn_ref[0], vn_ref[0]
        for h in range(heads):
            s = _scores(q_rows(h), kn[:, h * A_V:(h + 1) * A_V])
            row = lax.broadcasted_iota(jnp.int32, s.shape, 0)
            col = lax.broadcasted_iota(jnp.int32, s.shape, 1)
            qpos = jnp.where(row >= t, row - t, row)
            s = jnp.where(col <= qpos, s, NEG)
            update(h, s, vn[:, h * A_V:(h + 1) * A_V])

    kcat = jnp.concatenate([r[0] for r in kp], axis=0).astype(BF16)
    vcat = jnp.concatenate([r[0] for r in vp], axis=0).astype(BF16)
    for h in range(heads):
        s = _scores(q_rows(h), kcat[:, h * A_V:(h + 1) * A_V])
        update(h, s, vcat[:, h * A_V:(h + 1) * A_V])

    @pl.when(step == pl.num_programs(1) - 1)
    def _():
        lam = _diff_lambda(lam_ref[...], lam_init)
        for h in range(heads):
            o = acc_sc[h] / l_sc[h]
            o = o[0:t, :] - lam * o[t:2 * t, :]
            o_ref[0, :, h * A_V:(h + 1) * A_V] = (
                _rms(o, hg_ref[...]) * (1.0 - lam_init)).astype(o_ref.dtype)


def _attn_sample(q, k_new, v_new, cache_k, cache_v, page_table, layer, lam_p, head_g, n, t, lam_init):
    hw = q.shape[1]
    heads = hw // A_V
    n_layers, n_pool, page = cache_k.shape[:3]
    n_pages = page_table.shape[1]
    pps = 8
    while n_pages % pps:
        pps //= 2
    ck = cache_k.reshape(n_layers * n_pool, page, hw)
    cv = cache_v.reshape(n_layers * n_pool, page, hw)
    pad = ((0, 0), (0, V7X_LANES - t), (0, 0))
    kn = jnp.pad(k_new.reshape(n, t, hw), pad)
    vn = jnp.pad(v_new.reshape(n, t, hw), pad)
    seq = lambda rows: pl.BlockSpec((1, rows, hw), lambda b, i, pt: (b, 0, 0))
    base = layer * n_pool

    def page_spec(r):
        return pl.BlockSpec((1, page, hw), lambda b, i, pt: (base + pt[b, i * pps + r], 0, 0))

    grid_spec = pltpu.PrefetchScalarGridSpec(
        num_scalar_prefetch=1,
        grid=(n, n_pages // pps),
        in_specs=[pl.BlockSpec(lam_p.shape, lambda b, i, pt: (0, 0)), seq(t), seq(V7X_LANES),
                  seq(V7X_LANES), pl.BlockSpec((1, A_V), lambda b, i, pt: (0, 0))]
                 + [page_spec(r) for r in range(pps)] * 2,
        out_specs=seq(t),
        scratch_shapes=[pltpu.VMEM((heads, 2 * t, 1), F32), pltpu.VMEM((heads, 2 * t, 1), F32),
                        pltpu.VMEM((heads, 2 * t, A_V), F32)],
    )
    out = pl.pallas_call(
        functools.partial(_attn_sample_body, pps=pps, heads=heads, t=t, lam_init=lam_init),
        grid_spec=grid_spec,
        out_shape=jax.ShapeDtypeStruct((n, t, hw), BF16),
        compiler_params=_cparams(2),
        name="attn_sample",
    )(page_table, lam_p, q.reshape(n, t, hw), kn, vn, head_g.reshape(1, A_V),
      *([ck] * pps), *([cv] * pps))
    return out.reshape(n * t, hw)


def _shift_rows(x, d, fill):
    row = lax.broadcasted_iota(jnp.int32, x.shape, 0)
    return jnp.where(row >= d, pltpu.roll(x, d, 0), fill)


def _linear_scan(a, b):
    d = 1
    while d < a.shape[0]:
        b = b + a * _shift_rows(b, d, 0.0)
        a = a * _shift_rows(a, d, 1.0)
        d *= 2
    return a, b


def _const_complex_scan(vr, vi, pr, pi):
    d = 1
    while d < vr.shape[0]:
        sr, si = _shift_rows(vr, d, 0.0), _shift_rows(vi, d, 0.0)
        vr, vi = vr + (pr * sr - pi * si), vi + (pr * si + pi * sr)
        pr, pi = pr * pr - pi * pi, 2.0 * pr * pi
        d *= 2
    return vr, vi


def _rglru_body(xb_ref, gb_ref, conv0_ref, h0_ref, cw_ref, cb_ref, wg_ref, gbias_ref, lam_ref,
                rec_ref, convn_ref, hl_ref, ext_sc, h_sc, *, tt, w):
    halo = 8
    step = pl.program_id(1)

    @pl.when(step == 0)
    def _():
        ext_sc[0:halo, :] = jnp.zeros((halo, w), F32)
        ext_sc[halo - (B_CONV - 1):halo, :] = conv0_ref[0]
        h_sc[...] = h0_ref[0]

    x = xb_ref[0]
    ext_sc[halo:halo + tt, :] = x
    cw = cw_ref[...]
    xc = cb_ref[...] + x * cw[B_CONV - 1:B_CONV, :]
    for j in range(1, B_CONV):
        xc = xc + ext_sc[halo - j:halo - j + tt, :] * cw[B_CONV - 1 - j:B_CONV - j, :]

    pre = _dot(xc.astype(BF16), wg_ref[...]) + gbias_ref[...]
    r = jax.nn.sigmoid(pre[:, :w])
    i = jax.nn.sigmoid(pre[:, w:])
    lam = lam_ref[...]
    softplus_neg = jnp.maximum(-lam, 0.0) + jnp.log1p(jnp.exp(-jnp.abs(lam)))
    log_a = (-C_RG) * r * softplus_neg
    a = jnp.exp(log_a)
    b = jnp.sqrt(1.0 - a * a) * (i * xc)
    a_cum, h_loc = _linear_scan(a, b)
    h = h_loc + a_cum * h_sc[...]
    rec_ref[0] = (_gelu_tanh(gb_ref[0]) * h).astype(rec_ref.dtype)

    h_sc[...] = h[tt - 1:tt, :]
    hl_ref[0] = h[tt - 1:tt, :]
    convn_ref[0] = ext_sc[halo + tt - (B_CONV - 1):halo + tt, :]
    ext_sc[0:halo, :] = ext_sc[tt:tt + halo, :]


def _prep_gate_w(gate_w):
    _, nb, bs, _ = gate_w.shape
    eye = jnp.eye(nb, dtype=gate_w.dtype)
    dense = jnp.einsum('gbij,bc->gbicj', gate_w, eye).reshape(2, nb * bs, nb * bs)
    return jnp.concatenate([dense[0], dense[1]], axis=1).astype(BF16)


def _rglru(xb, gb, conv0, h0, conv_w, conv_b, gate_wd, gate_b, lam_p, n, t):
    w = xb.shape[1]
    tt = _row_tile(t, 256)
    seq = pl.BlockSpec((1, tt, w), lambda b, i: (b, i, 0))
    per_n = lambda rows: pl.BlockSpec((1, rows, w), lambda b, i: (b, 0, 0))
    return pl.pallas_call(
        functools.partial(_rglru_body, tt=tt, w=w),
        grid=(n, t // tt),
        in_specs=[seq, seq, per_n(B_CONV - 1), per_n(1), _const_spec((B_CONV, w)), _const_spec((1, w)),
                  _const_spec((w, 2 * w)), _const_spec((1, 2 * w)), _const_spec((1, w))],
        out_specs=[seq, per_n(B_CONV - 1), per_n(1)],
        out_shape=[jax.ShapeDtypeStruct((n, t, w), BF16),
                   jax.ShapeDtypeStruct((n, B_CONV - 1, w), F32),
                   jax.ShapeDtypeStruct((n, 1, w), F32)],
        scratch_shapes=[pltpu.VMEM((tt + 8, w), F32), pltpu.VMEM((1, w), F32)],
        compiler_params=_cparams(2),
        name="rglru",
    )(xb.reshape(n, t, w), gb.reshape(n, t, w), conv0, h0.reshape(n, 1, w), conv_w,
      conv_b.reshape(1, w), gate_wd, gate_b.reshape(1, 2 * w), lam_p.reshape(1, w))


def _outproj_body(x_ref, a_ref, b_ref, wa_ref, wb_ref, o_ref):
    o_ref[...] = x_ref[...] + _dot(a_ref[...], wa_ref[...]) + _dot(b_ref[...], wb_ref[...])


def _outproj(x, a, b, w_out):
    m, d = x.shape
    hw = a.shape[1]
    tm = _row_tile(m, 512)
    row = pl.BlockSpec((tm, d), lambda i: (i, 0))
    half = pl.BlockSpec((tm, hw), lambda i: (i, 0))
    return pl.pallas_call(
        _outproj_body,
        grid=(m // tm,),
        in_specs=[row, half, half, _const_spec((hw, d)), _const_spec((hw, d))],
        out_specs=row,
        out_shape=jax.ShapeDtypeStruct((m, d), F32),
        compiler_params=_cparams(1),
        name="outproj",
    )(x, a, b, w_out[:hw], w_out[hw:])


def _inproj_odd_body(x_ref, g_ref, w_ref, u_ref, gin_ref, *, hw):
    h = _rms(x_ref[...], g_ref[...]).astype(BF16)
    proj = _dot(h, w_ref[...])
    u_ref[...] = proj[:, :hw]
    gin_ref[...] = proj[:, hw:2 * hw] * jax.nn.sigmoid(proj[:, 2 * hw:])


def _inproj_odd(x, g, w_in):
    m, d = x.shape
    hw = w_in.shape[1] // 3
    tm = _row_tile(m, 512)
    row = pl.BlockSpec((tm, d), lambda i: (i, 0))
    orow = pl.BlockSpec((tm, hw), lambda i: (i, 0))
    return pl.pallas_call(
        functools.partial(_inproj_odd_body, hw=hw),
        grid=(m // tm,),
        in_specs=[row, _const_spec((1, d)), _const_spec(w_in.shape)],
        out_specs=[orow, orow],
        out_shape=[jax.ShapeDtypeStruct((m, hw), F32)] * 2,
        compiler_params=_cparams(1),
        name="inproj_odd",
    )(x, g.reshape(1, d), w_in)


def _s5_body(u_ref, re0_ref, im0_ref, are_ref, aim_ref, ldt_ref, bre_ref, bim_ref, cre_ref, cim_ref,
             d_ref, gw_ref, gbias_ref, o_ref, sre_ref, sim_ref, st_re, st_im, *, tt, w, sw):
    step = pl.program_id(1)

    @pl.when(step == 0)
    def _():
        st_re[...] = re0_ref[0]
        st_im[...] = im0_ref[0]

    a_re, a_im = are_ref[...], aim_ref[...]
    dt = jnp.exp(ldt_ref[...])
    mag = jnp.exp(dt * a_re)
    ab_re = mag * jnp.cos(dt * a_im)
    ab_im = mag * jnp.sin(dt * a_im)
    den = a_re * a_re + a_im * a_im
    nr = ab_re - 1.0
    coef_re = (nr * a_re + ab_im * a_im) / den
    coef_im = (ab_im * a_re - nr * a_im) / den

    u = u_ref[0]
    ub = u.astype(BF16)
    gpt = V7X_MXU_DIM // C_STATE
    kpt = V7X_MXU_DIM // C_GROUP
    bu_re, bu_im = [], []
    for nt in range(sw // V7X_MXU_DIM):
        kt = (nt * gpt) // kpt
        lhs = ub[:, kt * V7X_MXU_DIM:(kt + 1) * V7X_MXU_DIM]
        rows = slice(kt * V7X_MXU_DIM, (kt + 1) * V7X_MXU_DIM)
        cols = slice(nt * V7X_MXU_DIM, (nt + 1) * V7X_MXU_DIM)
        bu_re.append(_dot(lhs, bre_ref[rows, cols]))
        bu_im.append(_dot(lhs, bim_ref[rows, cols]))
    bu_re = jnp.concatenate(bu_re, axis=-1)
    bu_im = jnp.concatenate(bu_im, axis=-1)
    v_re = coef_re * bu_re - coef_im * bu_im
    v_im = coef_re * bu_im + coef_im * bu_re
    row = lax.broadcasted_iota(jnp.int32, v_re.shape, 0)
    h0r, h0i = st_re[...], st_im[...]
    v_re = v_re + jnp.where(row == 0, ab_re * h0r - ab_im * h0i, 0.0)
    v_im = v_im + jnp.where(row == 0, ab_re * h0i + ab_im * h0r, 0.0)
    s_re, s_im = _const_complex_scan(v_re, v_im, ab_re, ab_im)
    st_re[...] = s_re[tt - 1:tt, :]
    st_im[...] = s_im[tt - 1:tt, :]
    sre_ref[0] = s_re[tt - 1:tt, :]
    sim_ref[0] = s_im[tt - 1:tt, :]

    sb_re, sb_im = s_re.astype(BF16), s_im.astype(BF16)
    ys = []
    for ot in range(w // V7X_MXU_DIM):
        acc = None
        for nt in range(ot * kpt // gpt, (ot + 1) * kpt // gpt):
            rows = slice(nt * V7X_MXU_DIM, (nt + 1) * V7X_MXU_DIM)
            cols = slice(ot * V7X_MXU_DIM, (ot + 1) * V7X_MXU_DIM)
            part = _dot(sb_re[:, rows], cre_ref[rows, cols]) - _dot(sb_im[:, rows], cim_ref[rows, cols])
            acc = part if acc is None else acc + part
        ys.append(acc)
    y = jnp.concatenate(ys, axis=-1) + d_ref[...] * u
    z = _gelu_tanh(y)
    o_ref[0] = (z * jax.nn.sigmoid(_dot(z.astype(BF16), gw_ref[...]) + gbias_ref[...])).astype(o_ref.dtype)


def _prep_s5(b_re, b_im, c_re, c_im):
    g = b_re.shape[0]
    eye = jnp.eye(g, dtype=b_re.dtype)
    bd_in = lambda b: jnp.einsum('gpc,gh->gchp', b, eye).reshape(g * C_GROUP, g * C_STATE).astype(BF16)
    bd_out = lambda c: jnp.einsum('gcp,gh->gphc', c, eye).reshape(g * C_STATE, g * C_GROUP).astype(BF16)
    return bd_in(b_re), bd_in(b_im), bd_out(c_re), bd_out(c_im)


def _s5(u, re0, im0, a_re, a_im, log_dt, mats, d_skip, glu_w, glu_b, n, t):
    w = u.shape[1]
    groups = w // C_GROUP
    sw = groups * C_STATE
    tt = _row_tile(t, 128)
    bre, bim, cre, cim = mats
    seq = pl.BlockSpec((1, tt, w), lambda b, i: (b, i, 0))
    st = pl.BlockSpec((1, 1, sw), lambda b, i: (b, 0, 0))
    return pl.pallas_call(
        functools.partial(_s5_body, tt=tt, w=w, sw=sw),
        grid=(n, t // tt),
        in_specs=[seq, st, st, _const_spec((1, sw)), _const_spec((1, sw)), _const_spec((1, sw)),
                  _const_spec((w, sw)), _const_spec((w, sw)), _const_spec((sw, w)), _const_spec((sw, w)),
                  _const_spec((1, w)), _const_spec((w, w)), _const_spec((1, w))],
        out_specs=[seq, st, st],
        out_shape=[jax.ShapeDtypeStruct((n, t, w), BF16),
                   jax.ShapeDtypeStruct((n, 1, sw), F32), jax.ShapeDtypeStruct((n, 1, sw), F32)],
        scratch_shapes=[pltpu.VMEM((1, sw), F32), pltpu.VMEM((1, sw), F32)],
        compiler_params=_cparams(2),
        name="s5",
    )(u.reshape(n, t, w), re0.reshape(n, 1, sw), im0.reshape(n, 1, sw), a_re.reshape(1, sw),
      a_im.reshape(1, sw), jnp.repeat(log_dt, C_STATE).reshape(1, sw), bre, bim, cre, cim,
      d_skip.reshape(1, w), glu_w.astype(BF16), glu_b.reshape(1, w))


def _conf_body(g_ref, conv0_ref, cw_ref, cb_ref, lng_ref, lnb_ref, o_ref, convn_ref, ext_sc, *, tt, w):
    halo = 32
    hist = D_CONV - 1
    step = pl.program_id(1)

    @pl.when(step == 0)
    def _():
        ext_sc[0:halo, :] = jnp.zeros((halo, w), F32)
        ext_sc[halo - hist:halo, :] = conv0_ref[0]

    ext_sc[halo:halo + tt, :] = g_ref[0]
    cw = cw_ref[...]
    c = cb_ref[...] + jnp.zeros((tt, w), F32)
    for j in range(D_CONV):
        c = c + ext_sc[halo - hist + j:halo - hist + j + tt, :] * cw[j:j + 1, :]
    mu = jnp.mean(c, axis=-1, keepdims=True)
    cc = c - mu
    y = cc * lax.rsqrt(jnp.mean(cc * cc, axis=-1, keepdims=True) + EPS) * lng_ref[...] + lnb_ref[...]
    o_ref[0] = (y * jax.nn.sigmoid(y)).astype(o_ref.dtype)
    convn_ref[0] = ext_sc[halo + tt - hist:halo + tt, :]
    ext_sc[0:halo, :] = ext_sc[tt:tt + halo, :]


def _conf(g_in, conv0, conv_w, conv_b, ln_g, ln_b, n, t):
    w = g_in.shape[1]
    tt = _row_tile(t, 256)
    seq = pl.BlockSpec((1, tt, w), lambda b, i: (b, i, 0))
    hist = pl.BlockSpec((1, D_CONV - 1, w), lambda b, i: (b, 0, 0))
    return pl.pallas_call(
        functools.partial(_conf_body, tt=tt, w=w),
        grid=(n, t // tt),
        in_specs=[seq, hist, _const_spec((D_CONV, w)), _const_spec((1, w)), _const_spec((1, w)),
                  _const_spec((1, w))],
        out_specs=[seq, hist],
        out_shape=[jax.ShapeDtypeStruct((n, t, w), BF16),
                   jax.ShapeDtypeStruct((n, D_CONV - 1, w), F32)],
        scratch_shapes=[pltpu.VMEM((tt + 32, w), F32)],
        compiler_params=_cparams(2),
        name="conformer_conv",
    )(g_in.reshape(n, t, w), conv0, conv_w, conv_b.reshape(1, w), ln_g.reshape(1, w), ln_b.reshape(1, w))


def _run_group(x3, cache, conv_b0, h_b0, c_re0, c_im0, conv_d0, p, prepped):
    n, t, d = x3.shape
    depth = p['ffn1_g'].shape[0]
    x = x3.reshape(n * t, d)
    ks, vs, cbs, hbs, cres, cims, cds = [], [], [], [], [], [], []
    for l in range(depth):
        x = _ffn(x, p['ffn1_g'][l], prepped['ffn1'][l])
        if l % 2 == 0:
            e = l // 2
            hw = p['even_w_in'].shape[2] // 5
            heads = hw // A_V
            q, k, kb, v, vb, xb, gb = _inproj_even(x, p['mix_g'][l], prepped['even_w_in'][e],
                                                   p['a_q_g'][e], p['a_k_g'][e])
            lam_init = 0.8 - 0.6 * math.exp(-0.3 * l)
            if cache is None:
                att = _attn_prompt(q, kb, vb, p['a_lambda'][e], p['a_head_g'][e], n, t, lam_init)
            else:
                cache_k, cache_v, page_table = cache
                att = _attn_sample(q, kb, vb, cache_k, cache_v, page_table, e, p['a_lambda'][e],
                                   p['a_head_g'][e], n, t, lam_init)
            rec, cb, hb = _rglru(xb, gb, conv_b0[e], h_b0[e], p['b_conv_w'][e], p['b_conv_b'][e],
                                 prepped['b_gate_w'][e], p['b_gate_b'][e], p['b_lambda'][e], n, t)
            x = _outproj(x, att, rec.reshape(n * t, -1), prepped['even_w_out'][e])
            ks.append(k.reshape(n, t, heads, 2 * A_QK))
            vs.append(v.reshape(n, t, heads, A_V))
            cbs.append(cb)
            hbs.append(hb.reshape(n, hw))
        else:
            o = l // 2
            u, g_in = _inproj_odd(x, p['mix_g'][l], prepped['odd_w_in'][o])
            groups = u.shape[1] // C_GROUP
            c_out, cre, cim = _s5(u, c_re0[o], c_im0[o], p['c_a_re'][o], p['c_a_im'][o], p['c_log_dt'][o],
                                  prepped['s5'][o], p['c_d'][o], p['c_glu_w'][o], p['c_glu_b'][o], n, t)
            d_out, cd = _conf(g_in, conv_d0[o], p['d_conv_w'][o], p['d_conv_b'][o],
                              p['d_ln_g'][o], p['d_ln_b'][o], n, t)
            x = _outproj(x, c_out.reshape(n * t, -1), d_out.reshape(n * t, -1), prepped['odd_w_out'][o])
            cres.append(cre.reshape(n, groups, C_STATE))
            cims.append(cim.reshape(n, groups, C_STATE))
            cds.append(cd)
        x = _ffn(x, p['ffn2_g'][l], prepped['ffn2'][l])
    return (x.reshape(n, t, d), jnp.stack(ks), jnp.stack(vs), jnp.stack(cbs), jnp.stack(hbs),
            jnp.stack(cres), jnp.stack(cims), jnp.stack(cds))


def kernel(x_prompt, x_sample, cache_k, cache_v, state_conv_b, state_h_b, state_c_re, state_c_im,
           state_conv_d, page_table, ffn1_g, ffn1_w_gu, ffn1_w_down, mix_g, ffn2_g, ffn2_w_gu,
           ffn2_w_down, even_w_in, even_w_out, a_q_g, a_k_g, a_lambda, a_head_g, b_conv_w, b_conv_b,
           b_gate_w, b_gate_b, b_lambda, odd_w_in, odd_w_out, c_a_re, c_a_im, c_log_dt, c_b_re, c_b_im,
           c_c_re, c_c_im, c_d, c_glu_w, c_glu_b, d_conv_w, d_conv_b, d_ln_g, d_ln_b):
    p = dict(ffn1_g=ffn1_g, mix_g=mix_g, ffn2_g=ffn2_g, even_w_in=even_w_in, a_q_g=a_q_g, a_k_g=a_k_g,
             a_lambda=a_lambda, a_head_g=a_head_g, b_conv_w=b_conv_w, b_conv_b=b_conv_b,
             b_gate_b=b_gate_b, b_lambda=b_lambda, c_a_re=c_a_re, c_a_im=c_a_im, c_log_dt=c_log_dt,
             c_d=c_d, c_glu_w=c_glu_w, c_glu_b=c_glu_b, d_conv_w=d_conv_w, d_conv_b=d_conv_b,
             d_ln_g=d_ln_g, d_ln_b=d_ln_b)
    depth = ffn1_g.shape[0]
    n_even, n_odd = even_w_in.shape[0], odd_w_in.shape[0]
    prepped = dict(
        ffn1=[_prep_ffn(ffn1_w_gu[l], ffn1_w_down[l]) for l in range(depth)],
        ffn2=[_prep_ffn(ffn2_w_gu[l], ffn2_w_down[l]) for l in range(depth)],
        even_w_in=[even_w_in[e].astype(BF16) for e in range(n_even)],
        even_w_out=[even_w_out[e].astype(BF16) for e in range(n_even)],
        odd_w_in=[odd_w_in[o].astype(BF16) for o in range(n_odd)],
        odd_w_out=[odd_w_out[o].astype(BF16) for o in range(n_odd)],
        b_gate_w=[_prep_gate_w(b_gate_w[e]) for e in range(n_even)],
        s5=[_prep_s5(c_b_re[o], c_b_im[o], c_c_re[o], c_c_im[o]) for o in range(n_odd)],
    )
    b = x_prompt.shape[0]
    dt = x_prompt.dtype
    hw = even_w_in.shape[2] // 5
    groups, states = state_c_re.shape[2], state_c_re.shape[3]
    prompt = _run_group(
        x_prompt, None,
        jnp.zeros((n_even, b, B_CONV - 1, hw), dt), jnp.zeros((n_even, b, hw), dt),
        jnp.zeros((n_odd, b, groups, states), dt), jnp.zeros((n_odd, b, groups, states), dt),
        jnp.zeros((n_odd, b, D_CONV - 1, hw), dt), p, prepped)
    sample = _run_group(
        x_sample, (cache_k, cache_v, page_table), state_conv_b, state_h_b, state_c_re, state_c_im,
        state_conv_d, p, prepped)
    return (prompt[0], sample[0]) + prompt[1:] + sample[1:]
```

```python
import functools
import math

import jax
import jax.numpy as jnp
from jax import lax
from jax.experimental import pallas as pl
from jax.experimental.pallas import tpu as pltpu

F32 = jnp.float32
BF16 = jnp.bfloat16
EPS = 1e-6
NEG = -1e30
LOG2E = 1.4426950408889634

LANES = 128
SUBL = 8
V7X_MXU_DIM = 256
V7X_VMEM_LIMIT = 56 * 1024 * 1024

A_HEADS = 4
A_QK = 64
A_V = 128
B_BLOCKS = 8
B_CONV = 4
C_RG = 8.0
C_GROUP = 16
C_STATE = 64
D_CONV = 31
S5_SCAN_LANES = 1024


def _cparams(n_axes):
    return pltpu.CompilerParams(dimension_semantics=("arbitrary",) * n_axes,
                                vmem_limit_bytes=V7X_VMEM_LIMIT)


def _const_spec(shape):
    nd = len(shape)
    return pl.BlockSpec(shape, lambda *_: (0,) * nd)


def _rms(x, g):
    return x * lax.rsqrt(jnp.mean(x * x, axis=-1, keepdims=True) + EPS) * g


def _gelu_tanh(x):
    return 0.5 * x * (1.0 + jnp.tanh(math.sqrt(2.0 / math.pi) * (x + 0.044715 * (x * x * x))))


def _dot(a, b):
    return jnp.dot(a, b, preferred_element_type=F32)


def _row_tile(m, want):
    t = min(m, want)
    while m % t:
        t //= 2
    return t


def _ffn_body(x_ref, g_ref, wg_ref, wu_ref, wd_ref, o_ref, *, ck):
    x = x_ref[...]
    h = _rms(x, g_ref[...]).astype(BF16)
    acc = jnp.zeros(x.shape, F32)
    for c in range(wg_ref.shape[1] // ck):
        sl = slice(c * ck, (c + 1) * ck)
        gate = _dot(h, wg_ref[:, sl])
        up = _dot(h, wu_ref[:, sl])
        a = (gate * jax.nn.sigmoid(gate) * up).astype(BF16)
        acc = acc + _dot(a, wd_ref[sl, :])
    o_ref[...] = x + 0.5 * acc


def _prep_ffn(w_gu, w_down):
    f = w_down.shape[0]
    fp = -(-f // V7X_MXU_DIM) * V7X_MXU_DIM
    pad_c = ((0, 0), (0, fp - f))
    wg = jnp.pad(w_gu[:, :f], pad_c).astype(BF16)
    wu = jnp.pad(w_gu[:, f:], pad_c).astype(BF16)
    wd = jnp.pad(w_down, ((0, fp - f), (0, 0))).astype(BF16)
    return wg, wu, wd


def _ffn(x, g, w):
    wg, wu, wd = w
    m, d = x.shape
    fp = wg.shape[1]
    tm = _row_tile(m, 512)
    row = pl.BlockSpec((tm, d), lambda i: (i, 0))
    return pl.pallas_call(
        functools.partial(_ffn_body, ck=V7X_MXU_DIM),
        grid=(m // tm,),
        in_specs=[row, _const_spec((1, d)), _const_spec((d, fp)), _const_spec((d, fp)),
                  _const_spec((fp, d))],
        out_specs=row,
        out_shape=jax.ShapeDtypeStruct((m, d), F32),
        compiler_params=_cparams(1),
        name="ffn",
    )(x, g.reshape(1, d), wg, wu, wd)


def _inproj_even_body(x_ref, g_ref, w_ref, gq_ref, gk_ref, ones_ref,
                      q_ref, k_ref, kb_ref, v_ref, vb_ref, xb_ref, gb_ref, *, hw, tm):
    x = x_ref[...]
    h = _rms(x, g_ref[...]).astype(BF16)
    proj = _dot(h, w_ref[...])

    def group_rms(t, gain):
        ss = t * t
        hi = ss.astype(BF16)
        lo = (ss - hi.astype(F32)).astype(BF16)
        parts = []
        for c in range(hw // V7X_MXU_DIM):
            sl = slice(c * V7X_MXU_DIM, (c + 1) * V7X_MXU_DIM)
            parts.append(_dot(hi[:, sl], ones_ref[...]) + _dot(lo[:, sl], ones_ref[...]))
        gs = jnp.concatenate(parts, axis=-1)
        return t * lax.rsqrt(gs * (1.0 / A_QK) + EPS) * gain

    q = group_rms(proj[:, 0:hw], gq_ref[...]) * (A_QK ** -0.5 * LOG2E)
    k = group_rms(proj[:, hw:2 * hw], gk_ref[...])
    v = proj[:, 2 * hw:3 * hw]
    q_ref[...] = q.astype(BF16)
    kb_ref[...] = k.astype(BF16)
    vb_ref[...] = v.astype(BF16)
    heads = hw // A_V
    for hd in range(heads):
        k_ref[pl.ds(hd, tm, stride=heads), :] = k[:, hd * A_V:(hd + 1) * A_V]
        v_ref[pl.ds(hd, tm, stride=heads), :] = v[:, hd * A_V:(hd + 1) * A_V]
    xb_ref[...] = proj[:, 3 * hw:4 * hw]
    gb_ref[...] = proj[:, 4 * hw:5 * hw]


def _inproj_even(x, g, w_in, gq, gk):
    m, d = x.shape
    hw = w_in.shape[1] // 5
    heads = hw // A_V
    tm = _row_tile(m, 512)
    row = pl.BlockSpec((tm, d), lambda i: (i, 0))
    orow = pl.BlockSpec((tm, hw), lambda i: (i, 0))
    hrow = pl.BlockSpec((tm * heads, A_V), lambda i: (i, 0))
    idx = jnp.arange(V7X_MXU_DIM) // A_QK
    ones_bd = (idx[:, None] == idx[None, :]).astype(BF16)
    tile = lambda t: jnp.tile(t, hw // A_QK).reshape(1, hw)
    sds = lambda dt: jax.ShapeDtypeStruct((m, hw), dt)
    hsds = jax.ShapeDtypeStruct((m * heads, A_V), F32)
    return pl.pallas_call(
        functools.partial(_inproj_even_body, hw=hw, tm=tm),
        grid=(m // tm,),
        in_specs=[row, _const_spec((1, d)), _const_spec(w_in.shape), _const_spec((1, hw)),
                  _const_spec((1, hw)), _const_spec(ones_bd.shape)],
        out_specs=[orow, hrow, orow, hrow, orow, orow, orow],
        out_shape=[sds(BF16), hsds, sds(BF16), hsds, sds(BF16), sds(F32), sds(F32)],
        compiler_params=_cparams(1),
        name="inproj_even",
    )(x, g.reshape(1, d), w_in, tile(gq), tile(gk), ones_bd)


def _diff_lambda(lv, lam_init):
    s01 = jnp.sum(lv[0:1, :] * lv[1:2, :], axis=-1, keepdims=True)
    s23 = jnp.sum(lv[2:3, :] * lv[3:4, :], axis=-1, keepdims=True)
    return jnp.exp(s01) - jnp.exp(s23) + lam_init


def _split_maps(q):
    lane = lax.broadcasted_iota(jnp.int32, q.shape, 1)
    zero = jnp.zeros_like(q)
    return jnp.where(lane < A_QK, q, zero), jnp.where(lane >= A_QK, q, zero)


def _scores(q, k):
    return lax.dot_general(q, k, (((1,), (1,)), ((), ())), preferred_element_type=F32)


def _attn_prompt_body(lam_ref, q_ref, k_ref, v_ref, hg_ref, o_ref, vext_sc, *, tq, hps, lam_init):
    qi = pl.program_id(2)

    @pl.when(qi == 0)
    def _():
        for hh in range(hps):
            vext_sc[:, 2 * hh * A_V:(2 * hh + 1) * A_V] = v_ref[:, hh * A_V:(hh + 1) * A_V]
            vext_sc[:, (2 * hh + 1) * A_V:(2 * hh + 2) * A_V] = jnp.ones((v_ref.shape[0], A_V), BF16)

    lam = _diff_lambda(lam_ref[...], lam_init)
    q = q_ref[...]
    qs = [m for hh in range(hps) for m in _split_maps(q[:, hh * A_V:(hh + 1) * A_V])]

    def step(kb, carry, diag):
        rows = pl.ds(pl.multiple_of(kb * tq, tq), tq)
        new = []
        for c in range(2 * hps):
            hh = c // 2
            m, acc = carry[c]
            s = _scores(qs[c], k_ref[rows, hh * A_V:(hh + 1) * A_V])
            if diag:
                row = lax.broadcasted_iota(jnp.int32, s.shape, 0)
                col = lax.broadcasted_iota(jnp.int32, s.shape, 1)
                s = jnp.where(col <= row, s, NEG)
            m_new = jnp.maximum(m, jnp.max(s, axis=-1, keepdims=True))
            p = jnp.exp2(s - m_new)
            pv = _dot(p.astype(BF16), vext_sc[rows, 2 * hh * A_V:(2 * hh + 2) * A_V])
            new.append((m_new, jnp.exp2(m - m_new) * acc + pv))
        return tuple(new)

    init_one = (jnp.full((tq, 1), NEG, F32), jnp.zeros((tq, 2 * A_V), F32))
    carry = lax.fori_loop(0, qi, lambda kb, c: step(kb, c, False), (init_one,) * (2 * hps))
    final = step(qi, carry, True)
    for hh in range(hps):
        a0, a1 = final[2 * hh][1], final[2 * hh + 1][1]
        o = a0[:, :A_V] / a0[:, A_V:] - lam * (a1[:, :A_V] / a1[:, A_V:])
        o_ref[:, hh * A_V:(hh + 1) * A_V] = (_rms(o, hg_ref[...]) * (1.0 - lam_init)).astype(o_ref.dtype)


def _attn_prompt(q, k, v, lam_p, head_g, n, t, lam_init):
    hw = q.shape[1]
    heads = hw // A_V
    hps = 2 if heads % 2 == 0 else 1
    tq = _row_tile(t, 512)
    nq = t // tq
    qspec = pl.BlockSpec((tq, hps * A_V), lambda b, h, i: (b * nq + i, h))
    kspec = pl.BlockSpec((t, hps * A_V), lambda b, h, i: (b, h))
    return pl.pallas_call(
        functools.partial(_attn_prompt_body, tq=tq, hps=hps, lam_init=lam_init),
        grid=(n, heads // hps, nq),
        in_specs=[_const_spec(lam_p.shape), qspec, kspec, kspec, _const_spec((1, A_V))],
        out_specs=qspec,
        out_shape=jax.ShapeDtypeStruct((n * t, hw), BF16),
        scratch_shapes=[pltpu.VMEM((t, hps * 2 * A_V), BF16)],
        compiler_params=_cparams(3),
        name="attn_prompt",
    )(lam_p, q, k, v, head_g.reshape(1, A_V))


def _attn_sample_body(pt_ref, lam_ref, q_ref, kn_ref, vn_ref, hg_ref, *rest,
                      pps, heads, t, lam_init):
    kp, vp = rest[:pps], rest[pps:2 * pps]
    o_ref = rest[2 * pps]
    bias_sc, m_sc, l_sc, acc_sc = rest[2 * pps + 1:]
    seq, step = pl.program_id(0), pl.program_id(1)
    qrows = 2 * t
    q = q_ref[0]
    qall = jnp.concatenate(
        [piece for h in range(heads) for piece in _split_maps(q[:, h * A_V:(h + 1) * A_V])], axis=0)

    def head_match(shape):
        row = lax.broadcasted_iota(jnp.int32, shape, 0)
        col = lax.broadcasted_iota(jnp.int32, shape, 1)
        return row, col, (col % heads) == (row // qrows)

    def update(s, v):
        m = m_sc[...]
        m_new = jnp.maximum(m, jnp.max(s, axis=-1, keepdims=True))
        alpha = jnp.exp2(m - m_new)
        p = jnp.exp2(s - m_new)
        l_sc[...] = alpha * l_sc[...] + jnp.sum(p, axis=-1, keepdims=True)
        acc_sc[...] = alpha * acc_sc[...] + _dot(p.astype(BF16), v)
        m_sc[...] = m_new

    @pl.when((seq == 0) & (step == 0))
    def _():
        _, _, ok = head_match(bias_sc.shape)
        bias_sc[...] = jnp.where(ok, 0.0, NEG)

    @pl.when(step == 0)
    def _():
        m_sc[...] = jnp.full(m_sc.shape, NEG, F32)
        l_sc[...] = jnp.zeros(l_sc.shape, F32)
        acc_sc[...] = jnp.zeros(acc_sc.shape, F32)
        s = _scores(qall, kn_ref[0])
        row, col, ok = head_match(s.shape)
        ok = ok & ((col // heads) <= (row % t))
        update(jnp.where(ok, s, NEG), vn_ref[0])

    kcat = jnp.concatenate([r[...] for r in kp], axis=0).astype(BF16)
    vcat = jnp.concatenate([r[...] for r in vp], axis=0).astype(BF16)
    update(_scores(qall, kcat) + bias_sc[...], vcat)

    @pl.when(step == pl.num_programs(1) - 1)
    def _():
        lam = _diff_lambda(lam_ref[...], lam_init)
        o = acc_sc[...] / l_sc[...]
        for h in range(heads):
            oh = o[h * qrows:h * qrows + t, :] - lam * o[h * qrows + t:(h + 1) * qrows, :]
            o_ref[0, :, h * A_V:(h + 1) * A_V] = (
                _rms(oh, hg_ref[...]) * (1.0 - lam_init)).astype(o_ref.dtype)


def _attn_sample(q, k_new, v_new, cache_k, cache_v, page_table, layer, lam_p, head_g, n, t, lam_init):
    hw = q.shape[1]
    heads = hw // A_V
    n_layers, n_pool, page = cache_k.shape[:3]
    n_pages = page_table.shape[1]
    prow = page * heads
    pps = 16
    while n_pages % pps:
        pps //= 2
    ck = cache_k.reshape(n_layers * n_pool * prow, A_V)
    cv = cache_v.reshape(n_layers * n_pool * prow, A_V)
    pad = ((0, 0), (0, prow - t * heads), (0, 0))
    kn = jnp.pad(k_new.astype(BF16).reshape(n, t * heads, A_V), pad)
    vn = jnp.pad(v_new.astype(BF16).reshape(n, t * heads, A_V), pad)
    base = layer * n_pool

    def page_spec(r):
        return pl.BlockSpec((prow, A_V), lambda b, i, pt: (base + pt[b, i * pps + r], 0))

    grid_spec = pltpu.PrefetchScalarGridSpec(
        num_scalar_prefetch=1,
        grid=(n, n_pages // pps),
        in_specs=[pl.BlockSpec(lam_p.shape, lambda b, i, pt: (0, 0)),
                  pl.BlockSpec((1, t, hw), lambda b, i, pt: (b, 0, 0)),
                  pl.BlockSpec((1, prow, A_V), lambda b, i, pt: (b, 0, 0)),
                  pl.BlockSpec((1, prow, A_V), lambda b, i, pt: (b, 0, 0)),
                  pl.BlockSpec((1, A_V), lambda b, i, pt: (0, 0))]
                 + [page_spec(r) for r in range(pps)] * 2,
        out_specs=pl.BlockSpec((1, t, hw), lambda b, i, pt: (b, 0, 0)),
        scratch_shapes=[pltpu.VMEM((heads * 2 * t, pps * prow), F32),
                        pltpu.VMEM((heads * 2 * t, 1), F32), pltpu.VMEM((heads * 2 * t, 1), F32),
                        pltpu.VMEM((heads * 2 * t, A_V), F32)],
    )
    out = pl.pallas_call(
        functools.partial(_attn_sample_body, pps=pps, heads=heads, t=t, lam_init=lam_init),
        grid_spec=grid_spec,
        out_shape=jax.ShapeDtypeStruct((n, t, hw), BF16),
        compiler_params=_cparams(2),
        name="attn_sample",
    )(page_table, lam_p, q.reshape(n, t, hw), kn, vn, head_g.reshape(1, A_V),
      *([ck] * pps), *([cv] * pps))
    return out.reshape(n * t, hw)


def _shift_rows(x, d, fill):
    row = lax.broadcasted_iota(jnp.int32, x.shape, 0)
    return jnp.where(row >= d, pltpu.roll(x, d, 0), fill)


def _linear_scan(a, b):
    d = 1
    while d < a.shape[0]:
        b = b + a * _shift_rows(b, d, 0.0)
        a = a * _shift_rows(a, d, 1.0)
        d *= 2
    return a, b


def _rglru_body(xb_ref, gb_ref, conv0_ref, h0_ref, cw_ref, cb_ref, wg_ref, gbias_ref, lam_ref,
                rec_ref, convn_ref, hl_ref, ext_sc, h_sc, *, tt, w):
    halo = 8
    step = pl.program_id(1)

    @pl.when(step == 0)
    def _():
        ext_sc[0:halo, :] = jnp.zeros((halo, w), F32)
        ext_sc[halo - (B_CONV - 1):halo, :] = conv0_ref[0]
        h_sc[...] = h0_ref[0]

    x = xb_ref[0]
    ext_sc[halo:halo + tt, :] = x
    cw = cw_ref[...]
    xc = cb_ref[...] + x * cw[B_CONV - 1:B_CONV, :]
    for j in range(1, B_CONV):
        xc = xc + ext_sc[halo - j:halo - j + tt, :] * cw[B_CONV - 1 - j:B_CONV - j, :]

    pre = _dot(xc.astype(BF16), wg_ref[...]) + gbias_ref[...]
    r = jax.nn.sigmoid(pre[:, :w])
    i = jax.nn.sigmoid(pre[:, w:])
    lam = lam_ref[...]
    softplus_neg = jnp.maximum(-lam, 0.0) + jnp.log1p(jnp.exp(-jnp.abs(lam)))
    log_a = (-C_RG) * r * softplus_neg
    a = jnp.exp(log_a)
    b = jnp.sqrt(1.0 - a * a) * (i * xc)
    a_cum, h_loc = _linear_scan(a, b)
    h = h_loc + a_cum * h_sc[...]
    rec_ref[0] = (_gelu_tanh(gb_ref[0]) * h).astype(rec_ref.dtype)

    h_sc[...] = h[tt - 1:tt, :]
    hl_ref[0] = h[tt - 1:tt, :]
    convn_ref[0] = ext_sc[halo + tt - (B_CONV - 1):halo + tt, :]
    ext_sc[0:halo, :] = ext_sc[tt:tt + halo, :]


def _prep_gate_w(gate_w):
    _, nb, bs, _ = gate_w.shape
    eye = jnp.eye(nb, dtype=gate_w.dtype)
    dense = jnp.einsum('gbij,bc->gbicj', gate_w, eye).reshape(2, nb * bs, nb * bs)
    return jnp.concatenate([dense[0], dense[1]], axis=1).astype(BF16)


def _rglru(xb, gb, conv0, h0, conv_w, conv_b, gate_wd, gate_b, lam_p, n, t):
    w = xb.shape[1]
    tt = _row_tile(t, 256)
    seq = pl.BlockSpec((1, tt, w), lambda b, i: (b, i, 0))
    per_n = lambda rows: pl.BlockSpec((1, rows, w), lambda b, i: (b, 0, 0))
    return pl.pallas_call(
        functools.partial(_rglru_body, tt=tt, w=w),
        grid=(n, t // tt),
        in_specs=[seq, seq, per_n(B_CONV - 1), per_n(1), _const_spec((B_CONV, w)), _const_spec((1, w)),
                  _const_spec((w, 2 * w)), _const_spec((1, 2 * w)), _const_spec((1, w))],
        out_specs=[seq, per_n(B_CONV - 1), per_n(1)],
        out_shape=[jax.ShapeDtypeStruct((n, t, w), BF16),
                   jax.ShapeDtypeStruct((n, B_CONV - 1, w), F32),
                   jax.ShapeDtypeStruct((n, 1, w), F32)],
        scratch_shapes=[pltpu.VMEM((tt + 8, w), F32), pltpu.VMEM((1, w), F32)],
        compiler_params=_cparams(2),
        name="rglru",
    )(xb.reshape(n, t, w), gb.reshape(n, t, w), conv0, h0.reshape(n, 1, w), conv_w,
      conv_b.reshape(1, w), gate_wd, gate_b.reshape(1, 2 * w), lam_p.reshape(1, w))


def _outproj_body(x_ref, a_ref, b_ref, wa_ref, wb_ref, o_ref):
    if len(a_ref.shape) == 3:
        a = jnp.concatenate([a_ref[c] for c in range(a_ref.shape[0])], axis=-1).astype(BF16)
    else:
        a = a_ref[...]
    o_ref[...] = x_ref[...] + _dot(a, wa_ref[...]) + _dot(b_ref[...], wb_ref[...])


def _outproj(x, a, b, w_out):
    m, d = x.shape
    hw = b.shape[1]
    tm = _row_tile(m, 512)
    row = pl.BlockSpec((tm, d), lambda i: (i, 0))
    half = pl.BlockSpec((tm, hw), lambda i: (i, 0))
    a_spec = half if a.ndim == 2 else pl.BlockSpec((a.shape[0], tm, LANES), lambda i: (0, i, 0))
    return pl.pallas_call(
        _outproj_body,
        grid=(m // tm,),
        in_specs=[row, a_spec, half, _const_spec((hw, d)), _const_spec((hw, d))],
        out_specs=row,
        out_shape=jax.ShapeDtypeStruct((m, d), F32),
        compiler_params=_cparams(1),
        name="outproj",
    )(x, a, b, w_out[:hw], w_out[hw:])


def _inproj_odd_body(x_ref, g_ref, w_ref, u_ref, gin_ref, *, hw):
    h = _rms(x_ref[...], g_ref[...]).astype(BF16)
    proj = _dot(h, w_ref[...])
    for c in range(hw // LANES):
        u_ref[c] = proj[:, c * LANES:(c + 1) * LANES]
    gin_ref[...] = proj[:, hw:2 * hw] * jax.nn.sigmoid(proj[:, 2 * hw:])


def _inproj_odd(x, g, w_in):
    m, d = x.shape
    hw = w_in.shape[1] // 3
    tm = _row_tile(m, 512)
    row = pl.BlockSpec((tm, d), lambda i: (i, 0))
    orow = pl.BlockSpec((tm, hw), lambda i: (i, 0))
    ucols = pl.BlockSpec((hw // LANES, tm, LANES), lambda i: (0, i, 0))
    return pl.pallas_call(
        functools.partial(_inproj_odd_body, hw=hw),
        grid=(m // tm,),
        in_specs=[row, _const_spec((1, d)), _const_spec(w_in.shape)],
        out_specs=[ucols, orow],
        out_shape=[jax.ShapeDtypeStruct((hw // LANES, m, LANES), F32),
                   jax.ShapeDtypeStruct((m, hw), F32)],
        compiler_params=_cparams(1),
        name="inproj_odd",
    )(x, g.reshape(1, d), w_in)


def _cmul(ar, ai, br, bi):
    return ar * br - ai * bi, ar * bi + ai * br


def _s5_discretise(a_re, a_im, log_dt):
    dt = jnp.exp(log_dt)
    mag = jnp.exp(dt * a_re)
    ab_re = mag * jnp.cos(dt * a_im)
    ab_im = mag * jnp.sin(dt * a_im)
    den = a_re * a_re + a_im * a_im
    nr = ab_re - 1.0
    return ab_re, ab_im, (nr * a_re + ab_im * a_im) / den, (ab_im * a_re - nr * a_im) / den


def _s5_body(u_ref, re0_ref, im0_ref, are_ref, aim_ref, ldt_ref,
             bre_ref, bim_ref, cre_ref, cim_ref, d_ref, gw_ref, gbias_ref, o_ref, sre_ref, sim_ref,
             st_re, st_im, bf_re, bf_im, lam_re, lam_im, lseg_re, lseg_im, up_sc, v_re, v_im,
             *, tt, w, sw):
    seg = tt // SUBL

    def token_rows(j):
        return pl.ds(j, SUBL, stride=seg)

    def step_rows(j):
        return pl.ds(pl.multiple_of(j * SUBL, SUBL), SUBL)

    @pl.when((pl.program_id(0) == 0) & (pl.program_id(1) == 0))
    def _():
        ab_re, ab_im, coef_re, coef_im = _s5_discretise(are_ref[...], aim_ref[...], ldt_ref[...])
        f_re, f_im = _cmul(coef_re, coef_im, bre_ref[...], bim_ref[...])
        bf_re[...] = f_re.astype(BF16)
        bf_im[...] = f_im.astype(BF16)
        lam_re[...] = jnp.broadcast_to(ab_re, lam_re.shape)
        lam_im[...] = jnp.broadcast_to(ab_im, lam_im.shape)
        pr, pi = ab_re, ab_im
        for _ in range(seg.bit_length() - 1):
            pr, pi = _cmul(pr, pi, pr, pi)
        lseg_re[...] = pr
        lseg_im[...] = pi

    @pl.when(pl.program_id(1) == 0)
    def _():
        st_re[...] = re0_ref[0]
        st_im[...] = im0_ref[0]

    def gather(j, _):
        up_sc[step_rows(j), :] = jnp.concatenate(
            [u_ref[c, 0, token_rows(j), :] for c in range(w // LANES)], axis=-1)
        return 0

    lax.fori_loop(0, seg, gather, 0)
    u = up_sc[...]
    ub = u.astype(BF16)
    gpt = V7X_MXU_DIM // C_STATE
    kpt = V7X_MXU_DIM // C_GROUP
    for nt in range(sw // V7X_MXU_DIM):
        kt = (nt * gpt) // kpt
        lhs = ub[:, kt * V7X_MXU_DIM:(kt + 1) * V7X_MXU_DIM]
        rows = slice(kt * V7X_MXU_DIM, (kt + 1) * V7X_MXU_DIM)
        cols = slice(nt * V7X_MXU_DIM, (nt + 1) * V7X_MXU_DIM)
        v_re[:, cols] = _dot(lhs, bf_re[rows, cols])
        v_im[:, cols] = _dot(lhs, bf_im[rows, cols])

    for ch in range(sw // S5_SCAN_LANES):
        lanes = slice(ch * S5_SCAN_LANES, (ch + 1) * S5_SCAN_LANES)
        l_re, l_im = lam_re[:, lanes], lam_im[:, lanes]

        def advance(j, sr, si, lanes=lanes, l_re=l_re, l_im=l_im):
            pr, pi = _cmul(l_re, l_im, sr, si)
            return pr + v_re[step_rows(j), lanes], pi + v_im[step_rows(j), lanes]

        zero = jnp.zeros((SUBL, S5_SCAN_LANES), F32)
        e_re, e_im = lax.fori_loop(0, seg, lambda j, c, adv=advance: adv(j, *c), (zero, zero))

        row = lax.broadcasted_iota(jnp.int32, zero.shape, 0)
        c_re, c_im = st_re[:, lanes], st_im[:, lanes]
        in_re, in_im = zero, zero
        for k in range(SUBL):
            in_re = jnp.where(row == k, c_re, in_re)
            in_im = jnp.where(row == k, c_im, in_im)
            p_re, p_im = _cmul(lseg_re[:, lanes], lseg_im[:, lanes], c_re, c_im)
            c_re, c_im = p_re + e_re[k:k + 1, :], p_im + e_im[k:k + 1, :]
        st_re[:, lanes] = c_re
        st_im[:, lanes] = c_im
        sre_ref[0, :, lanes] = c_re
        sim_ref[0, :, lanes] = c_im

        def emit(j, c, lanes=lanes, adv=advance):
            sr, si = adv(j, *c)
            v_re[step_rows(j), lanes] = sr
            v_im[step_rows(j), lanes] = si
            return sr, si

        lax.fori_loop(0, seg, emit, (in_re, in_im))

    ys = []
    for ot in range(w // V7X_MXU_DIM):
        acc = None
        for nt in range(ot * kpt // gpt, (ot + 1) * kpt // gpt):
            rows = slice(nt * V7X_MXU_DIM, (nt + 1) * V7X_MXU_DIM)
            cols = slice(ot * V7X_MXU_DIM, (ot + 1) * V7X_MXU_DIM)
            part = (_dot(v_re[:, rows].astype(BF16), cre_ref[rows, cols])
                    - _dot(v_im[:, rows].astype(BF16), cim_ref[rows, cols]))
            acc = part if acc is None else acc + part
        ys.append(acc)
    y = jnp.concatenate(ys, axis=-1) + d_ref[...] * u
    z = _gelu_tanh(y)
    up_sc[...] = z * jax.nn.sigmoid(_dot(z.astype(BF16), gw_ref[...]) + gbias_ref[...])

    def scatter(j, _):
        res = up_sc[step_rows(j), :]
        for c in range(w // LANES):
            o_ref[c, 0, token_rows(j), :] = res[:, c * LANES:(c + 1) * LANES]
        return 0

    lax.fori_loop(0, seg, scatter, 0)


def _prep_s5(b_re, b_im, c_re, c_im):
    g = b_re.shape[0]
    eye = jnp.eye(g, dtype=b_re.dtype)
    bd_in = lambda b: jnp.einsum('gpc,gh->gchp', b, eye).reshape(g * C_GROUP, g * C_STATE)
    bd_out = lambda c: jnp.einsum('gcp,gh->gphc', c, eye).reshape(g * C_STATE, g * C_GROUP).astype(BF16)
    return bd_in(b_re), bd_in(b_im), bd_out(c_re), bd_out(c_im)


def _s5(u, re0, im0, a_re, a_im, log_dt, mats, d_skip, glu_w, glu_b, n, t):
    w = u.shape[0] * LANES
    groups = w // C_GROUP
    sw = groups * C_STATE
    tt = _row_tile(t, 512)
    bre, bim, cre, cim = mats
    wc = w // LANES
    seq = pl.BlockSpec((wc, 1, tt, LANES), lambda b, i: (0, b, i, 0))
    st = pl.BlockSpec((1, 1, sw), lambda b, i: (b, 0, 0))
    vm = lambda shape, dt: pltpu.VMEM(shape, dt)
    flat = lambda a: a.reshape(1, sw)
    out, s_re, s_im = pl.pallas_call(
        functools.partial(_s5_body, tt=tt, w=w, sw=sw),
        grid=(n, t // tt),
        in_specs=[seq, st, st] + [_const_spec((1, sw))] * 3
                 + [_const_spec((w, sw)), _const_spec((w, sw)), _const_spec((sw, w)), _const_spec((sw, w)),
                    _const_spec((1, w)), _const_spec((w, w)), _const_spec((1, w))],
        out_specs=[seq, st, st],
        out_shape=[jax.ShapeDtypeStruct((wc, n, t, LANES), F32),
                   jax.ShapeDtypeStruct((n, 1, sw), F32), jax.ShapeDtypeStruct((n, 1, sw), F32)],
        scratch_shapes=[vm((1, sw), F32), vm((1, sw), F32), vm((w, sw), BF16), vm((w, sw), BF16),
                        vm((SUBL, sw), F32), vm((SUBL, sw), F32), vm((1, sw), F32), vm((1, sw), F32),
                        vm((tt, w), F32), vm((tt, sw), F32), vm((tt, sw), F32)],
        compiler_params=_cparams(2),
        name="s5",
    )(u.reshape(wc, n, t, LANES), re0.reshape(n, 1, sw), im0.reshape(n, 1, sw),
      flat(a_re), flat(a_im), flat(jnp.repeat(log_dt, C_STATE)), bre, bim, cre, cim,
      d_skip.reshape(1, w), glu_w.astype(BF16), glu_b.reshape(1, w))
    return out.reshape(wc, n * t, LANES), s_re, s_im


def _conf_body(g_ref, conv0_ref, cw_ref, cb_ref, lng_ref, lnb_ref, o_ref, convn_ref, ext_sc, *, tt, w):
    halo = 32
    hist = D_CONV - 1
    step = pl.program_id(1)

    @pl.when(step == 0)
    def _():
        ext_sc[0:halo, :] = jnp.zeros((halo, w), F32)
        ext_sc[halo - hist:halo, :] = conv0_ref[0]

    ext_sc[halo:halo + tt, :] = g_ref[0]
    cw = cw_ref[...]
    c = cb_ref[...] + jnp.zeros((tt, w), F32)
    for j in range(D_CONV):
        c = c + ext_sc[halo - hist + j:halo - hist + j + tt, :] * cw[j:j + 1, :]
    mu = jnp.mean(c, axis=-1, keepdims=True)
    cc = c - mu
    y = cc * lax.rsqrt(jnp.mean(cc * cc, axis=-1, keepdims=True) + EPS) * lng_ref[...] + lnb_ref[...]
    o_ref[0] = (y * jax.nn.sigmoid(y)).astype(o_ref.dtype)
    convn_ref[0] = ext_sc[halo + tt - hist:halo + tt, :]
    ext_sc[0:halo, :] = ext_sc[tt:tt + halo, :]


def _conf(g_in, conv0, conv_w, conv_b, ln_g, ln_b, n, t):
    w = g_in.shape[1]
    tt = _row_tile(t, 256)
    seq = pl.BlockSpec((1, tt, w), lambda b, i: (b, i, 0))
    hist = pl.BlockSpec((1, D_CONV - 1, w), lambda b, i: (b, 0, 0))
    return pl.pallas_call(
        functools.partial(_conf_body, tt=tt, w=w),
        grid=(n, t // tt),
        in_specs=[seq, hist, _const_spec((D_CONV, w)), _const_spec((1, w)), _const_spec((1, w)),
                  _const_spec((1, w))],
        out_specs=[seq, hist],
        out_shape=[jax.ShapeDtypeStruct((n, t, w), BF16),
                   jax.ShapeDtypeStruct((n, D_CONV - 1, w), F32)],
        scratch_shapes=[pltpu.VMEM((tt + 32, w), F32)],
        compiler_params=_cparams(2),
        name="conformer_conv",
    )(g_in.reshape(n, t, w), conv0, conv_w, conv_b.reshape(1, w), ln_g.reshape(1, w), ln_b.reshape(1, w))


def _run_group(x3, cache, conv_b0, h_b0, c_re0, c_im0, conv_d0, p, prepped):
    n, t, d = x3.shape
    depth = p['ffn1_g'].shape[0]
    x = x3.reshape(n * t, d)
    ks, vs, cbs, hbs, cres, cims, cds = [], [], [], [], [], [], []
    for l in range(depth):
        x = _ffn(x, p['ffn1_g'][l], prepped['ffn1'][l])
        if l % 2 == 0:
            e = l // 2
            hw = p['even_w_in'].shape[2] // 5
            heads = hw // A_V
            q, k, kb, v, vb, xb, gb = _inproj_even(x, p['mix_g'][l], prepped['even_w_in'][e],
                                                   p['a_q_g'][e], p['a_k_g'][e])
            lam_init = 0.8 - 0.6 * math.exp(-0.3 * l)
            if cache is None:
                att = _attn_prompt(q, kb, vb, p['a_lambda'][e], p['a_head_g'][e], n, t, lam_init)
            else:
                cache_k, cache_v, page_table = cache
                att = _attn_sample(q, k, v, cache_k, cache_v, page_table, e, p['a_lambda'][e],
                                   p['a_head_g'][e], n, t, lam_init)
            rec, cb, hb = _rglru(xb, gb, conv_b0[e], h_b0[e], p['b_conv_w'][e], p['b_conv_b'][e],
                                 prepped['b_gate_w'][e], p['b_gate_b'][e], p['b_lambda'][e], n, t)
            x = _outproj(x, att, rec.reshape(n * t, -1), prepped['even_w_out'][e])
            ks.append(k.reshape(n, t, heads, 2 * A_QK))
            vs.append(v.reshape(n, t, heads, A_V))
            cbs.append(cb)
            hbs.append(hb.reshape(n, hw))
        else:
            o = l // 2
            u, g_in = _inproj_odd(x, p['mix_g'][l], prepped['odd_w_in'][o])
            groups = u.shape[0] * LANES // C_GROUP
            c_out, cre, cim = _s5(u, c_re0[o], c_im0[o], p['c_a_re'][o], p['c_a_im'][o], p['c_log_dt'][o],
                                  prepped['s5'][o], p['c_d'][o], p['c_glu_w'][o], p['c_glu_b'][o], n, t)
            d_out, cd = _conf(g_in, conv_d0[o], p['d_conv_w'][o], p['d_conv_b'][o],
                              p['d_ln_g'][o], p['d_ln_b'][o], n, t)
            x = _outproj(x, c_out, d_out.reshape(n * t, -1), prepped['odd_w_out'][o])
            cres.append(cre.reshape(n, groups, C_STATE))
            cims.append(cim.reshape(n, groups, C_STATE))
            cds.append(cd)
        x = _ffn(x, p['ffn2_g'][l], prepped['ffn2'][l])
    return (x.reshape(n, t, d), jnp.stack(ks), jnp.stack(vs), jnp.stack(cbs), jnp.stack(hbs),
            jnp.stack(cres), jnp.stack(cims), jnp.stack(cds))


def kernel(x_prompt, x_sample, cache_k, cache_v, state_conv_b, state_h_b, state_c_re, state_c_im,
           state_conv_d, page_table, ffn1_g, ffn1_w_gu, ffn1_w_down, mix_g, ffn2_g, ffn2_w_gu,
           ffn2_w_down, even_w_in, even_w_out, a_q_g, a_k_g, a_lambda, a_head_g, b_conv_w, b_conv_b,
           b_gate_w, b_gate_b, b_lambda, odd_w_in, odd_w_out, c_a_re, c_a_im, c_log_dt, c_b_re, c_b_im,
           c_c_re, c_c_im, c_d, c_glu_w, c_glu_b, d_conv_w, d_conv_b, d_ln_g, d_ln_b):
    p = dict(ffn1_g=ffn1_g, mix_g=mix_g, ffn2_g=ffn2_g, even_w_in=even_w_in, a_q_g=a_q_g, a_k_g=a_k_g,
             a_lambda=a_lambda, a_head_g=a_head_g, b_conv_w=b_conv_w, b_conv_b=b_conv_b,
             b_gate_b=b_gate_b, b_lambda=b_lambda, c_a_re=c_a_re, c_a_im=c_a_im, c_log_dt=c_log_dt,
             c_d=c_d, c_glu_w=c_glu_w, c_glu_b=c_glu_b, d_conv_w=d_conv_w, d_conv_b=d_conv_b,
             d_ln_g=d_ln_g, d_ln_b=d_ln_b)
    depth = ffn1_g.shape[0]
    n_even, n_odd = even_w_in.shape[0], odd_w_in.shape[0]
    prepped = dict(
        ffn1=[_prep_ffn(ffn1_w_gu[l], ffn1_w_down[l]) for l in range(depth)],
        ffn2=[_prep_ffn(ffn2_w_gu[l], ffn2_w_down[l]) for l in range(depth)],
        even_w_in=[even_w_in[e].astype(BF16) for e in range(n_even)],
        even_w_out=[even_w_out[e].astype(BF16) for e in range(n_even)],
        odd_w_in=[odd_w_in[o].astype(BF16) for o in range(n_odd)],
        odd_w_out=[odd_w_out[o].astype(BF16) for o in range(n_odd)],
        b_gate_w=[_prep_gate_w(b_gate_w[e]) for e in range(n_even)],
        s5=[_prep_s5(c_b_re[o], c_b_im[o], c_c_re[o], c_c_im[o]) for o in range(n_odd)],
    )
    b = x_prompt.shape[0]
    dt = x_prompt.dtype
    hw = even_w_in.shape[2] // 5
    groups, states = state_c_re.shape[2], state_c_re.shape[3]
    prompt = _run_group(
        x_prompt, None,
        jnp.zeros((n_even, b, B_CONV - 1, hw), dt), jnp.zeros((n_even, b, hw), dt),
        jnp.zeros((n_odd, b, groups, states), dt), jnp.zeros((n_odd, b, groups, states), dt),
        jnp.zeros((n_odd, b, D_CONV - 1, hw), dt), p, prepped)
    sample = _run_group(
        x_sample, (cache_k, cache_v, page_table), state_conv_b, state_h_b, state_c_re, state_c_im,
        state_conv_d, p, prepped)
    return (prompt[0], sample[0]) + prompt[1:] + sample[1:]
```

```python
import functools
import math

import jax
import jax.numpy as jnp
from jax import lax
from jax.experimental import pallas as pl
from jax.experimental.pallas import tpu as pltpu

F32 = jnp.float32
BF16 = jnp.bfloat16
EPS = 1e-6
NEG = -1e30
LOG2E = 1.4426950408889634
ATTN_SCORE_BOUND = 64.0

LANES = 128
SUBL = 8
V7X_MXU_DIM = 256
V7X_VMEM_LIMIT = 56 * 1024 * 1024

A_HEADS = 4
A_QK = 64
A_V = 128
B_BLOCKS = 8
B_CONV = 4
C_RG = 8.0
C_GROUP = 16
C_STATE = 64
D_CONV = 31
S5_SCAN_LANES = 1024


def _cparams(n_axes):
    return pltpu.CompilerParams(dimension_semantics=("arbitrary",) * n_axes,
                                vmem_limit_bytes=V7X_VMEM_LIMIT)


def _const_spec(shape):
    nd = len(shape)
    return pl.BlockSpec(shape, lambda *_: (0,) * nd)


def _rms(x, g):
    return x * lax.rsqrt(jnp.mean(x * x, axis=-1, keepdims=True) + EPS) * g


def _gelu_tanh(x):
    return 0.5 * x * (1.0 + jnp.tanh(math.sqrt(2.0 / math.pi) * (x + 0.044715 * (x * x * x))))


def _dot(a, b):
    return jnp.dot(a, b, preferred_element_type=F32)


def _row_tile(m, want):
    t = min(m, want)
    while m % t:
        t //= 2
    return t


def _mixer_halves(a_ref, b_ref):
    if len(a_ref.shape) == 3:
        a = jnp.concatenate([a_ref[c] for c in range(a_ref.shape[0])], axis=-1).astype(BF16)
    else:
        a = a_ref[...]
    return a, b_ref[...]


def _ffn_body(x_ref, g_ref, wg_ref, wu_ref, wd_ref, *rest, ck, with_mixer):
    x = x_ref[...]
    if with_mixer:
        a_ref, b_ref, wa_ref, wb_ref, o_ref = rest
        a, b = _mixer_halves(a_ref, b_ref)
        x = x + _dot(a, wa_ref[...]) + _dot(b, wb_ref[...])
    else:
        o_ref, = rest
    h = _rms(x, g_ref[...]).astype(BF16)
    acc = jnp.zeros(x.shape, F32)
    for c in range(wg_ref.shape[1] // ck):
        sl = slice(c * ck, (c + 1) * ck)
        gate = _dot(h, wg_ref[:, sl])
        up = _dot(h, wu_ref[:, sl])
        a = (gate * jax.nn.sigmoid(gate) * up).astype(BF16)
        acc = acc + _dot(a, wd_ref[sl, :])
    o_ref[...] = x + 0.5 * acc


def _prep_ffn(w_gu, w_down):
    f = w_down.shape[0]
    fp = -(-f // V7X_MXU_DIM) * V7X_MXU_DIM
    pad_c = ((0, 0), (0, fp - f))
    wg = jnp.pad(w_gu[:, :f], pad_c).astype(BF16)
    wu = jnp.pad(w_gu[:, f:], pad_c).astype(BF16)
    wd = jnp.pad(w_down, ((0, fp - f), (0, 0))).astype(BF16)
    return wg, wu, wd


def _ffn(x, g, w, mixer=None):
    wg, wu, wd = w
    m, d = x.shape
    fp = wg.shape[1]
    tm = _row_tile(m, 512)
    row = pl.BlockSpec((tm, d), lambda i: (i, 0))
    in_specs = [row, _const_spec((1, d)), _const_spec((d, fp)), _const_spec((d, fp)), _const_spec((fp, d))]
    args = [x, g.reshape(1, d), wg, wu, wd]
    if mixer is not None:
        a, b, w_out = mixer
        hw = b.shape[1]
        half = pl.BlockSpec((tm, hw), lambda i: (i, 0))
        a_spec = half if a.ndim == 2 else pl.BlockSpec((a.shape[0], tm, LANES), lambda i: (0, i, 0))
        in_specs += [a_spec, half, _const_spec((hw, d)), _const_spec((hw, d))]
        args += [a, b, w_out[:hw], w_out[hw:]]
    return pl.pallas_call(
        functools.partial(_ffn_body, ck=V7X_MXU_DIM, with_mixer=mixer is not None),
        grid=(m // tm,),
        in_specs=in_specs,
        out_specs=row,
        out_shape=jax.ShapeDtypeStruct((m, d), F32),
        compiler_params=_cparams(1),
        name="ffn_mix" if mixer is not None else "ffn",
    )(*args)


def _inproj_even_body(x_ref, g_ref, w_ref, gq_ref, gk_ref, ones_ref,
                      q_ref, k_ref, kb_ref, v_ref, vb_ref, xb_ref, gb_ref, *, hw, tm):
    x = x_ref[...]
    h = _rms(x, g_ref[...]).astype(BF16)
    proj = _dot(h, w_ref[...])

    def group_rms(t, gain):
        ss = t * t
        hi = ss.astype(BF16)
        lo = (ss - hi.astype(F32)).astype(BF16)
        parts = []
        for c in range(hw // V7X_MXU_DIM):
            sl = slice(c * V7X_MXU_DIM, (c + 1) * V7X_MXU_DIM)
            parts.append(_dot(hi[:, sl], ones_ref[...]) + _dot(lo[:, sl], ones_ref[...]))
        gs = jnp.concatenate(parts, axis=-1)
        return t * lax.rsqrt(gs * (1.0 / A_QK) + EPS) * gain

    q = group_rms(proj[:, 0:hw], gq_ref[...]) * (A_QK ** -0.5 * LOG2E)
    k = group_rms(proj[:, hw:2 * hw], gk_ref[...])
    v = proj[:, 2 * hw:3 * hw]
    q_ref[...] = q.astype(BF16)
    kb_ref[...] = k.astype(BF16)
    vb_ref[...] = v.astype(BF16)
    heads = hw // A_V
    for hd in range(heads):
        k_ref[pl.ds(hd, tm, stride=heads), :] = k[:, hd * A_V:(hd + 1) * A_V]
        v_ref[pl.ds(hd, tm, stride=heads), :] = v[:, hd * A_V:(hd + 1) * A_V]
    xb_ref[...] = proj[:, 3 * hw:4 * hw]
    gb_ref[...] = proj[:, 4 * hw:5 * hw]


def _inproj_even(x, g, w_in, gq, gk):
    m, d = x.shape
    hw = w_in.shape[1] // 5
    heads = hw // A_V
    tm = _row_tile(m, 512)
    row = pl.BlockSpec((tm, d), lambda i: (i, 0))
    orow = pl.BlockSpec((tm, hw), lambda i: (i, 0))
    hrow = pl.BlockSpec((tm * heads, A_V), lambda i: (i, 0))
    idx = jnp.arange(V7X_MXU_DIM) // A_QK
    ones_bd = (idx[:, None] == idx[None, :]).astype(BF16)
    tile = lambda t: jnp.tile(t, hw // A_QK).reshape(1, hw)
    sds = lambda dt: jax.ShapeDtypeStruct((m, hw), dt)
    hsds = jax.ShapeDtypeStruct((m * heads, A_V), F32)
    return pl.pallas_call(
        functools.partial(_inproj_even_body, hw=hw, tm=tm),
        grid=(m // tm,),
        in_specs=[row, _const_spec((1, d)), _const_spec(w_in.shape), _const_spec((1, hw)),
                  _const_spec((1, hw)), _const_spec(ones_bd.shape)],
        out_specs=[orow, hrow, orow, hrow, orow, orow, orow],
        out_shape=[sds(BF16), hsds, sds(BF16), hsds, sds(BF16), sds(F32), sds(F32)],
        compiler_params=_cparams(1),
        name="inproj_even",
    )(x, g.reshape(1, d), w_in, tile(gq), tile(gk), ones_bd)


def _diff_lambda(lv, lam_init):
    s01 = jnp.sum(lv[0:1, :] * lv[1:2, :], axis=-1, keepdims=True)
    s23 = jnp.sum(lv[2:3, :] * lv[3:4, :], axis=-1, keepdims=True)
    return jnp.exp(s01) - jnp.exp(s23) + lam_init


def _split_maps(q):
    lane = lax.broadcasted_iota(jnp.int32, q.shape, 1)
    zero = jnp.zeros_like(q)
    return jnp.where(lane < A_QK, q, zero), jnp.where(lane >= A_QK, q, zero)


def _scores(q, k):
    return lax.dot_general(q, k, (((1,), (1,)), ((), ())), preferred_element_type=F32)


def _attn_prompt_body(lam_ref, q_ref, k_ref, v_ref, hg_ref, o_ref, vext_sc, kn2_sc,
                      *, tq, hps, lam_init):
    qi = pl.program_id(2)

    def max_row_norm2(x):
        xf = x.astype(F32)
        per_head = [jnp.sum(xf[:, hh * A_V:(hh + 1) * A_V] ** 2, axis=-1, keepdims=True)
                    for hh in range(hps)]
        return jnp.max(functools.reduce(jnp.maximum, per_head), axis=0, keepdims=True)

    @pl.when(qi == 0)
    def _():
        for hh in range(hps):
            vext_sc[:, 2 * hh * A_V:(2 * hh + 1) * A_V] = v_ref[:, hh * A_V:(hh + 1) * A_V]
            vext_sc[:, (2 * hh + 1) * A_V:(2 * hh + 2) * A_V] = jnp.ones((v_ref.shape[0], A_V), BF16)
        kn2_sc[...] = max_row_norm2(k_ref[...])

    lam = _diff_lambda(lam_ref[...], lam_init)
    q = q_ref[...]
    qs = [m for hh in range(hps) for m in _split_maps(q[:, hh * A_V:(hh + 1) * A_V])]

    def chain_inputs(kb, c):
        rows = pl.ds(pl.multiple_of(kb * tq, tq), tq)
        hh = c // 2
        s = _scores(qs[c], k_ref[rows, hh * A_V:(hh + 1) * A_V])
        return s, vext_sc[rows, 2 * hh * A_V:(2 * hh + 2) * A_V]

    def causal(shape):
        return lax.broadcasted_iota(jnp.int32, shape, 1) <= lax.broadcasted_iota(jnp.int32, shape, 0)

    def finish(accs):
        for hh in range(hps):
            a0, a1 = accs[2 * hh], accs[2 * hh + 1]
            o = a0[:, :A_V] / a0[:, A_V:] - lam * (a1[:, :A_V] / a1[:, A_V:])
            o_ref[:, hh * A_V:(hh + 1) * A_V] = (
                _rms(o, hg_ref[...]) * (1.0 - lam_init)).astype(o_ref.dtype)

    def bounded():
        def step(kb, accs, diag):
            new = []
            for c in range(2 * hps):
                s, v = chain_inputs(kb, c)
                p = jnp.exp2(s)
                if diag:
                    p = jnp.where(causal(p.shape), p, 0.0)
                new.append(accs[c] + _dot(p.astype(BF16), v))
            return tuple(new)

        zero = jnp.zeros((tq, 2 * A_V), F32)
        accs = lax.fori_loop(0, qi, lambda kb, a: step(kb, a, False), (zero,) * (2 * hps))
        finish(step(qi, accs, True))

    def general():
        def step(kb, carry, diag):
            new = []
            for c in range(2 * hps):
                m, acc = carry[c]
                s, v = chain_inputs(kb, c)
                if diag:
                    s = jnp.where(causal(s.shape), s, NEG)
                m_new = jnp.maximum(m, jnp.max(s, axis=-1, keepdims=True))
                p = jnp.exp2(s - m_new)
                new.append((m_new, jnp.exp2(m - m_new) * acc + _dot(p.astype(BF16), v)))
            return tuple(new)

        init_one = (jnp.full((tq, 1), NEG, F32), jnp.zeros((tq, 2 * A_V), F32))
        carry = lax.fori_loop(0, qi, lambda kb, c: step(kb, c, False), (init_one,) * (2 * hps))
        finish([acc for _, acc in step(qi, carry, True)])

    bound2 = max_row_norm2(q) * kn2_sc[...]
    lax.cond(bound2[0, 0] <= ATTN_SCORE_BOUND ** 2, bounded, general)


def _attn_prompt(q, k, v, lam_p, head_g, n, t, lam_init):
    hw = q.shape[1]
    heads = hw // A_V
    hps = next(c for c in (4, 2, 1) if heads % c == 0)
    tq = _row_tile(t, 512)
    nq = t // tq
    qspec = pl.BlockSpec((tq, hps * A_V), lambda b, h, i: (b * nq + i, h))
    kspec = pl.BlockSpec((t, hps * A_V), lambda b, h, i: (b, h))
    return pl.pallas_call(
        functools.partial(_attn_prompt_body, tq=tq, hps=hps, lam_init=lam_init),
        grid=(n, heads // hps, nq),
        in_specs=[_const_spec(lam_p.shape), qspec, kspec, kspec, _const_spec((1, A_V))],
        out_specs=qspec,
        out_shape=jax.ShapeDtypeStruct((n * t, hw), BF16),
        scratch_shapes=[pltpu.VMEM((t, hps * 2 * A_V), BF16), pltpu.VMEM((1, 1), F32)],
        compiler_params=_cparams(3),
        name="attn_prompt",
    )(lam_p, q, k, v, head_g.reshape(1, A_V))


def _attn_sample_body(pt_ref, lam_ref, q_ref, kn_ref, vn_ref, hg_ref, *rest,
                      pps, heads, t, lam_init):
    kp, vp = rest[:pps], rest[pps:2 * pps]
    o_ref = rest[2 * pps]
    bias_sc, m_sc, l_sc, acc_sc = rest[2 * pps + 1:]
    seq, step = pl.program_id(0), pl.program_id(1)
    qrows = 2 * t
    q = q_ref[0]
    qall = jnp.concatenate(
        [piece for h in range(heads) for piece in _split_maps(q[:, h * A_V:(h + 1) * A_V])], axis=0)

    def head_match(shape):
        row = lax.broadcasted_iota(jnp.int32, shape, 0)
        col = lax.broadcasted_iota(jnp.int32, shape, 1)
        return row, col, (col % heads) == (row // qrows)

    def update(s, v):
        m = m_sc[...]
        m_new = jnp.maximum(m, jnp.max(s, axis=-1, keepdims=True))
        alpha = jnp.exp2(m - m_new)
        p = jnp.exp2(s - m_new)
        l_sc[...] = alpha * l_sc[...] + jnp.sum(p, axis=-1, keepdims=True)
        acc_sc[...] = alpha * acc_sc[...] + _dot(p.astype(BF16), v)
        m_sc[...] = m_new

    @pl.when((seq == 0) & (step == 0))
    def _():
        _, _, ok = head_match(bias_sc.shape)
        bias_sc[...] = jnp.where(ok, 0.0, NEG)

    @pl.when(step == 0)
    def _():
        m_sc[...] = jnp.full(m_sc.shape, NEG, F32)
        l_sc[...] = jnp.zeros(l_sc.shape, F32)
        acc_sc[...] = jnp.zeros(acc_sc.shape, F32)
        s = _scores(qall, kn_ref[0])
        row, col, ok = head_match(s.shape)
        ok = ok & ((col // heads) <= (row % t))
        update(jnp.where(ok, s, NEG), vn_ref[0])

    kcat = jnp.concatenate([r[...] for r in kp], axis=0).astype(BF16)
    vcat = jnp.concatenate([r[...] for r in vp], axis=0).astype(BF16)
    update(_scores(qall, kcat) + bias_sc[...], vcat)

    @pl.when(step == pl.num_programs(1) - 1)
    def _():
        lam = _diff_lambda(lam_ref[...], lam_init)
        o = acc_sc[...] / l_sc[...]
        for h in range(heads):
            oh = o[h * qrows:h * qrows + t, :] - lam * o[h * qrows + t:(h + 1) * qrows, :]
            o_ref[0, :, h * A_V:(h + 1) * A_V] = (
                _rms(oh, hg_ref[...]) * (1.0 - lam_init)).astype(o_ref.dtype)


def _attn_sample(q, k_new, v_new, cache_k, cache_v, page_table, layer, lam_p, head_g, n, t, lam_init):
    hw = q.shape[1]
    heads = hw // A_V
    n_layers, n_pool, page = cache_k.shape[:3]
    n_pages = page_table.shape[1]
    prow = page * heads
    pps = 16
    while n_pages % pps:
        pps //= 2
    ck = cache_k.reshape(n_layers * n_pool * prow, A_V)
    cv = cache_v.reshape(n_layers * n_pool * prow, A_V)
    pad = ((0, 0), (0, prow - t * heads), (0, 0))
    kn = jnp.pad(k_new.astype(BF16).reshape(n, t * heads, A_V), pad)
    vn = jnp.pad(v_new.astype(BF16).reshape(n, t * heads, A_V), pad)
    base = layer * n_pool

    def page_spec(r):
        return pl.BlockSpec((prow, A_V), lambda b, i, pt: (base + pt[b, i * pps + r], 0))

    grid_spec = pltpu.PrefetchScalarGridSpec(
        num_scalar_prefetch=1,
        grid=(n, n_pages // pps),
        in_specs=[pl.BlockSpec(lam_p.shape, lambda b, i, pt: (0, 0)),
                  pl.BlockSpec((1, t, hw), lambda b, i, pt: (b, 0, 0)),
                  pl.BlockSpec((1, prow, A_V), lambda b, i, pt: (b, 0, 0)),
                  pl.BlockSpec((1, prow, A_V), lambda b, i, pt: (b, 0, 0)),
                  pl.BlockSpec((1, A_V), lambda b, i, pt: (0, 0))]
                 + [page_spec(r) for r in range(pps)] * 2,
        out_specs=pl.BlockSpec((1, t, hw), lambda b, i, pt: (b, 0, 0)),
        scratch_shapes=[pltpu.VMEM((heads * 2 * t, pps * prow), F32),
                        pltpu.VMEM((heads * 2 * t, 1), F32), pltpu.VMEM((heads * 2 * t, 1), F32),
                        pltpu.VMEM((heads * 2 * t, A_V), F32)],
    )
    out = pl.pallas_call(
        functools.partial(_attn_sample_body, pps=pps, heads=heads, t=t, lam_init=lam_init),
        grid_spec=grid_spec,
        out_shape=jax.ShapeDtypeStruct((n, t, hw), BF16),
        compiler_params=_cparams(2),
        name="attn_sample",
    )(page_table, lam_p, q.reshape(n, t, hw), kn, vn, head_g.reshape(1, A_V),
      *([ck] * pps), *([cv] * pps))
    return out.reshape(n * t, hw)


def _shift_rows(x, d, fill):
    row = lax.broadcasted_iota(jnp.int32, x.shape, 0)
    return jnp.where(row >= d, pltpu.roll(x, d, 0), fill)


def _linear_scan(a, b):
    d = 1
    while d < a.shape[0]:
        b = b + a * _shift_rows(b, d, 0.0)
        a = a * _shift_rows(a, d, 1.0)
        d *= 2
    return a, b


def _rglru_body(xb_ref, gb_ref, conv0_ref, h0_ref, cw_ref, cb_ref, wg_ref, gbias_ref, lam_ref,
                rec_ref, convn_ref, hl_ref, ext_sc, h_sc, *, tt, w):
    halo = 8
    step = pl.program_id(1)

    @pl.when(step == 0)
    def _():
        ext_sc[0:halo, :] = jnp.zeros((halo, w), F32)
        ext_sc[halo - (B_CONV - 1):halo, :] = conv0_ref[0]
        h_sc[...] = h0_ref[0]

    x = xb_ref[0]
    ext_sc[halo:halo + tt, :] = x
    cw = cw_ref[...]
    xc = cb_ref[...] + x * cw[B_CONV - 1:B_CONV, :]
    for j in range(1, B_CONV):
        xc = xc + ext_sc[halo - j:halo - j + tt, :] * cw[B_CONV - 1 - j:B_CONV - j, :]

    pre = _dot(xc.astype(BF16), wg_ref[...]) + gbias_ref[...]
    r = jax.nn.sigmoid(pre[:, :w])
    i = jax.nn.sigmoid(pre[:, w:])
    lam = lam_ref[...]
    softplus_neg = jnp.maximum(-lam, 0.0) + jnp.log1p(jnp.exp(-jnp.abs(lam)))
    log_a = (-C_RG) * r * softplus_neg
    a = jnp.exp(log_a)
    b = jnp.sqrt(1.0 - a * a) * (i * xc)
    a_cum, h_loc = _linear_scan(a, b)
    h = h_loc + a_cum * h_sc[...]
    rec_ref[0] = (_gelu_tanh(gb_ref[0]) * h).astype(rec_ref.dtype)

    h_sc[...] = h[tt - 1:tt, :]
    hl_ref[0] = h[tt - 1:tt, :]
    convn_ref[0] = ext_sc[halo + tt - (B_CONV - 1):halo + tt, :]
    ext_sc[0:halo, :] = ext_sc[tt:tt + halo, :]


def _prep_gate_w(gate_w):
    _, nb, bs, _ = gate_w.shape
    eye = jnp.eye(nb, dtype=gate_w.dtype)
    dense = jnp.einsum('gbij,bc->gbicj', gate_w, eye).reshape(2, nb * bs, nb * bs)
    return jnp.concatenate([dense[0], dense[1]], axis=1).astype(BF16)


def _rglru(xb, gb, conv0, h0, conv_w, conv_b, gate_wd, gate_b, lam_p, n, t):
    w = xb.shape[1]
    tt = _row_tile(t, 256)
    seq = pl.BlockSpec((1, tt, w), lambda b, i: (b, i, 0))
    per_n = lambda rows: pl.BlockSpec((1, rows, w), lambda b, i: (b, 0, 0))
    return pl.pallas_call(
        functools.partial(_rglru_body, tt=tt, w=w),
        grid=(n, t // tt),
        in_specs=[seq, seq, per_n(B_CONV - 1), per_n(1), _const_spec((B_CONV, w)), _const_spec((1, w)),
                  _const_spec((w, 2 * w)), _const_spec((1, 2 * w)), _const_spec((1, w))],
        out_specs=[seq, per_n(B_CONV - 1), per_n(1)],
        out_shape=[jax.ShapeDtypeStruct((n, t, w), BF16),
                   jax.ShapeDtypeStruct((n, B_CONV - 1, w), F32),
                   jax.ShapeDtypeStruct((n, 1, w), F32)],
        scratch_shapes=[pltpu.VMEM((tt + 8, w), F32), pltpu.VMEM((1, w), F32)],
        compiler_params=_cparams(2),
        name="rglru",
    )(xb.reshape(n, t, w), gb.reshape(n, t, w), conv0, h0.reshape(n, 1, w), conv_w,
      conv_b.reshape(1, w), gate_wd, gate_b.reshape(1, 2 * w), lam_p.reshape(1, w))


def _inproj_odd_body(x_ref, g_ref, w_ref, u_ref, gin_ref, *, hw):
    h = _rms(x_ref[...], g_ref[...]).astype(BF16)
    proj = _dot(h, w_ref[...])
    for c in range(hw // LANES):
        u_ref[c] = proj[:, c * LANES:(c + 1) * LANES]
    gin_ref[...] = proj[:, hw:2 * hw] * jax.nn.sigmoid(proj[:, 2 * hw:])


def _inproj_odd(x, g, w_in):
    m, d = x.shape
    hw = w_in.shape[1] // 3
    tm = _row_tile(m, 512)
    row = pl.BlockSpec((tm, d), lambda i: (i, 0))
    orow = pl.BlockSpec((tm, hw), lambda i: (i, 0))
    ucols = pl.BlockSpec((hw // LANES, tm, LANES), lambda i: (0, i, 0))
    return pl.pallas_call(
        functools.partial(_inproj_odd_body, hw=hw),
        grid=(m // tm,),
        in_specs=[row, _const_spec((1, d)), _const_spec(w_in.shape)],
        out_specs=[ucols, orow],
        out_shape=[jax.ShapeDtypeStruct((hw // LANES, m, LANES), F32),
                   jax.ShapeDtypeStruct((m, hw), F32)],
        compiler_params=_cparams(1),
        name="inproj_odd",
    )(x, g.reshape(1, d), w_in)


def _cmul(ar, ai, br, bi):
    return ar * br - ai * bi, ar * bi + ai * br


def _s5_discretise(a_re, a_im, log_dt):
    dt = jnp.exp(log_dt)
    mag = jnp.exp(dt * a_re)
    ab_re = mag * jnp.cos(dt * a_im)
    ab_im = mag * jnp.sin(dt * a_im)
    den = a_re * a_re + a_im * a_im
    nr = ab_re - 1.0
    return ab_re, ab_im, (nr * a_re + ab_im * a_im) / den, (ab_im * a_re - nr * a_im) / den


def _s5_body(u_ref, re0_ref, im0_ref, are_ref, aim_ref, ldt_ref,
             bre_ref, bim_ref, cre_ref, cim_ref, d_ref, gw_ref, gbias_ref, o_ref, sre_ref, sim_ref,
             st_re, st_im, bf_re, bf_im, lam_re, lam_im, lseg_re, lseg_im, up_sc, v_re, v_im,
             *, tt, w, sw):
    seg = tt // SUBL

    def token_rows(j):
        return pl.ds(j, SUBL, stride=seg)

    def step_rows(j):
        return pl.ds(pl.multiple_of(j * SUBL, SUBL), SUBL)

    @pl.when((pl.program_id(0) == 0) & (pl.program_id(1) == 0))
    def _():
        ab_re, ab_im, coef_re, coef_im = _s5_discretise(are_ref[...], aim_ref[...], ldt_ref[...])
        f_re, f_im = _cmul(coef_re, coef_im, bre_ref[...], bim_ref[...])
        bf_re[...] = f_re.astype(BF16)
        bf_im[...] = f_im.astype(BF16)
        lam_re[...] = jnp.broadcast_to(ab_re, lam_re.shape)
        lam_im[...] = jnp.broadcast_to(ab_im, lam_im.shape)
        pr, pi = ab_re, ab_im
        for _ in range(seg.bit_length() - 1):
            pr, pi = _cmul(pr, pi, pr, pi)
        lseg_re[...] = pr
        lseg_im[...] = pi

    @pl.when(pl.program_id(1) == 0)
    def _():
        st_re[...] = re0_ref[0]
        st_im[...] = im0_ref[0]

    def gather(j, _):
        up_sc[step_rows(j), :] = jnp.concatenate(
            [u_ref[c, 0, token_rows(j), :] for c in range(w // LANES)], axis=-1)
        return 0

    lax.fori_loop(0, seg, gather, 0)
    u = up_sc[...]
    ub = u.astype(BF16)
    gpt = V7X_MXU_DIM // C_STATE
    kpt = V7X_MXU_DIM // C_GROUP
    for nt in range(sw // V7X_MXU_DIM):
        kt = (nt * gpt) // kpt
        lhs = ub[:, kt * V7X_MXU_DIM:(kt + 1) * V7X_MXU_DIM]
        rows = slice(kt * V7X_MXU_DIM, (kt + 1) * V7X_MXU_DIM)
        cols = slice(nt * V7X_MXU_DIM, (nt + 1) * V7X_MXU_DIM)
        v_re[:, cols] = _dot(lhs, bf_re[rows, cols])
        v_im[:, cols] = _dot(lhs, bf_im[rows, cols])

    for ch in range(sw // S5_SCAN_LANES):
        lanes = slice(ch * S5_SCAN_LANES, (ch + 1) * S5_SCAN_LANES)
        l_re, l_im = lam_re[:, lanes], lam_im[:, lanes]

        def advance(j, sr, si, lanes=lanes, l_re=l_re, l_im=l_im):
            pr, pi = _cmul(l_re, l_im, sr, si)
            return pr + v_re[step_rows(j), lanes], pi + v_im[step_rows(j), lanes]

        zero = jnp.zeros((SUBL, S5_SCAN_LANES), F32)
        e_re, e_im = lax.fori_loop(0, seg, lambda j, c, adv=advance: adv(j, *c), (zero, zero))

        row = lax.broadcasted_iota(jnp.int32, zero.shape, 0)
        c_re, c_im = st_re[:, lanes], st_im[:, lanes]
        in_re, in_im = zero, zero
        for k in range(SUBL):
            in_re = jnp.where(row == k, c_re, in_re)
            in_im = jnp.where(row == k, c_im, in_im)
            p_re, p_im = _cmul(lseg_re[:, lanes], lseg_im[:, lanes], c_re, c_im)
            c_re, c_im = p_re + e_re[k:k + 1, :], p_im + e_im[k:k + 1, :]
        st_re[:, lanes] = c_re
        st_im[:, lanes] = c_im
        sre_ref[0, :, lanes] = c_re
        sim_ref[0, :, lanes] = c_im

        def emit(j, c, lanes=lanes, adv=advance):
            sr, si = adv(j, *c)
            v_re[step_rows(j), lanes] = sr
            v_im[step_rows(j), lanes] = si
            return sr, si

        lax.fori_loop(0, seg, emit, (in_re, in_im))

    ys = []
    for ot in range(w // V7X_MXU_DIM):
        acc = None
        for nt in range(ot * kpt // gpt, (ot + 1) * kpt // gpt):
            rows = slice(nt * V7X_MXU_DIM, (nt + 1) * V7X_MXU_DIM)
            cols = slice(ot * V7X_MXU_DIM, (ot + 1) * V7X_MXU_DIM)
            part = (_dot(v_re[:, rows].astype(BF16), cre_ref[rows, cols])
                    - _dot(v_im[:, rows].astype(BF16), cim_ref[rows, cols]))
            acc = part if acc is None else acc + part
        ys.append(acc)
    y = jnp.concatenate(ys, axis=-1) + d_ref[...] * u
    z = _gelu_tanh(y)
    up_sc[...] = z * jax.nn.sigmoid(_dot(z.astype(BF16), gw_ref[...]) + gbias_ref[...])

    def scatter(j, _):
        res = up_sc[step_rows(j), :]
        for c in range(w // LANES):
            o_ref[c, 0, token_rows(j), :] = res[:, c * LANES:(c + 1) * LANES]
        return 0

    lax.fori_loop(0, seg, scatter, 0)


def _prep_s5(b_re, b_im, c_re, c_im):
    g = b_re.shape[0]
    eye = jnp.eye(g, dtype=b_re.dtype)
    bd_in = lambda b: jnp.einsum('gpc,gh->gchp', b, eye).reshape(g * C_GROUP, g * C_STATE)
    bd_out = lambda c: jnp.einsum('gcp,gh->gphc', c, eye).reshape(g * C_STATE, g * C_GROUP).astype(BF16)
    return bd_in(b_re), bd_in(b_im), bd_out(c_re), bd_out(c_im)


def _s5(u, re0, im0, a_re, a_im, log_dt, mats, d_skip, glu_w, glu_b, n, t):
    w = u.shape[0] * LANES
    groups = w // C_GROUP
    sw = groups * C_STATE
    tt = _row_tile(t, 512)
    bre, bim, cre, cim = mats
    wc = w // LANES
    seq = pl.BlockSpec((wc, 1, tt, LANES), lambda b, i: (0, b, i, 0))
    st = pl.BlockSpec((1, 1, sw), lambda b, i: (b, 0, 0))
    vm = lambda shape, dt: pltpu.VMEM(shape, dt)
    flat = lambda a: a.reshape(1, sw)
    out, s_re, s_im = pl.pallas_call(
        functools.partial(_s5_body, tt=tt, w=w, sw=sw),
        grid=(n, t // tt),
        in_specs=[seq, st, st] + [_const_spec((1, sw))] * 3
                 + [_const_spec((w, sw)), _const_spec((w, sw)), _const_spec((sw, w)), _const_spec((sw, w)),
                    _const_spec((1, w)), _const_spec((w, w)), _const_spec((1, w))],
        out_specs=[seq, st, st],
        out_shape=[jax.ShapeDtypeStruct((wc, n, t, LANES), F32),
                   jax.ShapeDtypeStruct((n, 1, sw), F32), jax.ShapeDtypeStruct((n, 1, sw), F32)],
        scratch_shapes=[vm((1, sw), F32), vm((1, sw), F32), vm((w, sw), BF16), vm((w, sw), BF16),
                        vm((SUBL, sw), F32), vm((SUBL, sw), F32), vm((1, sw), F32), vm((1, sw), F32),
                        vm((tt, w), F32), vm((tt, sw), F32), vm((tt, sw), F32)],
        compiler_params=_cparams(2),
        name="s5",
    )(u.reshape(wc, n, t, LANES), re0.reshape(n, 1, sw), im0.reshape(n, 1, sw),
      flat(a_re), flat(a_im), flat(jnp.repeat(log_dt, C_STATE)), bre, bim, cre, cim,
      d_skip.reshape(1, w), glu_w.astype(BF16), glu_b.reshape(1, w))
    return out.reshape(wc, n * t, LANES), s_re, s_im


def _conf_body(g_ref, conv0_ref, cw_ref, cb_ref, lng_ref, lnb_ref, o_ref, convn_ref, ext_sc, *, tt, w):
    halo = 32
    hist = D_CONV - 1
    step = pl.program_id(1)

    @pl.when(step == 0)
    def _():
        ext_sc[0:halo, :] = jnp.zeros((halo, w), F32)
        ext_sc[halo - hist:halo, :] = conv0_ref[0]

    ext_sc[halo:halo + tt, :] = g_ref[0]
    cw = cw_ref[...]
    c = cb_ref[...] + jnp.zeros((tt, w), F32)
    for j in range(D_CONV):
        c = c + ext_sc[halo - hist + j:halo - hist + j + tt, :] * cw[j:j + 1, :]
    mu = jnp.mean(c, axis=-1, keepdims=True)
    cc = c - mu
    y = cc * lax.rsqrt(jnp.mean(cc * cc, axis=-1, keepdims=True) + EPS) * lng_ref[...] + lnb_ref[...]
    o_ref[0] = (y * jax.nn.sigmoid(y)).astype(o_ref.dtype)
    convn_ref[0] = ext_sc[halo + tt - hist:halo + tt, :]
    ext_sc[0:halo, :] = ext_sc[tt:tt + halo, :]


def _conf(g_in, conv0, conv_w, conv_b, ln_g, ln_b, n, t):
    w = g_in.shape[1]
    tt = _row_tile(t, 256)
    seq = pl.BlockSpec((1, tt, w), lambda b, i: (b, i, 0))
    hist = pl.BlockSpec((1, D_CONV - 1, w), lambda b, i: (b, 0, 0))
    return pl.pallas_call(
        functools.partial(_conf_body, tt=tt, w=w),
        grid=(n, t // tt),
        in_specs=[seq, hist, _const_spec((D_CONV, w)), _const_spec((1, w)), _const_spec((1, w)),
                  _const_spec((1, w))],
        out_specs=[seq, hist],
        out_shape=[jax.ShapeDtypeStruct((n, t, w), BF16),
                   jax.ShapeDtypeStruct((n, D_CONV - 1, w), F32)],
        scratch_shapes=[pltpu.VMEM((tt + 32, w), F32)],
        compiler_params=_cparams(2),
        name="conformer_conv",
    )(g_in.reshape(n, t, w), conv0, conv_w, conv_b.reshape(1, w), ln_g.reshape(1, w), ln_b.reshape(1, w))


def _run_group(x3, cache, conv_b0, h_b0, c_re0, c_im0, conv_d0, p, prepped):
    n, t, d = x3.shape
    depth = p['ffn1_g'].shape[0]
    x = x3.reshape(n * t, d)
    ks, vs, cbs, hbs, cres, cims, cds = [], [], [], [], [], [], []
    for l in range(depth):
        x = _ffn(x, p['ffn1_g'][l], prepped['ffn1'][l])
        if l % 2 == 0:
            e = l // 2
            hw = p['even_w_in'].shape[2] // 5
            heads = hw // A_V
            q, k, kb, v, vb, xb, gb = _inproj_even(x, p['mix_g'][l], prepped['even_w_in'][e],
                                                   p['a_q_g'][e], p['a_k_g'][e])
            lam_init = 0.8 - 0.6 * math.exp(-0.3 * l)
            if cache is None:
                att = _attn_prompt(q, kb, vb, p['a_lambda'][e], p['a_head_g'][e], n, t, lam_init)
            else:
                cache_k, cache_v, page_table = cache
                att = _attn_sample(q, k, v, cache_k, cache_v, page_table, e, p['a_lambda'][e],
                                   p['a_head_g'][e], n, t, lam_init)
            rec, cb, hb = _rglru(xb, gb, conv_b0[e], h_b0[e], p['b_conv_w'][e], p['b_conv_b'][e],
                                 prepped['b_gate_w'][e], p['b_gate_b'][e], p['b_lambda'][e], n, t)
            mixer = (att, rec.reshape(n * t, -1), prepped['even_w_out'][e])
            ks.append(k.reshape(n, t, heads, 2 * A_QK))
            vs.append(v.reshape(n, t, heads, A_V))
            cbs.append(cb)
            hbs.append(hb.reshape(n, hw))
        else:
            o = l // 2
            u, g_in = _inproj_odd(x, p['mix_g'][l], prepped['odd_w_in'][o])
            groups = u.shape[0] * LANES // C_GROUP
            c_out, cre, cim = _s5(u, c_re0[o], c_im0[o], p['c_a_re'][o], p['c_a_im'][o], p['c_log_dt'][o],
                                  prepped['s5'][o], p['c_d'][o], p['c_glu_w'][o], p['c_glu_b'][o], n, t)
            d_out, cd = _conf(g_in, conv_d0[o], p['d_conv_w'][o], p['d_conv_b'][o],
                              p['d_ln_g'][o], p['d_ln_b'][o], n, t)
            mixer = (c_out, d_out.reshape(n * t, -1), prepped['odd_w_out'][o])
            cres.append(cre.reshape(n, groups, C_STATE))
            cims.append(cim.reshape(n, groups, C_STATE))
            cds.append(cd)
        x = _ffn(x, p['ffn2_g'][l], prepped['ffn2'][l], mixer)
    return (x.reshape(n, t, d), jnp.stack(ks), jnp.stack(vs), jnp.stack(cbs), jnp.stack(hbs),
            jnp.stack(cres), jnp.stack(cims), jnp.stack(cds))


def kernel(x_prompt, x_sample, cache_k, cache_v, state_conv_b, state_h_b, state_c_re, state_c_im,
           state_conv_d, page_table, ffn1_g, ffn1_w_gu, ffn1_w_down, mix_g, ffn2_g, ffn2_w_gu,
           ffn2_w_down, even_w_in, even_w_out, a_q_g, a_k_g, a_lambda, a_head_g, b_conv_w, b_conv_b,
           b_gate_w, b_gate_b, b_lambda, odd_w_in, odd_w_out, c_a_re, c_a_im, c_log_dt, c_b_re, c_b_im,
           c_c_re, c_c_im, c_d, c_glu_w, c_glu_b, d_conv_w, d_conv_b, d_ln_g, d_ln_b):
    p = dict(ffn1_g=ffn1_g, mix_g=mix_g, ffn2_g=ffn2_g, even_w_in=even_w_in, a_q_g=a_q_g, a_k_g=a_k_g,
             a_lambda=a_lambda, a_head_g=a_head_g, b_conv_w=b_conv_w, b_conv_b=b_conv_b,
             b_gate_b=b_gate_b, b_lambda=b_lambda, c_a_re=c_a_re, c_a_im=c_a_im, c_log_dt=c_log_dt,
             c_d=c_d, c_glu_w=c_glu_w, c_glu_b=c_glu_b, d_conv_w=d_conv_w, d_conv_b=d_conv_b,
             d_ln_g=d_ln_g, d_ln_b=d_ln_b)
    depth = ffn1_g.shape[0]
    n_even, n_odd = even_w_in.shape[0], odd_w_in.shape[0]
    prepped = dict(
        ffn1=[_prep_ffn(ffn1_w_gu[l], ffn1_w_down[l]) for l in range(depth)],
        ffn2=[_prep_ffn(ffn2_w_gu[l], ffn2_w_down[l]) for l in range(depth)],
        even_w_in=[even_w_in[e].astype(BF16) for e in range(n_even)],
        even_w_out=[even_w_out[e].astype(BF16) for e in range(n_even)],
        odd_w_in=[odd_w_in[o].astype(BF16) for o in range(n_odd)],
        odd_w_out=[odd_w_out[o].astype(BF16) for o in range(n_odd)],
        b_gate_w=[_prep_gate_w(b_gate_w[e]) for e in range(n_even)],
        s5=[_prep_s5(c_b_re[o], c_b_im[o], c_c_re[o], c_c_im[o]) for o in range(n_odd)],
    )
    b = x_prompt.shape[0]
    dt = x_prompt.dtype
    hw = even_w_in.shape[2] // 5
    groups, states = state_c_re.shape[2], state_c_re.shape[3]
    prompt = _run_group(
        x_prompt, None,
        jnp.zeros((n_even, b, B_CONV - 1, hw), dt), jnp.zeros((n_even, b, hw), dt),
        jnp.zeros((n_odd, b, groups, states), dt), jnp.zeros((n_odd, b, groups, states), dt),
        jnp.zeros((n_odd, b, D_CONV - 1, hw), dt), p, prepped)
    sample = _run_group(
        x_sample, (cache_k, cache_v, page_table), state_conv_b, state_h_b, state_c_re, state_c_im,
        state_conv_d, p, prepped)
    return (prompt[0], sample[0]) + prompt[1:] + sample[1:]
```

```python
import functools
import math

import jax
import jax.numpy as jnp
from jax import lax
from jax.experimental import pallas as pl
from jax.experimental.pallas import tpu as pltpu

F32 = jnp.float32
BF16 = jnp.bfloat16
EPS = 1e-6
NEG = -1e30
LOG2E = 1.4426950408889634
ATTN_SCORE_BOUND = 64.0

LANES = 128
SUBL = 8
V7X_MXU_DIM = 256
V7X_VMEM_LIMIT = 56 * 1024 * 1024

A_HEADS = 4
A_QK = 64
A_V = 128
B_BLOCKS = 8
B_CONV = 4
C_RG = 8.0
C_GROUP = 16
C_STATE = 64
D_CONV = 31
SEG_BLOCK = 512
S5_SCAN_LANES = 1024


def _cparams(n_axes):
    return pltpu.CompilerParams(dimension_semantics=("arbitrary",) * n_axes,
                                vmem_limit_bytes=V7X_VMEM_LIMIT)


def _const_spec(shape):
    nd = len(shape)
    return pl.BlockSpec(shape, lambda *_: (0,) * nd)


def _rms(x, g):
    return x * lax.rsqrt(jnp.mean(x * x, axis=-1, keepdims=True) + EPS) * g


def _gelu_tanh(x):
    return 0.5 * x * (1.0 + jnp.tanh(math.sqrt(2.0 / math.pi) * (x + 0.044715 * (x * x * x))))


def _dot(a, b):
    return jnp.dot(a, b, preferred_element_type=F32)


def _row_tile(m, want):
    t = min(m, want)
    while m % t:
        t //= 2
    return t


def _store_col_tiles(ref, val, seg):
    for c in range(val.shape[1] // LANES):
        tile = val[:, c * LANES:(c + 1) * LANES]
        if seg == 1:
            ref[c] = tile
        else:
            for s in range(SUBL):
                ref[c, pl.ds(s, seg, stride=SUBL), :] = tile[s * seg:(s + 1) * seg, :]


def _load_col_tiles(ref, seg):
    cols = []
    for c in range(ref.shape[0]):
        if seg == 1:
            cols.append(ref[c])
        else:
            cols.append(jnp.concatenate(
                [ref[c, pl.ds(s, seg, stride=SUBL), :] for s in range(SUBL)], axis=0))
    return jnp.concatenate(cols, axis=-1)


def _mixer_half(ref, seg):
    if len(ref.shape) == 2:
        return ref[...]
    return _load_col_tiles(ref, seg).astype(BF16)


def _ffn_body(x_ref, g_ref, wg_ref, wu_ref, wd_ref, *rest, ck, with_mixer, seg):
    x = x_ref[...]
    if with_mixer:
        a_ref, b_ref, wa_ref, wb_ref, o_ref = rest
        x = x + _dot(_mixer_half(a_ref, seg), wa_ref[...]) + _dot(_mixer_half(b_ref, seg), wb_ref[...])
    else:
        o_ref, = rest
    h = _rms(x, g_ref[...]).astype(BF16)
    acc = jnp.zeros(x.shape, F32)
    for c in range(wg_ref.shape[1] // ck):
        sl = slice(c * ck, (c + 1) * ck)
        gate = _dot(h, wg_ref[:, sl])
        up = _dot(h, wu_ref[:, sl])
        a = (gate * jax.nn.sigmoid(gate) * up).astype(BF16)
        acc = acc + _dot(a, wd_ref[sl, :])
    o_ref[...] = x + 0.5 * acc


def _prep_ffn(w_gu, w_down):
    f = w_down.shape[0]
    fp = -(-f // V7X_MXU_DIM) * V7X_MXU_DIM
    pad_c = ((0, 0), (0, fp - f))
    wg = jnp.pad(w_gu[:, :f], pad_c).astype(BF16)
    wu = jnp.pad(w_gu[:, f:], pad_c).astype(BF16)
    wd = jnp.pad(w_down, ((0, fp - f), (0, 0))).astype(BF16)
    return wg, wu, wd


def _ffn(x, g, w, mixer=None, seg=1):
    wg, wu, wd = w
    m, d = x.shape
    fp = wg.shape[1]
    tm = SUBL * seg if seg > 1 else _row_tile(m, 512)
    row = pl.BlockSpec((tm, d), lambda i: (i, 0))
    in_specs = [row, _const_spec((1, d)), _const_spec((d, fp)), _const_spec((d, fp)), _const_spec((fp, d))]
    args = [x, g.reshape(1, d), wg, wu, wd]
    if mixer is not None:
        a, b, w_out = mixer
        hw = w_out.shape[0] // 2
        spec = lambda v: (pl.BlockSpec((tm, hw), lambda i: (i, 0)) if v.ndim == 2 else
                          pl.BlockSpec((v.shape[0], tm, LANES), lambda i: (0, i, 0)))
        in_specs += [spec(a), spec(b), _const_spec((hw, d)), _const_spec((hw, d))]
        args += [a, b, w_out[:hw], w_out[hw:]]
    return pl.pallas_call(
        functools.partial(_ffn_body, ck=V7X_MXU_DIM, with_mixer=mixer is not None, seg=seg),
        grid=(m // tm,),
        in_specs=in_specs,
        out_specs=row,
        out_shape=jax.ShapeDtypeStruct((m, d), F32),
        compiler_params=_cparams(1),
        name="ffn_mix" if mixer is not None else "ffn",
    )(*args)


def _inproj_even_body(x_ref, g_ref, w_ref, gq_ref, gk_ref, ones_ref,
                      q_ref, k_ref, kb_ref, v_ref, vb_ref, xb_ref, gb_ref, *, hw, tm):
    x = x_ref[...]
    h = _rms(x, g_ref[...]).astype(BF16)
    proj = _dot(h, w_ref[...])

    def group_rms(t, gain):
        ss = t * t
        hi = ss.astype(BF16)
        lo = (ss - hi.astype(F32)).astype(BF16)
        parts = []
        for c in range(hw // V7X_MXU_DIM):
            sl = slice(c * V7X_MXU_DIM, (c + 1) * V7X_MXU_DIM)
            parts.append(_dot(hi[:, sl], ones_ref[...]) + _dot(lo[:, sl], ones_ref[...]))
        gs = jnp.concatenate(parts, axis=-1)
        return t * lax.rsqrt(gs * (1.0 / A_QK) + EPS) * gain

    q = group_rms(proj[:, 0:hw], gq_ref[...]) * (A_QK ** -0.5 * LOG2E)
    k = group_rms(proj[:, hw:2 * hw], gk_ref[...])
    v = proj[:, 2 * hw:3 * hw]
    q_ref[...] = q.astype(BF16)
    kb_ref[...] = k.astype(BF16)
    vb_ref[...] = v.astype(BF16)
    heads = hw // A_V
    for hd in range(heads):
        k_ref[pl.ds(hd, tm, stride=heads), :] = k[:, hd * A_V:(hd + 1) * A_V]
        v_ref[pl.ds(hd, tm, stride=heads), :] = v[:, hd * A_V:(hd + 1) * A_V]
    xb_ref[...] = proj[:, 3 * hw:4 * hw]
    gb_ref[...] = proj[:, 4 * hw:5 * hw]


def _inproj_even(x, g, w_in, gq, gk):
    m, d = x.shape
    hw = w_in.shape[1] // 5
    heads = hw // A_V
    tm = _row_tile(m, 512)
    row = pl.BlockSpec((tm, d), lambda i: (i, 0))
    orow = pl.BlockSpec((tm, hw), lambda i: (i, 0))
    hrow = pl.BlockSpec((tm * heads, A_V), lambda i: (i, 0))
    idx = jnp.arange(V7X_MXU_DIM) // A_QK
    ones_bd = (idx[:, None] == idx[None, :]).astype(BF16)
    tile = lambda t: jnp.tile(t, hw // A_QK).reshape(1, hw)
    sds = lambda dt: jax.ShapeDtypeStruct((m, hw), dt)
    hsds = jax.ShapeDtypeStruct((m * heads, A_V), F32)
    return pl.pallas_call(
        functools.partial(_inproj_even_body, hw=hw, tm=tm),
        grid=(m // tm,),
        in_specs=[row, _const_spec((1, d)), _const_spec(w_in.shape), _const_spec((1, hw)),
                  _const_spec((1, hw)), _const_spec(ones_bd.shape)],
        out_specs=[orow, hrow, orow, hrow, orow, orow, orow],
        out_shape=[sds(BF16), hsds, sds(BF16), hsds, sds(BF16), sds(F32), sds(F32)],
        compiler_params=_cparams(1),
        name="inproj_even",
    )(x, g.reshape(1, d), w_in, tile(gq), tile(gk), ones_bd)


def _diff_lambda(lv, lam_init):
    s01 = jnp.sum(lv[0:1, :] * lv[1:2, :], axis=-1, keepdims=True)
    s23 = jnp.sum(lv[2:3, :] * lv[3:4, :], axis=-1, keepdims=True)
    return jnp.exp(s01) - jnp.exp(s23) + lam_init


def _split_maps(q):
    lane = lax.broadcasted_iota(jnp.int32, q.shape, 1)
    zero = jnp.zeros_like(q)
    return jnp.where(lane < A_QK, q, zero), jnp.where(lane >= A_QK, q, zero)


def _scores(q, k):
    return lax.dot_general(q, k, (((1,), (1,)), ((), ())), preferred_element_type=F32)


def _attn_prompt_body(lam_ref, q_ref, k_ref, v_ref, hg_ref, o_ref, vext_sc, kn2_sc,
                      *, tq, hps, lam_init):
    qi = pl.program_id(2)

    def max_row_norm2(x):
        xf = x.astype(F32)
        per_head = [jnp.sum(xf[:, hh * A_V:(hh + 1) * A_V] ** 2, axis=-1, keepdims=True)
                    for hh in range(hps)]
        return jnp.max(functools.reduce(jnp.maximum, per_head), axis=0, keepdims=True)

    @pl.when(qi == 0)
    def _():
        for hh in range(hps):
            vext_sc[:, 2 * hh * A_V:(2 * hh + 1) * A_V] = v_ref[:, hh * A_V:(hh + 1) * A_V]
            vext_sc[:, (2 * hh + 1) * A_V:(2 * hh + 2) * A_V] = jnp.ones((v_ref.shape[0], A_V), BF16)
        kn2_sc[...] = max_row_norm2(k_ref[...])

    lam = _diff_lambda(lam_ref[...], lam_init)
    q = q_ref[...]
    qs = [m for hh in range(hps) for m in _split_maps(q[:, hh * A_V:(hh + 1) * A_V])]

    def chain_inputs(kb, c):
        rows = pl.ds(pl.multiple_of(kb * tq, tq), tq)
        hh = c // 2
        s = _scores(qs[c], k_ref[rows, hh * A_V:(hh + 1) * A_V])
        return s, vext_sc[rows, 2 * hh * A_V:(2 * hh + 2) * A_V]

    def causal(shape):
        return lax.broadcasted_iota(jnp.int32, shape, 1) <= lax.broadcasted_iota(jnp.int32, shape, 0)

    def finish(accs):
        for hh in range(hps):
            a0, a1 = accs[2 * hh], accs[2 * hh + 1]
            o = a0[:, :A_V] / a0[:, A_V:] - lam * (a1[:, :A_V] / a1[:, A_V:])
            o_ref[:, hh * A_V:(hh + 1) * A_V] = (
                _rms(o, hg_ref[...]) * (1.0 - lam_init)).astype(o_ref.dtype)

    def bounded():
        def step(kb, accs, diag):
            new = []
            for c in range(2 * hps):
                s, v = chain_inputs(kb, c)
                p = jnp.exp2(s)
                if diag:
                    p = jnp.where(causal(p.shape), p, 0.0)
                new.append(accs[c] + _dot(p.astype(BF16), v))
            return tuple(new)

        zero = jnp.zeros((tq, 2 * A_V), F32)
        accs = lax.fori_loop(0, qi, lambda kb, a: step(kb, a, False), (zero,) * (2 * hps))
        finish(step(qi, accs, True))

    def general():
        def step(kb, carry, diag):
            new = []
            for c in range(2 * hps):
                m, acc = carry[c]
                s, v = chain_inputs(kb, c)
                if diag:
                    s = jnp.where(causal(s.shape), s, NEG)
                m_new = jnp.maximum(m, jnp.max(s, axis=-1, keepdims=True))
                p = jnp.exp2(s - m_new)
                new.append((m_new, jnp.exp2(m - m_new) * acc + _dot(p.astype(BF16), v)))
            return tuple(new)

        init_one = (jnp.full((tq, 1), NEG, F32), jnp.zeros((tq, 2 * A_V), F32))
        carry = lax.fori_loop(0, qi, lambda kb, c: step(kb, c, False), (init_one,) * (2 * hps))
        finish([acc for _, acc in step(qi, carry, True)])

    bound2 = max_row_norm2(q) * kn2_sc[...]
    lax.cond(bound2[0, 0] <= ATTN_SCORE_BOUND ** 2, bounded, general)


def _attn_prompt(q, k, v, lam_p, head_g, n, t, lam_init):
    hw = q.shape[1]
    heads = hw // A_V
    hps = next(c for c in (4, 2, 1) if heads % c == 0)
    tq = _row_tile(t, 512)
    nq = t // tq
    qspec = pl.BlockSpec((tq, hps * A_V), lambda b, h, i: (b * nq + i, h))
    kspec = pl.BlockSpec((t, hps * A_V), lambda b, h, i: (b, h))
    return pl.pallas_call(
        functools.partial(_attn_prompt_body, tq=tq, hps=hps, lam_init=lam_init),
        grid=(n, heads // hps, nq),
        in_specs=[_const_spec(lam_p.shape), qspec, kspec, kspec, _const_spec((1, A_V))],
        out_specs=qspec,
        out_shape=jax.ShapeDtypeStruct((n * t, hw), BF16),
        scratch_shapes=[pltpu.VMEM((t, hps * 2 * A_V), BF16), pltpu.VMEM((1, 1), F32)],
        compiler_params=_cparams(3),
        name="attn_prompt",
    )(lam_p, q, k, v, head_g.reshape(1, A_V))


def _attn_sample_body(pt_ref, lam_ref, q_ref, kn_ref, vn_ref, hg_ref, *rest,
                      pps, gsz, heads, t, lam_init):
    kp, vp = rest[:pps], rest[pps:2 * pps]
    o_ref = rest[2 * pps]
    bias_sc, m_sc, l_sc, acc_sc = rest[2 * pps + 1:]
    seq, step = pl.program_id(0), pl.program_id(1)
    qrows = 2 * t
    q = q_ref[0]
    qall = jnp.concatenate(
        [piece for h in range(heads) for piece in _split_maps(q[:, h * A_V:(h + 1) * A_V])], axis=0)

    def head_match(shape):
        row = lax.broadcasted_iota(jnp.int32, shape, 0)
        col = lax.broadcasted_iota(jnp.int32, shape, 1)
        return row, col, (col % heads) == (row // qrows)

    def update(s, v):
        m = m_sc[...]
        m_new = jnp.maximum(m, jnp.max(s, axis=-1, keepdims=True))
        alpha = jnp.exp2(m - m_new)
        p = jnp.exp2(s - m_new)
        l_sc[...] = alpha * l_sc[...] + jnp.sum(p, axis=-1, keepdims=True)
        acc_sc[...] = alpha * acc_sc[...] + _dot(p.astype(BF16), v)
        m_sc[...] = m_new

    @pl.when((seq == 0) & (step == 0))
    def _():
        _, _, ok = head_match(bias_sc.shape)
        bias_sc[...] = jnp.where(ok, 0.0, NEG)

    @pl.when(step == 0)
    def _():
        m_sc[...] = jnp.full(m_sc.shape, NEG, F32)
        l_sc[...] = jnp.zeros(l_sc.shape, F32)
        acc_sc[...] = jnp.zeros(acc_sc.shape, F32)
        s = _scores(qall, kn_ref[0])
        row, col, ok = head_match(s.shape)
        ok = ok & ((col // heads) <= (row % t))
        update(jnp.where(ok, s, NEG), vn_ref[0])

    parts = []
    for g in range(pps // gsz):
        kcat = jnp.concatenate([r[...] for r in kp[g * gsz:(g + 1) * gsz]], axis=0).astype(BF16)
        vcat = jnp.concatenate([r[...] for r in vp[g * gsz:(g + 1) * gsz]], axis=0).astype(BF16)
        s = _scores(qall, kcat) + bias_sc[...]
        mg = jnp.max(s, axis=-1, keepdims=True)
        p = jnp.exp2(s - mg)
        parts.append((mg, jnp.sum(p, axis=-1, keepdims=True), _dot(p.astype(BF16), vcat)))
    m_old = m_sc[...]
    m_new = functools.reduce(jnp.maximum, [m_old] + [mg for mg, _, _ in parts])
    alpha = jnp.exp2(m_old - m_new)
    l, acc = alpha * l_sc[...], alpha * acc_sc[...]
    for mg, lg, ag in parts:
        wgt = jnp.exp2(mg - m_new)
        l, acc = l + wgt * lg, acc + wgt * ag
    m_sc[...], l_sc[...], acc_sc[...] = m_new, l, acc

    @pl.when(step == pl.num_programs(1) - 1)
    def _():
        lam = _diff_lambda(lam_ref[...], lam_init)
        o = acc_sc[...] / l_sc[...]
        for h in range(heads):
            oh = o[h * qrows:h * qrows + t, :] - lam * o[h * qrows + t:(h + 1) * qrows, :]
            o_ref[0, :, h * A_V:(h + 1) * A_V] = (
                _rms(oh, hg_ref[...]) * (1.0 - lam_init)).astype(o_ref.dtype)


def _attn_sample(q, k_new, v_new, cache_k, cache_v, page_table, layer, lam_p, head_g, n, t, lam_init):
    hw = q.shape[1]
    heads = hw // A_V
    n_layers, n_pool, page = cache_k.shape[:3]
    n_pages = page_table.shape[1]
    prow = page * heads
    pps = 16
    while n_pages % pps:
        pps //= 2
    gsz = max(pps // 4, 1)
    ck = cache_k.reshape(n_layers * n_pool * prow, A_V)
    cv = cache_v.reshape(n_layers * n_pool * prow, A_V)
    pad = ((0, 0), (0, prow - t * heads), (0, 0))
    kn = jnp.pad(k_new.astype(BF16).reshape(n, t * heads, A_V), pad)
    vn = jnp.pad(v_new.astype(BF16).reshape(n, t * heads, A_V), pad)
    base = layer * n_pool

    def page_spec(r):
        return pl.BlockSpec((prow, A_V), lambda b, i, pt: (base + pt[b, i * pps + r], 0))

    grid_spec = pltpu.PrefetchScalarGridSpec(
        num_scalar_prefetch=1,
        grid=(n, n_pages // pps),
        in_specs=[pl.BlockSpec(lam_p.shape, lambda b, i, pt: (0, 0)),
                  pl.BlockSpec((1, t, hw), lambda b, i, pt: (b, 0, 0)),
                  pl.BlockSpec((1, prow, A_V), lambda b, i, pt: (b, 0, 0)),
                  pl.BlockSpec((1, prow, A_V), lambda b, i, pt: (b, 0, 0)),
                  pl.BlockSpec((1, A_V), lambda b, i, pt: (0, 0))]
                 + [page_spec(r) for r in range(pps)] * 2,
        out_specs=pl.BlockSpec((1, t, hw), lambda b, i, pt: (b, 0, 0)),
        scratch_shapes=[pltpu.VMEM((heads * 2 * t, gsz * prow), F32),
                        pltpu.VMEM((heads * 2 * t, 1), F32), pltpu.VMEM((heads * 2 * t, 1), F32),
                        pltpu.VMEM((heads * 2 * t, A_V), F32)],
    )
    out = pl.pallas_call(
        functools.partial(_attn_sample_body, pps=pps, gsz=gsz, heads=heads, t=t, lam_init=lam_init),
        grid_spec=grid_spec,
        out_shape=jax.ShapeDtypeStruct((n, t, hw), BF16),
        compiler_params=_cparams(2),
        name="attn_sample",
    )(page_table, lam_p, q.reshape(n, t, hw), kn, vn, head_g.reshape(1, A_V),
      *([ck] * pps), *([cv] * pps))
    return out.reshape(n * t, hw)


def _shift_rows(x, d, fill):
    row = lax.broadcasted_iota(jnp.int32, x.shape, 0)
    return jnp.where(row >= d, pltpu.roll(x, d, 0), fill)


def _linear_scan(a, b):
    d = 1
    while d < a.shape[0]:
        b = b + a * _shift_rows(b, d, 0.0)
        a = a * _shift_rows(a, d, 1.0)
        d *= 2
    return a, b


def _rglru_body(xb_ref, gb_ref, conv0_ref, h0_ref, cw_ref, cb_ref, wg_ref, gbias_ref, lam_ref,
                rec_ref, convn_ref, hl_ref, ext_sc, h_sc, *, tt, w):
    halo = 8
    step = pl.program_id(1)

    @pl.when(step == 0)
    def _():
        ext_sc[0:halo, :] = jnp.zeros((halo, w), F32)
        ext_sc[halo - (B_CONV - 1):halo, :] = conv0_ref[0]
        h_sc[...] = h0_ref[0]

    x = xb_ref[0]
    ext_sc[halo:halo + tt, :] = x
    cw = cw_ref[...]
    xc = cb_ref[...] + x * cw[B_CONV - 1:B_CONV, :]
    for j in range(1, B_CONV):
        xc = xc + ext_sc[halo - j:halo - j + tt, :] * cw[B_CONV - 1 - j:B_CONV - j, :]

    pre = _dot(xc.astype(BF16), wg_ref[...]) + gbias_ref[...]
    r = jax.nn.sigmoid(pre[:, :w])
    i = jax.nn.sigmoid(pre[:, w:])
    lam = lam_ref[...]
    softplus_neg = jnp.maximum(-lam, 0.0) + jnp.log1p(jnp.exp(-jnp.abs(lam)))
    log_a = (-C_RG) * r * softplus_neg
    a = jnp.exp(log_a)
    b = jnp.sqrt(1.0 - a * a) * (i * xc)
    a_cum, h_loc = _linear_scan(a, b)
    h = h_loc + a_cum * h_sc[...]
    rec_ref[0] = (_gelu_tanh(gb_ref[0]) * h).astype(rec_ref.dtype)

    h_sc[...] = h[tt - 1:tt, :]
    hl_ref[0] = h[tt - 1:tt, :]
    convn_ref[0] = ext_sc[halo + tt - (B_CONV - 1):halo + tt, :]
    ext_sc[0:halo, :] = ext_sc[tt:tt + halo, :]


def _prep_gate_w(gate_w):
    _, nb, bs, _ = gate_w.shape
    eye = jnp.eye(nb, dtype=gate_w.dtype)
    dense = jnp.einsum('gbij,bc->gbicj', gate_w, eye).reshape(2, nb * bs, nb * bs)
    return jnp.concatenate([dense[0], dense[1]], axis=1).astype(BF16)


def _rglru(xb, gb, conv0, h0, conv_w, conv_b, gate_wd, gate_b, lam_p, n, t):
    w = xb.shape[1]
    tt = _row_tile(t, 256)
    seq = pl.BlockSpec((1, tt, w), lambda b, i: (b, i, 0))
    per_n = lambda rows: pl.BlockSpec((1, rows, w), lambda b, i: (b, 0, 0))
    return pl.pallas_call(
        functools.partial(_rglru_body, tt=tt, w=w),
        grid=(n, t // tt),
        in_specs=[seq, seq, per_n(B_CONV - 1), per_n(1), _const_spec((B_CONV, w)), _const_spec((1, w)),
                  _const_spec((w, 2 * w)), _const_spec((1, 2 * w)), _const_spec((1, w))],
        out_specs=[seq, per_n(B_CONV - 1), per_n(1)],
        out_shape=[jax.ShapeDtypeStruct((n, t, w), BF16),
                   jax.ShapeDtypeStruct((n, B_CONV - 1, w), F32),
                   jax.ShapeDtypeStruct((n, 1, w), F32)],
        scratch_shapes=[pltpu.VMEM((tt + 8, w), F32), pltpu.VMEM((1, w), F32)],
        compiler_params=_cparams(2),
        name="rglru",
    )(xb.reshape(n, t, w), gb.reshape(n, t, w), conv0, h0.reshape(n, 1, w), conv_w,
      conv_b.reshape(1, w), gate_wd, gate_b.reshape(1, 2 * w), lam_p.reshape(1, w))


def _inproj_odd_body(x_ref, g_ref, w_ref, u_ref, gin_ref, *, hw, seg):
    h = _rms(x_ref[...], g_ref[...]).astype(BF16)
    proj = _dot(h, w_ref[...])
    _store_col_tiles(u_ref, proj[:, :hw], seg)
    _store_col_tiles(gin_ref, proj[:, hw:2 * hw] * jax.nn.sigmoid(proj[:, 2 * hw:]), seg)


def _inproj_odd(x, g, w_in, seg):
    m, d = x.shape
    hw = w_in.shape[1] // 3
    tm = SUBL * seg if seg > 1 else _row_tile(m, 512)
    row = pl.BlockSpec((tm, d), lambda i: (i, 0))
    cols = pl.BlockSpec((hw // LANES, tm, LANES), lambda i: (0, i, 0))
    return pl.pallas_call(
        functools.partial(_inproj_odd_body, hw=hw, seg=seg),
        grid=(m // tm,),
        in_specs=[row, _const_spec((1, d)), _const_spec(w_in.shape)],
        out_specs=[cols, cols],
        out_shape=[jax.ShapeDtypeStruct((hw // LANES, m, LANES), F32)] * 2,
        compiler_params=_cparams(1),
        name="inproj_odd",
    )(x, g.reshape(1, d), w_in)


def _cmul(ar, ai, br, bi):
    return ar * br - ai * bi, ar * bi + ai * br


def _s5_discretise(a_re, a_im, log_dt):
    dt = jnp.exp(log_dt)
    mag = jnp.exp(dt * a_re)
    ab_re = mag * jnp.cos(dt * a_im)
    ab_im = mag * jnp.sin(dt * a_im)
    den = a_re * a_re + a_im * a_im
    nr = ab_re - 1.0
    return ab_re, ab_im, (nr * a_re + ab_im * a_im) / den, (ab_im * a_re - nr * a_im) / den


def _s5_body(u_ref, re0_ref, im0_ref, are_ref, aim_ref, ldt_ref,
             bre_ref, bim_ref, cre_ref, cim_ref, d_ref, gw_ref, gbias_ref, o_ref, sre_ref, sim_ref,
             st_re, st_im, bf_re, bf_im, lam_re, lam_im, lseg_re, lseg_im, v_re, v_im,
             *, tt, w, sw):
    seg = tt // SUBL

    def step_rows(j):
        return pl.ds(pl.multiple_of(j * SUBL, SUBL), SUBL)

    @pl.when((pl.program_id(0) == 0) & (pl.program_id(1) == 0))
    def _():
        ab_re, ab_im, coef_re, coef_im = _s5_discretise(are_ref[...], aim_ref[...], ldt_ref[...])
        f_re, f_im = _cmul(coef_re, coef_im, bre_ref[...], bim_ref[...])
        bf_re[...] = f_re.astype(BF16)
        bf_im[...] = f_im.astype(BF16)
        lam_re[...] = jnp.broadcast_to(ab_re, lam_re.shape)
        lam_im[...] = jnp.broadcast_to(ab_im, lam_im.shape)
        pr, pi = ab_re, ab_im
        for _ in range(seg.bit_length() - 1):
            pr, pi = _cmul(pr, pi, pr, pi)
        lseg_re[...] = pr
        lseg_im[...] = pi

    @pl.when(pl.program_id(1) == 0)
    def _():
        st_re[...] = re0_ref[0]
        st_im[...] = im0_ref[0]

    u = jnp.concatenate([u_ref[c, 0] for c in range(w // LANES)], axis=-1)
    ub = u.astype(BF16)
    gpt = V7X_MXU_DIM // C_STATE
    kpt = V7X_MXU_DIM // C_GROUP
    for nt in range(sw // V7X_MXU_DIM):
        kt = (nt * gpt) // kpt
        lhs = ub[:, kt * V7X_MXU_DIM:(kt + 1) * V7X_MXU_DIM]
        rows = slice(kt * V7X_MXU_DIM, (kt + 1) * V7X_MXU_DIM)
        cols = slice(nt * V7X_MXU_DIM, (nt + 1) * V7X_MXU_DIM)
        v_re[:, cols] = _dot(lhs, bf_re[rows, cols])
        v_im[:, cols] = _dot(lhs, bf_im[rows, cols])

    for ch in range(sw // S5_SCAN_LANES):
        lanes = slice(ch * S5_SCAN_LANES, (ch + 1) * S5_SCAN_LANES)
        l_re, l_im = lam_re[:, lanes], lam_im[:, lanes]

        def advance(j, sr, si, lanes=lanes, l_re=l_re, l_im=l_im):
            pr, pi = _cmul(l_re, l_im, sr, si)
            return pr + v_re[step_rows(j), lanes], pi + v_im[step_rows(j), lanes]

        zero = jnp.zeros((SUBL, S5_SCAN_LANES), F32)
        e_re, e_im = lax.fori_loop(0, seg, lambda j, c, adv=advance: adv(j, *c), (zero, zero))

        row = lax.broadcasted_iota(jnp.int32, zero.shape, 0)
        c_re, c_im = st_re[:, lanes], st_im[:, lanes]
        in_re, in_im = zero, zero
        for k in range(SUBL):
            in_re = jnp.where(row == k, c_re, in_re)
            in_im = jnp.where(row == k, c_im, in_im)
            p_re, p_im = _cmul(lseg_re[:, lanes], lseg_im[:, lanes], c_re, c_im)
            c_re, c_im = p_re + e_re[k:k + 1, :], p_im + e_im[k:k + 1, :]
        st_re[:, lanes] = c_re
        st_im[:, lanes] = c_im
        sre_ref[0, :, lanes] = c_re
        sim_ref[0, :, lanes] = c_im

        def emit(j, c, lanes=lanes, adv=advance):
            sr, si = adv(j, *c)
            v_re[step_rows(j), lanes] = sr
            v_im[step_rows(j), lanes] = si
            return sr, si

        lax.fori_loop(0, seg, emit, (in_re, in_im))

    ys = []
    for ot in range(w // V7X_MXU_DIM):
        acc = None
        for nt in range(ot * kpt // gpt, (ot + 1) * kpt // gpt):
            rows = slice(nt * V7X_MXU_DIM, (nt + 1) * V7X_MXU_DIM)
            cols = slice(ot * V7X_MXU_DIM, (ot + 1) * V7X_MXU_DIM)
            part = (_dot(v_re[:, rows].astype(BF16), cre_ref[rows, cols])
                    - _dot(v_im[:, rows].astype(BF16), cim_ref[rows, cols]))
            acc = part if acc is None else acc + part
        ys.append(acc)
    y = jnp.concatenate(ys, axis=-1) + d_ref[...] * u
    z = _gelu_tanh(y)
    res = z * jax.nn.sigmoid(_dot(z.astype(BF16), gw_ref[...]) + gbias_ref[...])
    for c in range(w // LANES):
        o_ref[c, 0] = res[:, c * LANES:(c + 1) * LANES]


def _prep_s5(b_re, b_im, c_re, c_im):
    g = b_re.shape[0]
    eye = jnp.eye(g, dtype=b_re.dtype)
    bd_in = lambda b: jnp.einsum('gpc,gh->gchp', b, eye).reshape(g * C_GROUP, g * C_STATE)
    bd_out = lambda c: jnp.einsum('gcp,gh->gphc', c, eye).reshape(g * C_STATE, g * C_GROUP).astype(BF16)
    return bd_in(b_re), bd_in(b_im), bd_out(c_re), bd_out(c_im)


def _s5(u, re0, im0, a_re, a_im, log_dt, mats, d_skip, glu_w, glu_b, n, t):
    w = u.shape[0] * LANES
    groups = w // C_GROUP
    sw = groups * C_STATE
    tt = _row_tile(t, 512)
    bre, bim, cre, cim = mats
    wc = w // LANES
    seq = pl.BlockSpec((wc, 1, tt, LANES), lambda b, i: (0, b, i, 0))
    st = pl.BlockSpec((1, 1, sw), lambda b, i: (b, 0, 0))
    vm = lambda shape, dt: pltpu.VMEM(shape, dt)
    flat = lambda a: a.reshape(1, sw)
    out, s_re, s_im = pl.pallas_call(
        functools.partial(_s5_body, tt=tt, w=w, sw=sw),
        grid=(n, t // tt),
        in_specs=[seq, st, st] + [_const_spec((1, sw))] * 3
                 + [_const_spec((w, sw)), _const_spec((w, sw)), _const_spec((sw, w)), _const_spec((sw, w)),
                    _const_spec((1, w)), _const_spec((w, w)), _const_spec((1, w))],
        out_specs=[seq, st, st],
        out_shape=[jax.ShapeDtypeStruct((wc, n, t, LANES), F32),
                   jax.ShapeDtypeStruct((n, 1, sw), F32), jax.ShapeDtypeStruct((n, 1, sw), F32)],
        scratch_shapes=[vm((1, sw), F32), vm((1, sw), F32), vm((w, sw), BF16), vm((w, sw), BF16),
                        vm((SUBL, sw), F32), vm((SUBL, sw), F32), vm((1, sw), F32), vm((1, sw), F32),
                        vm((tt, sw), F32), vm((tt, sw), F32)],
        compiler_params=_cparams(2),
        name="s5",
    )(u.reshape(wc, n, t, LANES), re0.reshape(n, 1, sw), im0.reshape(n, 1, sw),
      flat(a_re), flat(a_im), flat(jnp.repeat(log_dt, C_STATE)), bre, bim, cre, cim,
      d_skip.reshape(1, w), glu_w.astype(BF16), glu_b.reshape(1, w))
    return out.reshape(wc, n * t, LANES), s_re, s_im


def _conf_body(g_ref, conv0_ref, cw_ref, cb_ref, lng_ref, lnb_ref, o_ref, convn_ref, ext_sc, *, tt, w):
    halo = 32
    hist = D_CONV - 1
    step = pl.program_id(1)

    @pl.when(step == 0)
    def _():
        ext_sc[0:halo, :] = jnp.zeros((halo, w), F32)
        ext_sc[halo - hist:halo, :] = conv0_ref[0]

    ext_sc[halo:halo + tt, :] = jnp.concatenate([g_ref[c, 0] for c in range(w // LANES)], axis=-1)
    cw = cw_ref[...]
    c = cb_ref[...] + jnp.zeros((tt, w), F32)
    for j in range(D_CONV):
        c = c + ext_sc[halo - hist + j:halo - hist + j + tt, :] * cw[j:j + 1, :]
    o_ref[0] = _ln_silu(c, lng_ref[...], lnb_ref[...]).astype(o_ref.dtype)
    convn_ref[0] = ext_sc[halo + tt - hist:halo + tt, :]
    ext_sc[0:halo, :] = ext_sc[tt:tt + halo, :]


def _ln_silu(c, g, b):
    cc = c - jnp.mean(c, axis=-1, keepdims=True)
    y = cc * lax.rsqrt(jnp.mean(cc * cc, axis=-1, keepdims=True) + EPS) * g + b
    return y * jax.nn.sigmoid(y)


def _conf_seg_body(g_ref, conv0_ref, cw_ref, cb_ref, lng_ref, lnb_ref, o_ref, convn_ref,
                   ext_sc, prev_sc, conv_sc, *, tt, w, rows_per_chunk):
    seg = tt // SUBL
    hist = D_CONV - 1
    step = pl.program_id(1)

    @pl.when(step == 0)
    def _():
        c0 = conv0_ref[0]
        for i in range(hist):
            prev_sc[i * SUBL:(i + 1) * SUBL, :] = jnp.broadcast_to(c0[i:i + 1, :], (SUBL, w))

    cur = jnp.concatenate([g_ref[c, 0] for c in range(w // LANES)], axis=-1)
    ext_sc[hist * SUBL:, :] = cur
    row = lax.broadcasted_iota(jnp.int32, (SUBL, w), 0)
    tail = cur[(seg - hist) * SUBL:, :]
    for i in range(hist):
        grp = jnp.where(row == SUBL - 1, prev_sc[i * SUBL:(i + 1) * SUBL, :],
                        tail[i * SUBL:(i + 1) * SUBL, :])
        ext_sc[i * SUBL:(i + 1) * SUBL, :] = pltpu.roll(grp, 1, 0)
    prev_sc[...] = tail
    convn_ref[0] = jnp.concatenate(
        [tail[i * SUBL + SUBL - 1:(i + 1) * SUBL, :] for i in range(hist)], axis=0)

    cw = cw_ref[...]
    gpc = rows_per_chunk // SUBL
    for c in range(w // LANES):
        lanes = slice(c * LANES, (c + 1) * LANES)
        taps = [jnp.broadcast_to(cw[k:k + 1, lanes], (SUBL, LANES)) for k in range(D_CONV)]
        bias = jnp.broadcast_to(cb_ref[:, lanes], (SUBL, LANES))

        def chunk(i, _, lanes=lanes, taps=taps, bias=bias):
            base = pl.multiple_of(i * rows_per_chunk, rows_per_chunk)
            accs = [bias] * gpc
            for g in range(gpc + D_CONV - 1):
                xg = ext_sc[pl.ds(base + g * SUBL, SUBL), lanes]
                for jj in range(gpc):
                    if 0 <= g - jj < D_CONV:
                        accs[jj] = accs[jj] + xg * taps[g - jj]
            conv_sc[pl.ds(base, rows_per_chunk), lanes] = jnp.concatenate(accs, axis=0)
            return 0

        lax.fori_loop(0, tt // rows_per_chunk, chunk, 0)

    y = _ln_silu(conv_sc[...], lng_ref[...], lnb_ref[...])
    for c in range(w // LANES):
        o_ref[c, 0] = y[:, c * LANES:(c + 1) * LANES]


def _conf(g_in, conv0, conv_w, conv_b, ln_g, ln_b, n, t, seg):
    wc = g_in.shape[0]
    w = wc * LANES
    hist = pl.BlockSpec((1, D_CONV - 1, w), lambda b, i: (b, 0, 0))
    params = (conv0, conv_w, conv_b.reshape(1, w), ln_g.reshape(1, w), ln_b.reshape(1, w))
    param_specs = [hist, _const_spec((D_CONV, w)), _const_spec((1, w)), _const_spec((1, w)),
                   _const_spec((1, w))]
    hist_shape = jax.ShapeDtypeStruct((n, D_CONV - 1, w), F32)
    if seg == 1:
        tt = _row_tile(t, 256)
        cols = pl.BlockSpec((wc, 1, tt, LANES), lambda b, i: (0, b, i, 0))
        out, cd = pl.pallas_call(
            functools.partial(_conf_body, tt=tt, w=w),
            grid=(n, t // tt),
            in_specs=[cols] + param_specs,
            out_specs=[pl.BlockSpec((1, tt, w), lambda b, i: (b, i, 0)), hist],
            out_shape=[jax.ShapeDtypeStruct((n, t, w), BF16), hist_shape],
            scratch_shapes=[pltpu.VMEM((tt + 32, w), F32)],
            compiler_params=_cparams(2),
            name="conformer_conv",
        )(g_in.reshape(wc, n, t, LANES), *params)
        return out.reshape(n * t, w), cd
    assert seg >= D_CONV - 1
    tt = SUBL * seg
    cols = pl.BlockSpec((wc, 1, tt, LANES), lambda b, i: (0, b, i, 0))
    out, cd = pl.pallas_call(
        functools.partial(_conf_seg_body, tt=tt, w=w, rows_per_chunk=64),
        grid=(n, t // tt),
        in_specs=[cols] + param_specs,
        out_specs=[cols, hist],
        out_shape=[jax.ShapeDtypeStruct((wc, n, t, LANES), F32), hist_shape],
        scratch_shapes=[pltpu.VMEM(((seg + D_CONV - 1) * SUBL, w), F32),
                        pltpu.VMEM(((D_CONV - 1) * SUBL, w), F32), pltpu.VMEM((tt, w), F32)],
        compiler_params=_cparams(2),
        name="conformer_conv_seg",
    )(g_in.reshape(wc, n, t, LANES), *params)
    return out.reshape(wc, n * t, LANES), cd


def _run_group(x3, cache, conv_b0, h_b0, c_re0, c_im0, conv_d0, p, prepped):
    n, t, d = x3.shape
    depth = p['ffn1_g'].shape[0]
    x = x3.reshape(n * t, d)
    ks, vs, cbs, hbs, cres, cims, cds = [], [], [], [], [], [], []
    for l in range(depth):
        x = _ffn(x, p['ffn1_g'][l], prepped['ffn1'][l])
        if l % 2 == 0:
            e = l // 2
            hw = p['even_w_in'].shape[2] // 5
            heads = hw // A_V
            q, k, kb, v, vb, xb, gb = _inproj_even(x, p['mix_g'][l], prepped['even_w_in'][e],
                                                   p['a_q_g'][e], p['a_k_g'][e])
            lam_init = 0.8 - 0.6 * math.exp(-0.3 * l)
            if cache is None:
                att = _attn_prompt(q, kb, vb, p['a_lambda'][e], p['a_head_g'][e], n, t, lam_init)
            else:
                cache_k, cache_v, page_table = cache
                att = _attn_sample(q, k, v, cache_k, cache_v, page_table, e, p['a_lambda'][e],
                                   p['a_head_g'][e], n, t, lam_init)
            rec, cb, hb = _rglru(xb, gb, conv_b0[e], h_b0[e], p['b_conv_w'][e], p['b_conv_b'][e],
                                 prepped['b_gate_w'][e], p['b_gate_b'][e], p['b_lambda'][e], n, t)
            mixer, mixer_seg = (att, rec.reshape(n * t, -1), prepped['even_w_out'][e]), 1
            ks.append(k.reshape(n, t, heads, 2 * A_QK))
            vs.append(v.reshape(n, t, heads, A_V))
            cbs.append(cb)
            hbs.append(hb.reshape(n, hw))
        else:
            o = l // 2
            seg = SEG_BLOCK // SUBL if t % SEG_BLOCK == 0 else 1
            u, g_in = _inproj_odd(x, p['mix_g'][l], prepped['odd_w_in'][o], seg)
            groups = u.shape[0] * LANES // C_GROUP
            c_out, cre, cim = _s5(u, c_re0[o], c_im0[o], p['c_a_re'][o], p['c_a_im'][o], p['c_log_dt'][o],
                                  prepped['s5'][o], p['c_d'][o], p['c_glu_w'][o], p['c_glu_b'][o], n, t)
            d_out, cd = _conf(g_in, conv_d0[o], p['d_conv_w'][o], p['d_conv_b'][o],
                              p['d_ln_g'][o], p['d_ln_b'][o], n, t, seg)
            mixer, mixer_seg = (c_out, d_out, prepped['odd_w_out'][o]), seg
            cres.append(cre.reshape(n, groups, C_STATE))
            cims.append(cim.reshape(n, groups, C_STATE))
            cds.append(cd)
        x = _ffn(x, p['ffn2_g'][l], prepped['ffn2'][l], mixer, mixer_seg)
    return (x.reshape(n, t, d), jnp.stack(ks), jnp.stack(vs), jnp.stack(cbs), jnp.stack(hbs),
            jnp.stack(cres), jnp.stack(cims), jnp.stack(cds))


def kernel(x_prompt, x_sample, cache_k, cache_v, state_conv_b, state_h_b, state_c_re, state_c_im,
           state_conv_d, page_table, ffn1_g, ffn1_w_gu, ffn1_w_down, mix_g, ffn2_g, ffn2_w_gu,
           ffn2_w_down, even_w_in, even_w_out, a_q_g, a_k_g, a_lambda, a_head_g, b_conv_w, b_conv_b,
           b_gate_w, b_gate_b, b_lambda, odd_w_in, odd_w_out, c_a_re, c_a_im, c_log_dt, c_b_re, c_b_im,
           c_c_re, c_c_im, c_d, c_glu_w, c_glu_b, d_conv_w, d_conv_b, d_ln_g, d_ln_b):
    p = dict(ffn1_g=ffn1_g, mix_g=mix_g, ffn2_g=ffn2_g, even_w_in=even_w_in, a_q_g=a_q_g, a_k_g=a_k_g,
             a_lambda=a_lambda, a_head_g=a_head_g, b_conv_w=b_conv_w, b_conv_b=b_conv_b,
             b_gate_b=b_gate_b, b_lambda=b_lambda, c_a_re=c_a_re, c_a_im=c_a_im, c_log_dt=c_log_dt,
             c_d=c_d, c_glu_w=c_glu_w, c_glu_b=c_glu_b, d_conv_w=d_conv_w, d_conv_b=d_conv_b,
             d_ln_g=d_ln_g, d_ln_b=d_ln_b)
    depth = ffn1_g.shape[0]
    n_even, n_odd = even_w_in.shape[0], odd_w_in.shape[0]
    prepped = dict(
        ffn1=[_prep_ffn(ffn1_w_gu[l], ffn1_w_down[l]) for l in range(depth)],
        ffn2=[_prep_ffn(ffn2_w_gu[l], ffn2_w_down[l]) for l in range(depth)],
        even_w_in=[even_w_in[e].astype(BF16) for e in range(n_even)],
        even_w_out=[even_w_out[e].astype(BF16) for e in range(n_even)],
        odd_w_in=[odd_w_in[o].astype(BF16) for o in range(n_odd)],
        odd_w_out=[odd_w_out[o].astype(BF16) for o in range(n_odd)],
        b_gate_w=[_prep_gate_w(b_gate_w[e]) for e in range(n_even)],
        s5=[_prep_s5(c_b_re[o], c_b_im[o], c_c_re[o], c_c_im[o]) for o in range(n_odd)],
    )
    b = x_prompt.shape[0]
    dt = x_prompt.dtype
    hw = even_w_in.shape[2] // 5
    groups, states = state_c_re.shape[2], state_c_re.shape[3]
    prompt = _run_group(
        x_prompt, None,
        jnp.zeros((n_even, b, B_CONV - 1, hw), dt), jnp.zeros((n_even, b, hw), dt),
        jnp.zeros((n_odd, b, groups, states), dt), jnp.zeros((n_odd, b, groups, states), dt),
        jnp.zeros((n_odd, b, D_CONV - 1, hw), dt), p, prepped)
    sample = _run_group(
        x_sample, (cache_k, cache_v, page_table), state_conv_b, state_h_b, state_c_re, state_c_im,
        state_conv_d, p, prepped)
    return (prompt[0], sample[0]) + prompt[1:] + sample[1:]
```

```python
import functools
import math

import jax
import jax.numpy as jnp
from jax import lax
from jax.experimental import pallas as pl
from jax.experimental.pallas import tpu as pltpu

F32 = jnp.float32
BF16 = jnp.bfloat16
EPS = 1e-6
NEG = -1e30
LOG2E = 1.4426950408889634
ATTN_SCORE_BOUND = 64.0

LANES = 128
SUBL = 8
V7X_MXU_DIM = 256
V7X_VMEM_LIMIT = 56 * 1024 * 1024

A_HEADS = 4
A_QK = 64
A_V = 128
B_BLOCKS = 8
B_CONV = 4
C_RG = 8.0
C_GROUP = 16
C_STATE = 64
D_CONV = 31
SEG_BLOCK = 512
S5_SCAN_LANES = 1024


def _cparams(n_axes):
    return pltpu.CompilerParams(dimension_semantics=("arbitrary",) * n_axes,
                                vmem_limit_bytes=V7X_VMEM_LIMIT)


def _const_spec(shape):
    nd = len(shape)
    return pl.BlockSpec(shape, lambda *_: (0,) * nd)


def _rms(x, g):
    return x * lax.rsqrt(jnp.mean(x * x, axis=-1, keepdims=True) + EPS) * g


def _gelu_tanh(x):
    return 0.5 * x * (1.0 + jnp.tanh(math.sqrt(2.0 / math.pi) * (x + 0.044715 * (x * x * x))))


def _dot(a, b):
    return jnp.dot(a, b, preferred_element_type=F32)


def _row_tile(m, want):
    t = min(m, want)
    while m % t:
        t //= 2
    return t


def _store_col_tiles(ref, val, seg):
    for c in range(val.shape[1] // LANES):
        tile = val[:, c * LANES:(c + 1) * LANES]
        if seg == 1:
            ref[c] = tile
        else:
            for s in range(SUBL):
                ref[c, pl.ds(s, seg, stride=SUBL), :] = tile[s * seg:(s + 1) * seg, :]


def _load_col_tiles(ref, seg):
    cols = []
    for c in range(ref.shape[0]):
        if seg == 1:
            cols.append(ref[c])
        else:
            cols.append(jnp.concatenate(
                [ref[c, pl.ds(s, seg, stride=SUBL), :] for s in range(SUBL)], axis=0))
    return jnp.concatenate(cols, axis=-1)


def _mixer_half(ref, seg):
    if len(ref.shape) == 2:
        return ref[...]
    return _load_col_tiles(ref, seg).astype(BF16)


def _ffn_body(x_ref, g_ref, wg_ref, wu_ref, wd_ref, *rest, ck, with_mixer, seg):
    x = x_ref[...]
    if with_mixer:
        a_ref, b_ref, wo_ref, o_ref = rest
        hw = wo_ref.shape[0] // 2
        x = (x + _dot(_mixer_half(a_ref, seg), wo_ref[:hw, :])
             + _dot(_mixer_half(b_ref, seg), wo_ref[hw:, :]))
    else:
        o_ref, = rest
    h = _rms(x, g_ref[...]).astype(BF16)
    acc = jnp.zeros(x.shape, F32)
    for c in range(wg_ref.shape[1] // ck):
        sl = slice(c * ck, (c + 1) * ck)
        gate = _dot(h, wg_ref[:, sl])
        up = _dot(h, wu_ref[:, sl])
        a = (gate * jax.nn.sigmoid(gate) * up).astype(BF16)
        acc = acc + _dot(a, wd_ref[sl, :])
    o_ref[...] = x + 0.5 * acc


def _split_gu_body(w_ref, g_ref, u_ref, *, f, fp):
    x = w_ref[0]
    lane = lax.broadcasted_iota(jnp.int32, (x.shape[0], fp), 1)
    g_ref[0] = jnp.where(lane < f, x[:, :fp], 0.0).astype(BF16)
    start = f // LANES * LANES
    up = pltpu.roll(x[:, start:start + fp], fp - (f - start), 1)
    u_ref[0] = jnp.where(lane < f, up, 0.0).astype(BF16)


def _pad_rows_body(w_ref, o_ref, *, f, rows):
    r = lax.broadcasted_iota(jnp.int32, w_ref.shape[1:], 0) + pl.program_id(1) * rows
    o_ref[0] = jnp.where(r < f, w_ref[0], 0.0).astype(BF16)


def _prep_ffn(w_gu, w_down):
    layers, d, f2 = w_gu.shape
    f = f2 // 2
    fp = -(-f // V7X_MXU_DIM) * V7X_MXU_DIM
    start = f // LANES * LANES
    if start + fp > f2 or d % V7X_MXU_DIM:
        pad_c = ((0, 0), (0, 0), (0, fp - f))
        return (jnp.pad(w_gu[..., :f], pad_c).astype(BF16), jnp.pad(w_gu[..., f:], pad_c).astype(BF16),
                jnp.pad(w_down, ((0, 0), (0, fp - f), (0, 0))).astype(BF16))
    rows = V7X_MXU_DIM
    half = pl.BlockSpec((1, rows, fp), lambda l, i: (l, i, 0))
    wg, wu = pl.pallas_call(
        functools.partial(_split_gu_body, f=f, fp=fp),
        grid=(layers, d // rows),
        in_specs=[pl.BlockSpec((1, rows, f2), lambda l, i: (l, i, 0))],
        out_specs=[half, half],
        out_shape=[jax.ShapeDtypeStruct((layers, d, fp), BF16)] * 2,
        compiler_params=_cparams(2),
        name="ffn_split_gate_up",
    )(w_gu)
    blk = pl.BlockSpec((1, rows, d), lambda l, i: (l, i, 0))
    wd = pl.pallas_call(
        functools.partial(_pad_rows_body, f=f, rows=rows),
        grid=(layers, fp // rows),
        in_specs=[blk],
        out_specs=blk,
        out_shape=jax.ShapeDtypeStruct((layers, fp, d), BF16),
        compiler_params=_cparams(2),
        name="ffn_pad_down",
    )(w_down)
    return wg, wu, wd


def _ffn(x, g, w, layer, mixer=None, seg=1):
    wg, wu, wd = w
    m, d = x.shape
    fp = wg.shape[2]
    tm = SUBL * seg if seg > 1 else _row_tile(m, 512)
    row = pl.BlockSpec((tm, d), lambda i: (i, 0))
    of_layer = lambda r, c: pl.BlockSpec((None, r, c), lambda i: (layer, 0, 0))
    in_specs = [row, _const_spec((1, d)), of_layer(d, fp), of_layer(d, fp), of_layer(fp, d)]
    args = [x, g.reshape(1, d), wg, wu, wd]
    if mixer is not None:
        a, b, w_out = mixer
        hw = w_out.shape[0] // 2
        spec = lambda v: (pl.BlockSpec((tm, hw), lambda i: (i, 0)) if v.ndim == 2 else
                          pl.BlockSpec((v.shape[0], tm, LANES), lambda i: (0, i, 0)))
        in_specs += [spec(a), spec(b), _const_spec((2 * hw, d))]
        args += [a, b, w_out]
    return pl.pallas_call(
        functools.partial(_ffn_body, ck=V7X_MXU_DIM, with_mixer=mixer is not None, seg=seg),
        grid=(m // tm,),
        in_specs=in_specs,
        out_specs=row,
        out_shape=jax.ShapeDtypeStruct((m, d), F32),
        compiler_params=_cparams(1),
        name="ffn_mix" if mixer is not None else "ffn",
    )(*args)


def _inproj_even_body(x_ref, g_ref, w_ref, gq_ref, gk_ref, ones_ref,
                      q_ref, k_ref, kb_ref, v_ref, vb_ref, xb_ref, gb_ref, *, hw, tm, seg):
    x = x_ref[...]
    h = _rms(x, g_ref[...]).astype(BF16)
    proj = _dot(h, w_ref[...])

    def group_rms(t, gain):
        ss = t * t
        hi = ss.astype(BF16)
        lo = (ss - hi.astype(F32)).astype(BF16)
        parts = []
        for c in range(hw // V7X_MXU_DIM):
            sl = slice(c * V7X_MXU_DIM, (c + 1) * V7X_MXU_DIM)
            parts.append(_dot(hi[:, sl], ones_ref[...]) + _dot(lo[:, sl], ones_ref[...]))
        gs = jnp.concatenate(parts, axis=-1)
        return t * lax.rsqrt(gs * (1.0 / A_QK) + EPS) * gain

    q = group_rms(proj[:, 0:hw], gq_ref[...]) * (A_QK ** -0.5 * LOG2E)
    k = group_rms(proj[:, hw:2 * hw], gk_ref[...])
    v = proj[:, 2 * hw:3 * hw]
    q_ref[...] = q.astype(BF16)
    kb_ref[...] = k.astype(BF16)
    vb_ref[...] = v.astype(BF16)
    heads = hw // A_V
    for hd in range(heads):
        k_ref[pl.ds(hd, tm, stride=heads), :] = k[:, hd * A_V:(hd + 1) * A_V]
        v_ref[pl.ds(hd, tm, stride=heads), :] = v[:, hd * A_V:(hd + 1) * A_V]
    _store_col_tiles(xb_ref, proj[:, 3 * hw:4 * hw], seg)
    _store_col_tiles(gb_ref, proj[:, 4 * hw:5 * hw], seg)


def _inproj_even(x, g, w_in, gq, gk, seg):
    m, d = x.shape
    hw = w_in.shape[1] // 5
    heads = hw // A_V
    tm = SUBL * seg if seg > 1 else _row_tile(m, 512)
    row = pl.BlockSpec((tm, d), lambda i: (i, 0))
    orow = pl.BlockSpec((tm, hw), lambda i: (i, 0))
    hrow = pl.BlockSpec((tm * heads, A_V), lambda i: (i, 0))
    cols = pl.BlockSpec((hw // LANES, tm, LANES), lambda i: (0, i, 0))
    idx = jnp.arange(V7X_MXU_DIM) // A_QK
    ones_bd = (idx[:, None] == idx[None, :]).astype(BF16)
    tile = lambda t: jnp.tile(t, hw // A_QK).reshape(1, hw)
    sds = lambda dt: jax.ShapeDtypeStruct((m, hw), dt)
    hsds = jax.ShapeDtypeStruct((m * heads, A_V), F32)
    csds = jax.ShapeDtypeStruct((hw // LANES, m, LANES), F32)
    return pl.pallas_call(
        functools.partial(_inproj_even_body, hw=hw, tm=tm, seg=seg),
        grid=(m // tm,),
        in_specs=[row, _const_spec((1, d)), _const_spec(w_in.shape), _const_spec((1, hw)),
                  _const_spec((1, hw)), _const_spec(ones_bd.shape)],
        out_specs=[orow, hrow, orow, hrow, orow, cols, cols],
        out_shape=[sds(BF16), hsds, sds(BF16), hsds, sds(BF16), csds, csds],
        compiler_params=_cparams(1),
        name="inproj_even",
    )(x, g.reshape(1, d), w_in, tile(gq), tile(gk), ones_bd)


def _diff_lambda(lv, lam_init):
    s01 = jnp.sum(lv[0:1, :] * lv[1:2, :], axis=-1, keepdims=True)
    s23 = jnp.sum(lv[2:3, :] * lv[3:4, :], axis=-1, keepdims=True)
    return jnp.exp(s01) - jnp.exp(s23) + lam_init


def _split_maps(q):
    lane = lax.broadcasted_iota(jnp.int32, q.shape, 1)
    zero = jnp.zeros_like(q)
    return jnp.where(lane < A_QK, q, zero), jnp.where(lane >= A_QK, q, zero)


def _scores(q, k):
    return lax.dot_general(q, k, (((1,), (1,)), ((), ())), preferred_element_type=F32)


def _attn_prompt_body(lam_ref, q_ref, k_ref, v_ref, hg_ref, o_ref, vext_sc, kn2_sc,
                      *, tq, hps, lam_init):
    qi = pl.program_id(2)

    def max_row_norm2(x):
        xf = x.astype(F32)
        per_head = [jnp.sum(xf[:, hh * A_V:(hh + 1) * A_V] ** 2, axis=-1, keepdims=True)
                    for hh in range(hps)]
        return jnp.max(functools.reduce(jnp.maximum, per_head), axis=0, keepdims=True)

    @pl.when(qi == 0)
    def _():
        for hh in range(hps):
            vext_sc[:, 2 * hh * A_V:(2 * hh + 1) * A_V] = v_ref[:, hh * A_V:(hh + 1) * A_V]
            vext_sc[:, (2 * hh + 1) * A_V:(2 * hh + 2) * A_V] = jnp.ones((v_ref.shape[0], A_V), BF16)
        kn2_sc[...] = max_row_norm2(k_ref[...])

    lam = _diff_lambda(lam_ref[...], lam_init)
    q = q_ref[...]
    qs = [m for hh in range(hps) for m in _split_maps(q[:, hh * A_V:(hh + 1) * A_V])]

    def chain_inputs(kb, c):
        rows = pl.ds(pl.multiple_of(kb * tq, tq), tq)
        hh = c // 2
        s = _scores(qs[c], k_ref[rows, hh * A_V:(hh + 1) * A_V])
        return s, vext_sc[rows, 2 * hh * A_V:(2 * hh + 2) * A_V]

    def causal(shape):
        return lax.broadcasted_iota(jnp.int32, shape, 1) <= lax.broadcasted_iota(jnp.int32, shape, 0)

    def finish(accs):
        for hh in range(hps):
            a0, a1 = accs[2 * hh], accs[2 * hh + 1]
            o = a0[:, :A_V] / a0[:, A_V:] - lam * (a1[:, :A_V] / a1[:, A_V:])
            o_ref[:, hh * A_V:(hh + 1) * A_V] = (
                _rms(o, hg_ref[...]) * (1.0 - lam_init)).astype(o_ref.dtype)

    def bounded():
        def step(kb, accs, diag):
            new = []
            for c in range(2 * hps):
                s, v = chain_inputs(kb, c)
                p = jnp.exp2(s)
                if diag:
                    p = jnp.where(causal(p.shape), p, 0.0)
                new.append(accs[c] + _dot(p.astype(BF16), v))
            return tuple(new)

        zero = jnp.zeros((tq, 2 * A_V), F32)
        accs = lax.fori_loop(0, qi, lambda kb, a: step(kb, a, False), (zero,) * (2 * hps))
        finish(step(qi, accs, True))

    def general():
        def step(kb, carry, diag):
            new = []
            for c in range(2 * hps):
                m, acc = carry[c]
                s, v = chain_inputs(kb, c)
                if diag:
                    s = jnp.where(causal(s.shape), s, NEG)
                m_new = jnp.maximum(m, jnp.max(s, axis=-1, keepdims=True))
                p = jnp.exp2(s - m_new)
                new.append((m_new, jnp.exp2(m - m_new) * acc + _dot(p.astype(BF16), v)))
            return tuple(new)

        init_one = (jnp.full((tq, 1), NEG, F32), jnp.zeros((tq, 2 * A_V), F32))
        carry = lax.fori_loop(0, qi, lambda kb, c: step(kb, c, False), (init_one,) * (2 * hps))
        finish([acc for _, acc in step(qi, carry, True)])

    bound2 = max_row_norm2(q) * kn2_sc[...]
    lax.cond(bound2[0, 0] <= ATTN_SCORE_BOUND ** 2, bounded, general)


def _attn_prompt(q, k, v, lam_p, head_g, n, t, lam_init):
    hw = q.shape[1]
    heads = hw // A_V
    hps = next(c for c in (4, 2, 1) if heads % c == 0)
    tq = _row_tile(t, 512)
    nq = t // tq
    qspec = pl.BlockSpec((tq, hps * A_V), lambda b, h, i: (b * nq + i, h))
    kspec = pl.BlockSpec((t, hps * A_V), lambda b, h, i: (b, h))
    return pl.pallas_call(
        functools.partial(_attn_prompt_body, tq=tq, hps=hps, lam_init=lam_init),
        grid=(n, heads // hps, nq),
        in_specs=[_const_spec(lam_p.shape), qspec, kspec, kspec, _const_spec((1, A_V))],
        out_specs=qspec,
        out_shape=jax.ShapeDtypeStruct((n * t, hw), BF16),
        scratch_shapes=[pltpu.VMEM((t, hps * 2 * A_V), BF16), pltpu.VMEM((1, 1), F32)],
        compiler_params=_cparams(3),
        name="attn_prompt",
    )(lam_p, q, k, v, head_g.reshape(1, A_V))


def _attn_sample_body(pt_ref, lam_ref, q_ref, kn_ref, vn_ref, hg_ref, *rest,
                      pps, gsz, heads, t, lam_init):
    kp, vp = rest[:pps], rest[pps:2 * pps]
    o_ref = rest[2 * pps]
    bias_sc, m_sc, l_sc, acc_sc = rest[2 * pps + 1:]
    seq, step = pl.program_id(0), pl.program_id(1)
    qrows = 2 * t
    q = q_ref[0]
    qall = jnp.concatenate(
        [piece for h in range(heads) for piece in _split_maps(q[:, h * A_V:(h + 1) * A_V])], axis=0)

    def head_match(shape):
        row = lax.broadcasted_iota(jnp.int32, shape, 0)
        col = lax.broadcasted_iota(jnp.int32, shape, 1)
        return row, col, (col % heads) == (row // qrows)

    def update(s, v):
        m = m_sc[...]
        m_new = jnp.maximum(m, jnp.max(s, axis=-1, keepdims=True))
        alpha = jnp.exp2(m - m_new)
        p = jnp.exp2(s - m_new)
        l_sc[...] = alpha * l_sc[...] + jnp.sum(p, axis=-1, keepdims=True)
        acc_sc[...] = alpha * acc_sc[...] + _dot(p.astype(BF16), v)
        m_sc[...] = m_new

    @pl.when((seq == 0) & (step == 0))
    def _():
        _, _, ok = head_match(bias_sc.shape)
        bias_sc[...] = jnp.where(ok, 0.0, NEG)

    @pl.when(step == 0)
    def _():
        m_sc[...] = jnp.full(m_sc.shape, NEG, F32)
        l_sc[...] = jnp.zeros(l_sc.shape, F32)
        acc_sc[...] = jnp.zeros(acc_sc.shape, F32)
        s = _scores(qall, kn_ref[0])
        row, col, ok = head_match(s.shape)
        ok = ok & ((col // heads) <= (row % t))
        update(jnp.where(ok, s, NEG), vn_ref[0])

    parts = []
    for g in range(pps // gsz):
        kcat = jnp.concatenate([r[...] for r in kp[g * gsz:(g + 1) * gsz]], axis=0).astype(BF16)
        vcat = jnp.concatenate([r[...] for r in vp[g * gsz:(g + 1) * gsz]], axis=0).astype(BF16)
        s = _scores(qall, kcat) + bias_sc[...]
        mg = jnp.max(s, axis=-1, keepdims=True)
        p = jnp.exp2(s - mg)
        parts.append((mg, jnp.sum(p, axis=-1, keepdims=True), _dot(p.astype(BF16), vcat)))
    m_old = m_sc[...]
    m_new = functools.reduce(jnp.maximum, [m_old] + [mg for mg, _, _ in parts])
    alpha = jnp.exp2(m_old - m_new)
    l, acc = alpha * l_sc[...], alpha * acc_sc[...]
    for mg, lg, ag in parts:
        wgt = jnp.exp2(mg - m_new)
        l, acc = l + wgt * lg, acc + wgt * ag
    m_sc[...], l_sc[...], acc_sc[...] = m_new, l, acc

    @pl.when(step == pl.num_programs(1) - 1)
    def _():
        lam = _diff_lambda(lam_ref[...], lam_init)
        o = acc_sc[...] / l_sc[...]
        for h in range(heads):
            oh = o[h * qrows:h * qrows + t, :] - lam * o[h * qrows + t:(h + 1) * qrows, :]
            o_ref[0, :, h * A_V:(h + 1) * A_V] = (
                _rms(oh, hg_ref[...]) * (1.0 - lam_init)).astype(o_ref.dtype)


def _attn_sample(q, k_new, v_new, cache_k, cache_v, page_table, layer, lam_p, head_g, n, t, lam_init):
    hw = q.shape[1]
    heads = hw // A_V
    n_layers, n_pool, page = cache_k.shape[:3]
    n_pages = page_table.shape[1]
    prow = page * heads
    pps = 16
    while n_pages % pps:
        pps //= 2
    gsz = pps
    ck = cache_k.reshape(n_layers * n_pool * prow, A_V)
    cv = cache_v.reshape(n_layers * n_pool * prow, A_V)
    pad = ((0, 0), (0, prow - t * heads), (0, 0))
    kn = jnp.pad(k_new.astype(BF16).reshape(n, t * heads, A_V), pad)
    vn = jnp.pad(v_new.astype(BF16).reshape(n, t * heads, A_V), pad)
    base = layer * n_pool

    def page_spec(r):
        return pl.BlockSpec((prow, A_V), lambda b, i, pt: (base + pt[b, i * pps + r], 0))

    grid_spec = pltpu.PrefetchScalarGridSpec(
        num_scalar_prefetch=1,
        grid=(n, n_pages // pps),
        in_specs=[pl.BlockSpec(lam_p.shape, lambda b, i, pt: (0, 0)),
                  pl.BlockSpec((1, t, hw), lambda b, i, pt: (b, 0, 0)),
                  pl.BlockSpec((1, prow, A_V), lambda b, i, pt: (b, 0, 0)),
                  pl.BlockSpec((1, prow, A_V), lambda b, i, pt: (b, 0, 0)),
                  pl.BlockSpec((1, A_V), lambda b, i, pt: (0, 0))]
                 + [page_spec(r) for r in range(pps)] * 2,
        out_specs=pl.BlockSpec((1, t, hw), lambda b, i, pt: (b, 0, 0)),
        scratch_shapes=[pltpu.VMEM((heads * 2 * t, gsz * prow), F32),
                        pltpu.VMEM((heads * 2 * t, 1), F32), pltpu.VMEM((heads * 2 * t, 1), F32),
                        pltpu.VMEM((heads * 2 * t, A_V), F32)],
    )
    out = pl.pallas_call(
        functools.partial(_attn_sample_body, pps=pps, gsz=gsz, heads=heads, t=t, lam_init=lam_init),
        grid_spec=grid_spec,
        out_shape=jax.ShapeDtypeStruct((n, t, hw), BF16),
        compiler_params=_cparams(2),
        name="attn_sample",
    )(page_table, lam_p, q.reshape(n, t, hw), kn, vn, head_g.reshape(1, A_V),
      *([ck] * pps), *([cv] * pps))
    return out.reshape(n * t, hw)


def _shift_rows(x, d, fill):
    row = lax.broadcasted_iota(jnp.int32, x.shape, 0)
    return jnp.where(row >= d, pltpu.roll(x, d, 0), fill)


def _linear_scan(a, b):
    d = 1
    while d < a.shape[0]:
        b = b + a * _shift_rows(b, d, 0.0)
        a = a * _shift_rows(a, d, 1.0)
        d *= 2
    return a, b


def _rglru_body(xb_ref, gb_ref, conv0_ref, h0_ref, cw_ref, cb_ref, wg_ref, gbias_ref, lam_ref,
                rec_ref, convn_ref, hl_ref, ext_sc, h_sc, *, tt, w):
    halo = 8
    step = pl.program_id(1)

    @pl.when(step == 0)
    def _():
        ext_sc[0:halo, :] = jnp.zeros((halo, w), F32)
        ext_sc[halo - (B_CONV - 1):halo, :] = conv0_ref[0]
        h_sc[...] = h0_ref[0]

    x = jnp.concatenate([xb_ref[c, 0] for c in range(w // LANES)], axis=-1)
    gb = jnp.concatenate([gb_ref[c, 0] for c in range(w // LANES)], axis=-1)
    ext_sc[halo:halo + tt, :] = x
    cw = cw_ref[...]
    xc = cb_ref[...] + x * cw[B_CONV - 1:B_CONV, :]
    for j in range(1, B_CONV):
        xc = xc + ext_sc[halo - j:halo - j + tt, :] * cw[B_CONV - 1 - j:B_CONV - j, :]

    a, b = _rglru_coeffs(xc, wg_ref[...], gbias_ref[...], lam_ref[...], w)
    a_cum, h_loc = _linear_scan(a, b)
    h = h_loc + a_cum * h_sc[...]
    rec_ref[0] = (_gelu_tanh(gb) * h).astype(rec_ref.dtype)

    h_sc[...] = h[tt - 1:tt, :]
    hl_ref[0] = h[tt - 1:tt, :]
    convn_ref[0] = ext_sc[halo + tt - (B_CONV - 1):halo + tt, :]
    ext_sc[0:halo, :] = ext_sc[tt:tt + halo, :]


def _rglru_coeffs(xc, gate_w, gate_b, lam, w):
    pre = _dot(xc.astype(BF16), gate_w) + gate_b
    r = jax.nn.sigmoid(pre[:, :w])
    i = jax.nn.sigmoid(pre[:, w:])
    softplus_neg = jnp.maximum(-lam, 0.0) + jnp.log1p(jnp.exp(-jnp.abs(lam)))
    a = jnp.exp((-C_RG) * r * softplus_neg)
    return a, jnp.sqrt(1.0 - a * a) * (i * xc)


def _rglru_seg_body(xb_ref, gb_ref, conv0_ref, h0_ref, cw_ref, cb_ref, wg_ref, gbias_ref, lam_ref,
                    rec_ref, convn_ref, hl_ref, prev_sc, h_sc, *, tt, w):
    seg = tt // SUBL
    hist = B_CONV - 1
    step = pl.program_id(1)

    @pl.when(step == 0)
    def _():
        c0 = conv0_ref[0]
        for i in range(hist):
            prev_sc[i * SUBL:(i + 1) * SUBL, :] = jnp.broadcast_to(c0[i:i + 1, :], (SUBL, w))
        h_sc[...] = h0_ref[0]

    cur = jnp.concatenate([xb_ref[c, 0] for c in range(w // LANES)], axis=-1)
    gb = jnp.concatenate([gb_ref[c, 0] for c in range(w // LANES)], axis=-1)
    row = lax.broadcasted_iota(jnp.int32, (SUBL, w), 0)
    tail = cur[(seg - hist) * SUBL:, :]
    halo = [pltpu.roll(jnp.where(row == SUBL - 1, prev_sc[i * SUBL:(i + 1) * SUBL, :],
                                 tail[i * SUBL:(i + 1) * SUBL, :]), 1, 0) for i in range(hist)]
    ext = jnp.concatenate(halo + [cur], axis=0)
    prev_sc[...] = tail
    convn_ref[0] = jnp.concatenate(
        [tail[i * SUBL + SUBL - 1:(i + 1) * SUBL, :] for i in range(hist)], axis=0)

    cw = cw_ref[...]
    xc = cb_ref[...] + jnp.zeros((tt, w), F32)
    for k in range(B_CONV):
        xc = xc + ext[k * SUBL:k * SUBL + tt, :] * cw[k:k + 1, :]
    a, b = _rglru_coeffs(xc, wg_ref[...], gbias_ref[...], lam_ref[...], w)

    grp = lambda v, j: v[j * SUBL:(j + 1) * SUBL, :]
    e, prod = grp(b, 0), grp(a, 0)
    for j in range(1, seg):
        e, prod = grp(a, j) * e + grp(b, j), grp(a, j) * prod
    c = h_sc[...]
    h_in = jnp.zeros((SUBL, w), F32)
    for k in range(SUBL):
        h_in = jnp.where(row == k, c, h_in)
        c = prod[k:k + 1, :] * c + e[k:k + 1, :]
    h_sc[...] = c
    hl_ref[0] = c
    hs, h = [], h_in
    for j in range(seg):
        h = grp(a, j) * h + grp(b, j)
        hs.append(h)
    rec = _gelu_tanh(gb) * jnp.concatenate(hs, axis=0)
    for cc in range(w // LANES):
        rec_ref[cc, 0] = rec[:, cc * LANES:(cc + 1) * LANES]


def _prep_gate_w(gate_w):
    _, nb, bs, _ = gate_w.shape
    eye = jnp.eye(nb, dtype=gate_w.dtype)
    dense = jnp.einsum('gbij,bc->gbicj', gate_w, eye).reshape(2, nb * bs, nb * bs)
    return jnp.concatenate([dense[0], dense[1]], axis=1).astype(BF16)


def _rglru(xb, gb, conv0, h0, conv_w, conv_b, gate_wd, gate_b, lam_p, n, t, seg):
    wc = xb.shape[0]
    w = wc * LANES
    per_n = lambda rows: pl.BlockSpec((1, rows, w), lambda b, i: (b, 0, 0))
    tt = SUBL * seg if seg > 1 else _row_tile(t, 256)
    cols = pl.BlockSpec((wc, 1, tt, LANES), lambda b, i: (0, b, i, 0))
    in_specs = [cols, cols, per_n(B_CONV - 1), per_n(1), _const_spec((B_CONV, w)), _const_spec((1, w)),
                _const_spec((w, 2 * w)), _const_spec((1, 2 * w)), _const_spec((1, w))]
    args = (xb.reshape(wc, n, t, LANES), gb.reshape(wc, n, t, LANES), conv0, h0.reshape(n, 1, w), conv_w,
            conv_b.reshape(1, w), gate_wd, gate_b.reshape(1, 2 * w), lam_p.reshape(1, w))
    state_shapes = [jax.ShapeDtypeStruct((n, B_CONV - 1, w), F32), jax.ShapeDtypeStruct((n, 1, w), F32)]
    if seg == 1:
        rec, cb, hb = pl.pallas_call(
            functools.partial(_rglru_body, tt=tt, w=w),
            grid=(n, t // tt),
            in_specs=in_specs,
            out_specs=[pl.BlockSpec((1, tt, w), lambda b, i: (b, i, 0)), per_n(B_CONV - 1), per_n(1)],
            out_shape=[jax.ShapeDtypeStruct((n, t, w), BF16)] + state_shapes,
            scratch_shapes=[pltpu.VMEM((tt + 8, w), F32), pltpu.VMEM((1, w), F32)],
            compiler_params=_cparams(2),
            name="rglru",
        )(*args)
        return rec.reshape(n * t, w), cb, hb
    assert seg >= B_CONV - 1
    rec, cb, hb = pl.pallas_call(
        functools.partial(_rglru_seg_body, tt=tt, w=w),
        grid=(n, t // tt),
        in_specs=in_specs,
        out_specs=[cols, per_n(B_CONV - 1), per_n(1)],
        out_shape=[jax.ShapeDtypeStruct((wc, n, t, LANES), F32)] + state_shapes,
        scratch_shapes=[pltpu.VMEM(((B_CONV - 1) * SUBL, w), F32), pltpu.VMEM((1, w), F32)],
        compiler_params=_cparams(2),
        name="rglru_seg",
    )(*args)
    return rec.reshape(wc, n * t, LANES), cb, hb


def _inproj_odd_body(x_ref, g_ref, w_ref, u_ref, gin_ref, *, hw, seg):
    h = _rms(x_ref[...], g_ref[...]).astype(BF16)
    proj = _dot(h, w_ref[...])
    _store_col_tiles(u_ref, proj[:, :hw], seg)
    _store_col_tiles(gin_ref, proj[:, hw:2 * hw] * jax.nn.sigmoid(proj[:, 2 * hw:]), seg)


def _inproj_odd(x, g, w_in, seg):
    m, d = x.shape
    hw = w_in.shape[1] // 3
    tm = SUBL * seg if seg > 1 else _row_tile(m, 512)
    row = pl.BlockSpec((tm, d), lambda i: (i, 0))
    cols = pl.BlockSpec((hw // LANES, tm, LANES), lambda i: (0, i, 0))
    return pl.pallas_call(
        functools.partial(_inproj_odd_body, hw=hw, seg=seg),
        grid=(m // tm,),
        in_specs=[row, _const_spec((1, d)), _const_spec(w_in.shape)],
        out_specs=[cols, cols],
        out_shape=[jax.ShapeDtypeStruct((hw // LANES, m, LANES), F32)] * 2,
        compiler_params=_cparams(1),
        name="inproj_odd",
    )(x, g.reshape(1, d), w_in)


def _cmul(ar, ai, br, bi):
    return ar * br - ai * bi, ar * bi + ai * br


def _s5_discretise(a_re, a_im, log_dt):
    dt = jnp.exp(log_dt)
    mag = jnp.exp(dt * a_re)
    ab_re = mag * jnp.cos(dt * a_im)
    ab_im = mag * jnp.sin(dt * a_im)
    den = a_re * a_re + a_im * a_im
    nr = ab_re - 1.0
    return ab_re, ab_im, (nr * a_re + ab_im * a_im) / den, (ab_im * a_re - nr * a_im) / den


def _s5_body(u_ref, re0_ref, im0_ref, are_ref, aim_ref, ldt_ref,
             bre_ref, bim_ref, cre_ref, cim_ref, d_ref, gw_ref, gbias_ref, o_ref, sre_ref, sim_ref,
             st_re, st_im, bf_re, bf_im, lam_re, lam_im, lseg_re, lseg_im, v_re, v_im,
             *, tt, w, sw):
    seg = tt // SUBL

    def step_rows(j):
        return pl.ds(pl.multiple_of(j * SUBL, SUBL), SUBL)

    @pl.when((pl.program_id(0) == 0) & (pl.program_id(1) == 0))
    def _():
        ab_re, ab_im, coef_re, coef_im = _s5_discretise(are_ref[...], aim_ref[...], ldt_ref[...])
        f_re, f_im = _cmul(coef_re, coef_im, bre_ref[...], bim_ref[...])
        bf_re[...] = f_re.astype(BF16)
        bf_im[...] = f_im.astype(BF16)
        lam_re[...] = jnp.broadcast_to(ab_re, lam_re.shape)
        lam_im[...] = jnp.broadcast_to(ab_im, lam_im.shape)
        pr, pi = ab_re, ab_im
        for _ in range(seg.bit_length() - 1):
            pr, pi = _cmul(pr, pi, pr, pi)
        lseg_re[...] = pr
        lseg_im[...] = pi

    @pl.when(pl.program_id(1) == 0)
    def _():
        st_re[...] = re0_ref[0]
        st_im[...] = im0_ref[0]

    u = jnp.concatenate([u_ref[c, 0] for c in range(w // LANES)], axis=-1)
    ub = u.astype(BF16)
    gpt = V7X_MXU_DIM // C_STATE
    kpt = V7X_MXU_DIM // C_GROUP
    for nt in range(sw // V7X_MXU_DIM):
        kt = (nt * gpt) // kpt
        lhs = ub[:, kt * V7X_MXU_DIM:(kt + 1) * V7X_MXU_DIM]
        rows = slice(kt * V7X_MXU_DIM, (kt + 1) * V7X_MXU_DIM)
        cols = slice(nt * V7X_MXU_DIM, (nt + 1) * V7X_MXU_DIM)
        v_re[:, cols] = _dot(lhs, bf_re[rows, cols])
        v_im[:, cols] = _dot(lhs, bf_im[rows, cols])

    for ch in range(sw // S5_SCAN_LANES):
        lanes = slice(ch * S5_SCAN_LANES, (ch + 1) * S5_SCAN_LANES)
        l_re, l_im = lam_re[:, lanes], lam_im[:, lanes]

        def advance(j, sr, si, lanes=lanes, l_re=l_re, l_im=l_im):
            pr, pi = _cmul(l_re, l_im, sr, si)
            return pr + v_re[step_rows(j), lanes], pi + v_im[step_rows(j), lanes]

        zero = jnp.zeros((SUBL, S5_SCAN_LANES), F32)
        e_re, e_im = lax.fori_loop(0, seg, lambda j, c, adv=advance: adv(j, *c), (zero, zero))

        row = lax.broadcasted_iota(jnp.int32, zero.shape, 0)
        c_re, c_im = st_re[:, lanes], st_im[:, lanes]
        in_re, in_im = zero, zero
        for k in range(SUBL):
            in_re = jnp.where(row == k, c_re, in_re)
            in_im = jnp.where(row == k, c_im, in_im)
            p_re, p_im = _cmul(lseg_re[:, lanes], lseg_im[:, lanes], c_re, c_im)
            c_re, c_im = p_re + e_re[k:k + 1, :], p_im + e_im[k:k + 1, :]
        st_re[:, lanes] = c_re
        st_im[:, lanes] = c_im
        sre_ref[0, :, lanes] = c_re
        sim_ref[0, :, lanes] = c_im

        def emit(j, c, lanes=lanes, adv=advance):
            sr, si = adv(j, *c)
            v_re[step_rows(j), lanes] = sr
            v_im[step_rows(j), lanes] = si
            return sr, si

        lax.fori_loop(0, seg, emit, (in_re, in_im))

    ys = []
    for ot in range(w // V7X_MXU_DIM):
        acc = None
        for nt in range(ot * kpt // gpt, (ot + 1) * kpt // gpt):
            rows = slice(nt * V7X_MXU_DIM, (nt + 1) * V7X_MXU_DIM)
            cols = slice(ot * V7X_MXU_DIM, (ot + 1) * V7X_MXU_DIM)
            part = (_dot(v_re[:, rows].astype(BF16), cre_ref[rows, cols])
                    - _dot(v_im[:, rows].astype(BF16), cim_ref[rows, cols]))
            acc = part if acc is None else acc + part
        ys.append(acc)
    y = jnp.concatenate(ys, axis=-1) + d_ref[...] * u
    z = _gelu_tanh(y)
    res = z * jax.nn.sigmoid(_dot(z.astype(BF16), gw_ref[...]) + gbias_ref[...])
    for c in range(w // LANES):
        o_ref[c, 0] = res[:, c * LANES:(c + 1) * LANES]


def _prep_s5(b_re, b_im, c_re, c_im):
    g = b_re.shape[0]
    eye = jnp.eye(g, dtype=b_re.dtype)
    bd_in = lambda b: jnp.einsum('gpc,gh->gchp', b, eye).reshape(g * C_GROUP, g * C_STATE)
    bd_out = lambda c: jnp.einsum('gcp,gh->gphc', c, eye).reshape(g * C_STATE, g * C_GROUP).astype(BF16)
    return bd_in(b_re), bd_in(b_im), bd_out(c_re), bd_out(c_im)


def _s5(u, re0, im0, a_re, a_im, log_dt, mats, d_skip, glu_w, glu_b, n, t):
    w = u.shape[0] * LANES
    groups = w // C_GROUP
    sw = groups * C_STATE
    tt = _row_tile(t, 512)
    bre, bim, cre, cim = mats
    wc = w // LANES
    seq = pl.BlockSpec((wc, 1, tt, LANES), lambda b, i: (0, b, i, 0))
    st = pl.BlockSpec((1, 1, sw), lambda b, i: (b, 0, 0))
    vm = lambda shape, dt: pltpu.VMEM(shape, dt)
    flat = lambda a: a.reshape(1, sw)
    out, s_re, s_im = pl.pallas_call(
        functools.partial(_s5_body, tt=tt, w=w, sw=sw),
        grid=(n, t // tt),
        in_specs=[seq, st, st] + [_const_spec((1, sw))] * 3
                 + [_const_spec((w, sw)), _const_spec((w, sw)), _const_spec((sw, w)), _const_spec((sw, w)),
                    _const_spec((1, w)), _const_spec((w, w)), _const_spec((1, w))],
        out_specs=[seq, st, st],
        out_shape=[jax.ShapeDtypeStruct((wc, n, t, LANES), F32),
                   jax.ShapeDtypeStruct((n, 1, sw), F32), jax.ShapeDtypeStruct((n, 1, sw), F32)],
        scratch_shapes=[vm((1, sw), F32), vm((1, sw), F32), vm((w, sw), BF16), vm((w, sw), BF16),
                        vm((SUBL, sw), F32), vm((SUBL, sw), F32), vm((1, sw), F32), vm((1, sw), F32),
                        vm((tt, sw), F32), vm((tt, sw), F32)],
        compiler_params=_cparams(2),
        name="s5",
    )(u.reshape(wc, n, t, LANES), re0.reshape(n, 1, sw), im0.reshape(n, 1, sw),
      flat(a_re), flat(a_im), flat(jnp.repeat(log_dt, C_STATE)), bre, bim, cre, cim,
      d_skip.reshape(1, w), glu_w.astype(BF16), glu_b.reshape(1, w))
    return out.reshape(wc, n * t, LANES), s_re, s_im


def _conf_body(g_ref, conv0_ref, cw_ref, cb_ref, lng_ref, lnb_ref, o_ref, convn_ref, ext_sc, *, tt, w):
    halo = 32
    hist = D_CONV - 1
    step = pl.program_id(1)

    @pl.when(step == 0)
    def _():
        ext_sc[0:halo, :] = jnp.zeros((halo, w), F32)
        ext_sc[halo - hist:halo, :] = conv0_ref[0]

    ext_sc[halo:halo + tt, :] = jnp.concatenate([g_ref[c, 0] for c in range(w // LANES)], axis=-1)
    cw = cw_ref[...]
    c = cb_ref[...] + jnp.zeros((tt, w), F32)
    for j in range(D_CONV):
        c = c + ext_sc[halo - hist + j:halo - hist + j + tt, :] * cw[j:j + 1, :]
    o_ref[0] = _ln_silu(c, lng_ref[...], lnb_ref[...]).astype(o_ref.dtype)
    convn_ref[0] = ext_sc[halo + tt - hist:halo + tt, :]
    ext_sc[0:halo, :] = ext_sc[tt:tt + halo, :]


def _ln_silu(c, g, b):
    cc = c - jnp.mean(c, axis=-1, keepdims=True)
    y = cc * lax.rsqrt(jnp.mean(cc * cc, axis=-1, keepdims=True) + EPS) * g + b
    return y * jax.nn.sigmoid(y)


def _conf_seg_body(g_ref, conv0_ref, cw_ref, cb_ref, lng_ref, lnb_ref, o_ref, convn_ref,
                   ext_sc, prev_sc, conv_sc, *, tt, w, rows_per_chunk):
    seg = tt // SUBL
    hist = D_CONV - 1
    step = pl.program_id(1)

    @pl.when(step == 0)
    def _():
        c0 = conv0_ref[0]
        for i in range(hist):
            prev_sc[i * SUBL:(i + 1) * SUBL, :] = jnp.broadcast_to(c0[i:i + 1, :], (SUBL, w))

    cur = jnp.concatenate([g_ref[c, 0] for c in range(w // LANES)], axis=-1)
    ext_sc[hist * SUBL:, :] = cur
    row = lax.broadcasted_iota(jnp.int32, (SUBL, w), 0)
    tail = cur[(seg - hist) * SUBL:, :]
    for i in range(hist):
        grp = jnp.where(row == SUBL - 1, prev_sc[i * SUBL:(i + 1) * SUBL, :],
                        tail[i * SUBL:(i + 1) * SUBL, :])
        ext_sc[i * SUBL:(i + 1) * SUBL, :] = pltpu.roll(grp, 1, 0)
    prev_sc[...] = tail
    convn_ref[0] = jnp.concatenate(
        [tail[i * SUBL + SUBL - 1:(i + 1) * SUBL, :] for i in range(hist)], axis=0)

    cw = cw_ref[...]
    gpc = rows_per_chunk // SUBL
    for c in range(w // LANES):
        lanes = slice(c * LANES, (c + 1) * LANES)
        taps = [jnp.broadcast_to(cw[k:k + 1, lanes], (SUBL, LANES)) for k in range(D_CONV)]
        bias = jnp.broadcast_to(cb_ref[:, lanes], (SUBL, LANES))

        def chunk(i, _, lanes=lanes, taps=taps, bias=bias):
            base = pl.multiple_of(i * rows_per_chunk, rows_per_chunk)
            accs = [bias] * gpc
            for g in range(gpc + D_CONV - 1):
                xg = ext_sc[pl.ds(base + g * SUBL, SUBL), lanes]
                for jj in range(gpc):
                    if 0 <= g - jj < D_CONV:
                        accs[jj] = accs[jj] + xg * taps[g - jj]
            conv_sc[pl.ds(base, rows_per_chunk), lanes] = jnp.concatenate(accs, axis=0)
            return 0

        lax.fori_loop(0, tt // rows_per_chunk, chunk, 0)

    y = _ln_silu(conv_sc[...], lng_ref[...], lnb_ref[...])
    for c in range(w // LANES):
        o_ref[c, 0] = y[:, c * LANES:(c + 1) * LANES]


def _conf(g_in, conv0, conv_w, conv_b, ln_g, ln_b, n, t, seg):
    wc = g_in.shape[0]
    w = wc * LANES
    hist = pl.BlockSpec((1, D_CONV - 1, w), lambda b, i: (b, 0, 0))
    params = (conv0, conv_w, conv_b.reshape(1, w), ln_g.reshape(1, w), ln_b.reshape(1, w))
    param_specs = [hist, _const_spec((D_CONV, w)), _const_spec((1, w)), _const_spec((1, w)),
                   _const_spec((1, w))]
    hist_shape = jax.ShapeDtypeStruct((n, D_CONV - 1, w), F32)
    if seg == 1:
        tt = _row_tile(t, 256)
        cols = pl.BlockSpec((wc, 1, tt, LANES), lambda b, i: (0, b, i, 0))
        out, cd = pl.pallas_call(
            functools.partial(_conf_body, tt=tt, w=w),
            grid=(n, t // tt),
            in_specs=[cols] + param_specs,
            out_specs=[pl.BlockSpec((1, tt, w), lambda b, i: (b, i, 0)), hist],
            out_shape=[jax.ShapeDtypeStruct((n, t, w), BF16), hist_shape],
            scratch_shapes=[pltpu.VMEM((tt + 32, w), F32)],
            compiler_params=_cparams(2),
            name="conformer_conv",
        )(g_in.reshape(wc, n, t, LANES), *params)
        return out.reshape(n * t, w), cd
    assert seg >= D_CONV - 1
    tt = SUBL * seg
    cols = pl.BlockSpec((wc, 1, tt, LANES), lambda b, i: (0, b, i, 0))
    out, cd = pl.pallas_call(
        functools.partial(_conf_seg_body, tt=tt, w=w, rows_per_chunk=64),
        grid=(n, t // tt),
        in_specs=[cols] + param_specs,
        out_specs=[cols, hist],
        out_shape=[jax.ShapeDtypeStruct((wc, n, t, LANES), F32), hist_shape],
        scratch_shapes=[pltpu.VMEM(((seg + D_CONV - 1) * SUBL, w), F32),
                        pltpu.VMEM(((D_CONV - 1) * SUBL, w), F32), pltpu.VMEM((tt, w), F32)],
        compiler_params=_cparams(2),
        name="conformer_conv_seg",
    )(g_in.reshape(wc, n, t, LANES), *params)
    return out.reshape(wc, n * t, LANES), cd


def _run_group(x3, cache, conv_b0, h_b0, c_re0, c_im0, conv_d0, p, prepped):
    n, t, d = x3.shape
    depth = p['ffn1_g'].shape[0]
    x = x3.reshape(n * t, d)
    ks, vs, cbs, hbs, cres, cims, cds = [], [], [], [], [], [], []
    for l in range(depth):
        x = _ffn(x, p['ffn1_g'][l], prepped['ffn1'], l)
        if l % 2 == 0:
            e = l // 2
            hw = p['even_w_in'].shape[2] // 5
            heads = hw // A_V
            seg = SEG_BLOCK // SUBL if t % SEG_BLOCK == 0 else 1
            q, k, kb, v, vb, xb, gb = _inproj_even(x, p['mix_g'][l], prepped['even_w_in'][e],
                                                   p['a_q_g'][e], p['a_k_g'][e], seg)
            lam_init = 0.8 - 0.6 * math.exp(-0.3 * l)
            if cache is None:
                att = _attn_prompt(q, kb, vb, p['a_lambda'][e], p['a_head_g'][e], n, t, lam_init)
            else:
                cache_k, cache_v, page_table = cache
                att = _attn_sample(q, k, v, cache_k, cache_v, page_table, e, p['a_lambda'][e],
                                   p['a_head_g'][e], n, t, lam_init)
            rec, cb, hb = _rglru(xb, gb, conv_b0[e], h_b0[e], p['b_conv_w'][e], p['b_conv_b'][e],
                                 prepped['b_gate_w'][e], p['b_gate_b'][e], p['b_lambda'][e], n, t, seg)
            mixer, mixer_seg = (att, rec, prepped['even_w_out'][e]), seg
            ks.append(k.reshape(n, t, heads, 2 * A_QK))
            vs.append(v.reshape(n, t, heads, A_V))
            cbs.append(cb)
            hbs.append(hb.reshape(n, hw))
        else:
            o = l // 2
            seg = SEG_BLOCK // SUBL if t % SEG_BLOCK == 0 else 1
            u, g_in = _inproj_odd(x, p['mix_g'][l], prepped['odd_w_in'][o], seg)
            groups = u.shape[0] * LANES // C_GROUP
            c_out, cre, cim = _s5(u, c_re0[o], c_im0[o], p['c_a_re'][o], p['c_a_im'][o], p['c_log_dt'][o],
                                  prepped['s5'][o], p['c_d'][o], p['c_glu_w'][o], p['c_glu_b'][o], n, t)
            d_out, cd = _conf(g_in, conv_d0[o], p['d_conv_w'][o], p['d_conv_b'][o],
                              p['d_ln_g'][o], p['d_ln_b'][o], n, t, seg)
            mixer, mixer_seg = (c_out, d_out, prepped['odd_w_out'][o]), seg
            cres.append(cre.reshape(n, groups, C_STATE))
            cims.append(cim.reshape(n, groups, C_STATE))
            cds.append(cd)
        x = _ffn(x, p['ffn2_g'][l], prepped['ffn2'], l, mixer, mixer_seg)
    return (x.reshape(n, t, d), jnp.stack(ks), jnp.stack(vs), jnp.stack(cbs), jnp.stack(hbs),
            jnp.stack(cres), jnp.stack(cims), jnp.stack(cds))


def kernel(x_prompt, x_sample, cache_k, cache_v, state_conv_b, state_h_b, state_c_re, state_c_im,
           state_conv_d, page_table, ffn1_g, ffn1_w_gu, ffn1_w_down, mix_g, ffn2_g, ffn2_w_gu,
           ffn2_w_down, even_w_in, even_w_out, a_q_g, a_k_g, a_lambda, a_head_g, b_conv_w, b_conv_b,
           b_gate_w, b_gate_b, b_lambda, odd_w_in, odd_w_out, c_a_re, c_a_im, c_log_dt, c_b_re, c_b_im,
           c_c_re, c_c_im, c_d, c_glu_w, c_glu_b, d_conv_w, d_conv_b, d_ln_g, d_ln_b):
    p = dict(ffn1_g=ffn1_g, mix_g=mix_g, ffn2_g=ffn2_g, even_w_in=even_w_in, a_q_g=a_q_g, a_k_g=a_k_g,
             a_lambda=a_lambda, a_head_g=a_head_g, b_conv_w=b_conv_w, b_conv_b=b_conv_b,
             b_gate_b=b_gate_b, b_lambda=b_lambda, c_a_re=c_a_re, c_a_im=c_a_im, c_log_dt=c_log_dt,
             c_d=c_d, c_glu_w=c_glu_w, c_glu_b=c_glu_b, d_conv_w=d_conv_w, d_conv_b=d_conv_b,
             d_ln_g=d_ln_g, d_ln_b=d_ln_b)
    depth = ffn1_g.shape[0]
    n_even, n_odd = even_w_in.shape[0], odd_w_in.shape[0]
    prepped = dict(
        ffn1=_prep_ffn(ffn1_w_gu, ffn1_w_down),
        ffn2=_prep_ffn(ffn2_w_gu, ffn2_w_down),
        even_w_in=[even_w_in[e].astype(BF16) for e in range(n_even)],
        even_w_out=[even_w_out[e].astype(BF16) for e in range(n_even)],
        odd_w_in=[odd_w_in[o].astype(BF16) for o in range(n_odd)],
        odd_w_out=[odd_w_out[o].astype(BF16) for o in range(n_odd)],
        b_gate_w=[_prep_gate_w(b_gate_w[e]) for e in range(n_even)],
        s5=[_prep_s5(c_b_re[o], c_b_im[o], c_c_re[o], c_c_im[o]) for o in range(n_odd)],
    )
    b = x_prompt.shape[0]
    dt = x_prompt.dtype
    hw = even_w_in.shape[2] // 5
    groups, states = state_c_re.shape[2], state_c_re.shape[3]
    prompt = _run_group(
        x_prompt, None,
        jnp.zeros((n_even, b, B_CONV - 1, hw), dt), jnp.zeros((n_even, b, hw), dt),
        jnp.zeros((n_odd, b, groups, states), dt), jnp.zeros((n_odd, b, groups, states), dt),
        jnp.zeros((n_odd, b, D_CONV - 1, hw), dt), p, prepped)
    sample = _run_group(
        x_sample, (cache_k, cache_v, page_table), state_conv_b, state_h_b, state_c_re, state_c_im,
        state_conv_d, p, prepped)
    return (prompt[0], sample[0]) + prompt[1:] + sample[1:]
```

```python
import functools
import math

import jax
import jax.numpy as jnp
from jax import lax
from jax.experimental import pallas as pl
from jax.experimental.pallas import tpu as pltpu

F32 = jnp.float32
BF16 = jnp.bfloat16
EPS = 1e-6
NEG = -1e30
LOG2E = 1.4426950408889634
ATTN_SCORE_BOUND = 64.0

LANES = 128
SUBL = 8
V7X_MXU_DIM = 256
V7X_VMEM_LIMIT = 56 * 1024 * 1024

A_HEADS = 4
A_QK = 64
A_V = 128
B_BLOCKS = 8
B_CONV = 4
C_RG = 8.0
C_GROUP = 16
C_STATE = 64
D_CONV = 31
SEG_BLOCK = 512
S5_SCAN_LANES = 1024


def _cparams(n_axes):
    return pltpu.CompilerParams(dimension_semantics=("arbitrary",) * n_axes,
                                vmem_limit_bytes=V7X_VMEM_LIMIT)


def _const_spec(shape):
    nd = len(shape)
    return pl.BlockSpec(shape, lambda *_: (0,) * nd)


def _rms(x, g):
    return x * lax.rsqrt(jnp.mean(x * x, axis=-1, keepdims=True) + EPS) * g


def _gelu_tanh(x):
    return 0.5 * x * (1.0 + jnp.tanh(math.sqrt(2.0 / math.pi) * (x + 0.044715 * (x * x * x))))


def _dot(a, b):
    return jnp.dot(a, b, preferred_element_type=F32)


def _row_tile(m, want):
    t = min(m, want)
    while m % t:
        t //= 2
    return t


def _store_col_tiles(ref, val, seg):
    for c in range(val.shape[1] // LANES):
        tile = val[:, c * LANES:(c + 1) * LANES]
        if seg == 1:
            ref[c] = tile
        else:
            for s in range(SUBL):
                ref[c, pl.ds(s, seg, stride=SUBL), :] = tile[s * seg:(s + 1) * seg, :]


def _load_col_tiles(ref, seg):
    cols = []
    for c in range(ref.shape[0]):
        if seg == 1:
            cols.append(ref[c])
        else:
            cols.append(jnp.concatenate(
                [ref[c, pl.ds(s, seg, stride=SUBL), :] for s in range(SUBL)], axis=0))
    return jnp.concatenate(cols, axis=-1)


def _mixer_half(ref, seg):
    if len(ref.shape) == 2:
        return ref[...]
    return _load_col_tiles(ref, seg).astype(BF16)


def _ffn_body(x_ref, g_ref, wg_ref, wu_ref, wd_ref, *rest, ck, with_mixer, seg):
    x = x_ref[...]
    if with_mixer:
        a_ref, b_ref, wo_ref, o_ref = rest
        hw = wo_ref.shape[0] // 2
        x = (x + _dot(_mixer_half(a_ref, seg), wo_ref[:hw, :])
             + _dot(_mixer_half(b_ref, seg), wo_ref[hw:, :]))
    else:
        o_ref, = rest
    h = _rms(x, g_ref[...]).astype(BF16)
    acc = jnp.zeros(x.shape, F32)
    for c in range(wg_ref.shape[1] // ck):
        sl = slice(c * ck, (c + 1) * ck)
        gate = _dot(h, wg_ref[:, sl])
        up = _dot(h, wu_ref[:, sl])
        a = (gate * jax.nn.sigmoid(gate) * up).astype(BF16)
        acc = acc + _dot(a, wd_ref[sl, :])
    o_ref[...] = x + 0.5 * acc


def _split_gu_body(w_ref, g_ref, u_ref, *, f, fp):
    x = w_ref[0]
    lane = lax.broadcasted_iota(jnp.int32, (x.shape[0], fp), 1)
    g_ref[0] = jnp.where(lane < f, x[:, :fp], 0.0).astype(BF16)
    start = f // LANES * LANES
    up = pltpu.roll(x[:, start:start + fp], fp - (f - start), 1)
    u_ref[0] = jnp.where(lane < f, up, 0.0).astype(BF16)


def _pad_rows_body(w_ref, o_ref, *, f, rows):
    r = lax.broadcasted_iota(jnp.int32, w_ref.shape[1:], 0) + pl.program_id(1) * rows
    o_ref[0] = jnp.where(r < f, w_ref[0], 0.0).astype(BF16)


def _prep_ffn(w_gu, w_down):
    layers, d, f2 = w_gu.shape
    f = f2 // 2
    fp = -(-f // V7X_MXU_DIM) * V7X_MXU_DIM
    start = f // LANES * LANES
    if start + fp > f2 or d % V7X_MXU_DIM:
        pad_c = ((0, 0), (0, 0), (0, fp - f))
        return (jnp.pad(w_gu[..., :f], pad_c).astype(BF16), jnp.pad(w_gu[..., f:], pad_c).astype(BF16),
                jnp.pad(w_down, ((0, 0), (0, fp - f), (0, 0))).astype(BF16))
    rows = V7X_MXU_DIM
    half = pl.BlockSpec((1, rows, fp), lambda l, i: (l, i, 0))
    wg, wu = pl.pallas_call(
        functools.partial(_split_gu_body, f=f, fp=fp),
        grid=(layers, d // rows),
        in_specs=[pl.BlockSpec((1, rows, f2), lambda l, i: (l, i, 0))],
        out_specs=[half, half],
        out_shape=[jax.ShapeDtypeStruct((layers, d, fp), BF16)] * 2,
        compiler_params=_cparams(2),
        name="ffn_split_gate_up",
    )(w_gu)
    blk = pl.BlockSpec((1, rows, d), lambda l, i: (l, i, 0))
    wd = pl.pallas_call(
        functools.partial(_pad_rows_body, f=f, rows=rows),
        grid=(layers, fp // rows),
        in_specs=[blk],
        out_specs=blk,
        out_shape=jax.ShapeDtypeStruct((layers, fp, d), BF16),
        compiler_params=_cparams(2),
        name="ffn_pad_down",
    )(w_down)
    return wg, wu, wd


def _ffn(x, g, w, layer, mixer=None, seg=1):
    wg, wu, wd = w
    m, d = x.shape
    fp = wg.shape[2]
    tm = SUBL * seg if seg > 1 else _row_tile(m, 512)
    row = pl.BlockSpec((tm, d), lambda i: (i, 0))
    of_layer = lambda r, c: pl.BlockSpec((None, r, c), lambda i: (layer, 0, 0))
    in_specs = [row, _const_spec((1, d)), of_layer(d, fp), of_layer(d, fp), of_layer(fp, d)]
    args = [x, g.reshape(1, d), wg, wu, wd]
    if mixer is not None:
        a, b, w_out = mixer
        hw = w_out.shape[0] // 2
        spec = lambda v: (pl.BlockSpec((tm, hw), lambda i: (i, 0)) if v.ndim == 2 else
                          pl.BlockSpec((v.shape[0], tm, LANES), lambda i: (0, i, 0)))
        in_specs += [spec(a), spec(b), _const_spec((2 * hw, d))]
        args += [a, b, w_out]
    return pl.pallas_call(
        functools.partial(_ffn_body, ck=V7X_MXU_DIM, with_mixer=mixer is not None, seg=seg),
        grid=(m // tm,),
        in_specs=in_specs,
        out_specs=row,
        out_shape=jax.ShapeDtypeStruct((m, d), F32),
        compiler_params=_cparams(1),
        name="ffn_mix" if mixer is not None else "ffn",
    )(*args)


def _inproj_even_body(x_ref, g_ref, w_ref, gq_ref, gk_ref, ones_ref,
                      q_ref, k_ref, kb_ref, v_ref, vb_ref, xb_ref, gb_ref, *, hw, tm, seg):
    x = x_ref[...]
    h = _rms(x, g_ref[...]).astype(BF16)
    proj = _dot(h, w_ref[...])

    def group_rms(t, gain):
        ss = t * t
        hi = ss.astype(BF16)
        lo = (ss - hi.astype(F32)).astype(BF16)
        parts = []
        for c in range(hw // V7X_MXU_DIM):
            sl = slice(c * V7X_MXU_DIM, (c + 1) * V7X_MXU_DIM)
            parts.append(_dot(hi[:, sl], ones_ref[...]) + _dot(lo[:, sl], ones_ref[...]))
        gs = jnp.concatenate(parts, axis=-1)
        return t * lax.rsqrt(gs * (1.0 / A_QK) + EPS) * gain

    q = group_rms(proj[:, 0:hw], gq_ref[...]) * (A_QK ** -0.5 * LOG2E)
    k = group_rms(proj[:, hw:2 * hw], gk_ref[...])
    v = proj[:, 2 * hw:3 * hw]
    q_ref[...] = q.astype(BF16)
    kb_ref[...] = k.astype(BF16)
    vb_ref[...] = v.astype(BF16)
    heads = hw // A_V
    for hd in range(heads):
        k_ref[pl.ds(hd, tm, stride=heads), :] = k[:, hd * A_V:(hd + 1) * A_V]
        v_ref[pl.ds(hd, tm, stride=heads), :] = v[:, hd * A_V:(hd + 1) * A_V]
    _store_col_tiles(xb_ref, proj[:, 3 * hw:4 * hw], seg)
    _store_col_tiles(gb_ref, proj[:, 4 * hw:5 * hw], seg)


def _inproj_even(x, g, w_in, gq, gk, seg):
    m, d = x.shape
    hw = w_in.shape[1] // 5
    heads = hw // A_V
    tm = SUBL * seg if seg > 1 else _row_tile(m, 512)
    row = pl.BlockSpec((tm, d), lambda i: (i, 0))
    orow = pl.BlockSpec((tm, hw), lambda i: (i, 0))
    hrow = pl.BlockSpec((tm * heads, A_V), lambda i: (i, 0))
    cols = pl.BlockSpec((hw // LANES, tm, LANES), lambda i: (0, i, 0))
    idx = jnp.arange(V7X_MXU_DIM) // A_QK
    ones_bd = (idx[:, None] == idx[None, :]).astype(BF16)
    tile = lambda t: jnp.tile(t, hw // A_QK).reshape(1, hw)
    sds = lambda dt: jax.ShapeDtypeStruct((m, hw), dt)
    hsds = jax.ShapeDtypeStruct((m * heads, A_V), F32)
    csds = jax.ShapeDtypeStruct((hw // LANES, m, LANES), F32)
    return pl.pallas_call(
        functools.partial(_inproj_even_body, hw=hw, tm=tm, seg=seg),
        grid=(m // tm,),
        in_specs=[row, _const_spec((1, d)), _const_spec(w_in.shape), _const_spec((1, hw)),
                  _const_spec((1, hw)), _const_spec(ones_bd.shape)],
        out_specs=[orow, hrow, orow, hrow, orow, cols, cols],
        out_shape=[sds(BF16), hsds, sds(BF16), hsds, sds(BF16), csds, csds],
        compiler_params=_cparams(1),
        name="inproj_even",
    )(x, g.reshape(1, d), w_in, tile(gq), tile(gk), ones_bd)


def _diff_lambda(lv, lam_init):
    s01 = jnp.sum(lv[0:1, :] * lv[1:2, :], axis=-1, keepdims=True)
    s23 = jnp.sum(lv[2:3, :] * lv[3:4, :], axis=-1, keepdims=True)
    return jnp.exp(s01) - jnp.exp(s23) + lam_init


def _split_maps(q):
    lane = lax.broadcasted_iota(jnp.int32, q.shape, 1)
    zero = jnp.zeros_like(q)
    return jnp.where(lane < A_QK, q, zero), jnp.where(lane >= A_QK, q, zero)


def _scores(q, k):
    return lax.dot_general(q, k, (((1,), (1,)), ((), ())), preferred_element_type=F32)


def _attn_prompt_body(lam_ref, q_ref, k_ref, v_ref, hg_ref, o_ref, vext_sc, kn2_sc,
                      *, tq, hps, lam_init):
    qi = pl.program_id(2)

    def max_row_norm2(x):
        xf = x.astype(F32)
        per_head = [jnp.sum(xf[:, hh * A_V:(hh + 1) * A_V] ** 2, axis=-1, keepdims=True)
                    for hh in range(hps)]
        return jnp.max(functools.reduce(jnp.maximum, per_head), axis=0, keepdims=True)

    @pl.when(qi == 0)
    def _():
        for hh in range(hps):
            vext_sc[:, 2 * hh * A_V:(2 * hh + 1) * A_V] = v_ref[:, hh * A_V:(hh + 1) * A_V]
            vext_sc[:, (2 * hh + 1) * A_V:(2 * hh + 2) * A_V] = jnp.ones((v_ref.shape[0], A_V), BF16)
        kn2_sc[...] = max_row_norm2(k_ref[...])

    lam = _diff_lambda(lam_ref[...], lam_init)
    q = q_ref[...]
    qs = [m for hh in range(hps) for m in _split_maps(q[:, hh * A_V:(hh + 1) * A_V])]

    def chain_inputs(kb, c):
        rows = pl.ds(pl.multiple_of(kb * tq, tq), tq)
        hh = c // 2
        s = _scores(qs[c], k_ref[rows, hh * A_V:(hh + 1) * A_V])
        return s, vext_sc[rows, 2 * hh * A_V:(2 * hh + 2) * A_V]

    def causal(shape):
        return lax.broadcasted_iota(jnp.int32, shape, 1) <= lax.broadcasted_iota(jnp.int32, shape, 0)

    def finish(accs):
        for hh in range(hps):
            a0, a1 = accs[2 * hh], accs[2 * hh + 1]
            o = a0[:, :A_V] / a0[:, A_V:] - lam * (a1[:, :A_V] / a1[:, A_V:])
            o_ref[:, hh * A_V:(hh + 1) * A_V] = (
                _rms(o, hg_ref[...]) * (1.0 - lam_init)).astype(o_ref.dtype)

    def bounded():
        def step(kb, accs, diag):
            new = []
            for c in range(2 * hps):
                s, v = chain_inputs(kb, c)
                p = jnp.exp2(s)
                if diag:
                    p = jnp.where(causal(p.shape), p, 0.0)
                new.append(accs[c] + _dot(p.astype(BF16), v))
            return tuple(new)

        zero = jnp.zeros((tq, 2 * A_V), F32)
        accs = lax.fori_loop(0, qi, lambda kb, a: step(kb, a, False), (zero,) * (2 * hps))
        finish(step(qi, accs, True))

    def general():
        def step(kb, carry, diag):
            new = []
            for c in range(2 * hps):
                m, acc = carry[c]
                s, v = chain_inputs(kb, c)
                if diag:
                    s = jnp.where(causal(s.shape), s, NEG)
                m_new = jnp.maximum(m, jnp.max(s, axis=-1, keepdims=True))
                p = jnp.exp2(s - m_new)
                new.append((m_new, jnp.exp2(m - m_new) * acc + _dot(p.astype(BF16), v)))
            return tuple(new)

        init_one = (jnp.full((tq, 1), NEG, F32), jnp.zeros((tq, 2 * A_V), F32))
        carry = lax.fori_loop(0, qi, lambda kb, c: step(kb, c, False), (init_one,) * (2 * hps))
        finish([acc for _, acc in step(qi, carry, True)])

    bound2 = max_row_norm2(q) * kn2_sc[...]
    lax.cond(bound2[0, 0] <= ATTN_SCORE_BOUND ** 2, bounded, general)


def _attn_prompt(q, k, v, lam_p, head_g, n, t, lam_init):
    hw = q.shape[1]
    heads = hw // A_V
    hps = next(c for c in (4, 2, 1) if heads % c == 0)
    tq = _row_tile(t, 512)
    nq = t // tq
    qspec = pl.BlockSpec((tq, hps * A_V), lambda b, h, i: (b * nq + i, h))
    kspec = pl.BlockSpec((t, hps * A_V), lambda b, h, i: (b, h))
    return pl.pallas_call(
        functools.partial(_attn_prompt_body, tq=tq, hps=hps, lam_init=lam_init),
        grid=(n, heads // hps, nq),
        in_specs=[_const_spec(lam_p.shape), qspec, kspec, kspec, _const_spec((1, A_V))],
        out_specs=qspec,
        out_shape=jax.ShapeDtypeStruct((n * t, hw), BF16),
        scratch_shapes=[pltpu.VMEM((t, hps * 2 * A_V), BF16), pltpu.VMEM((1, 1), F32)],
        compiler_params=_cparams(3),
        name="attn_prompt",
    )(lam_p, q, k, v, head_g.reshape(1, A_V))


def _attn_sample_body(pt_ref, lam_ref, q_ref, kn_ref, vn_ref, hg_ref, ck_ref, cv_ref, o_ref,
                      kbuf, vbuf, sem, bias_sc, m_sc, l_sc, acc_sc,
                      *, pps, prow, base, heads, t, lam_init):
    seq, step = pl.program_id(0), pl.program_id(1)
    n_seq, n_step = pl.num_programs(0), pl.num_programs(1)
    flat = seq * n_step + step
    slot = flat % 2

    def page_copies(b, i, s):
        cps = []
        for r in range(pps):
            src = pl.ds(pl.multiple_of((base + pt_ref[b, i * pps + r]) * prow, prow), prow)
            dst = pl.ds(r * prow, prow)
            cps.append(pltpu.make_async_copy(ck_ref.at[src, :], kbuf.at[s, dst, :], sem.at[s, 0]))
            cps.append(pltpu.make_async_copy(cv_ref.at[src, :], vbuf.at[s, dst, :], sem.at[s, 1]))
        return cps

    @pl.when(flat == 0)
    def _():
        for cp in page_copies(seq, step, slot):
            cp.start()

    last = flat == n_seq * n_step - 1
    wrap = step == n_step - 1
    nb = jnp.where(last, seq, jnp.where(wrap, seq + 1, seq))
    ni = jnp.where(last, step, jnp.where(wrap, 0, step + 1))
    for cp in page_copies(nb, ni, 1 - slot):
        cp.start()
    for cp in page_copies(seq, step, slot):
        cp.wait()

    qrows = 2 * t
    q = q_ref[0]
    qall = jnp.concatenate(
        [piece for h in range(heads) for piece in _split_maps(q[:, h * A_V:(h + 1) * A_V])], axis=0)

    def head_match(shape):
        row = lax.broadcasted_iota(jnp.int32, shape, 0)
        col = lax.broadcasted_iota(jnp.int32, shape, 1)
        return row, col, (col % heads) == (row // qrows)

    def update(s, v):
        m = m_sc[...]
        m_new = jnp.maximum(m, jnp.max(s, axis=-1, keepdims=True))
        alpha = jnp.exp2(m - m_new)
        p = jnp.exp2(s - m_new)
        l_sc[...] = alpha * l_sc[...] + jnp.sum(p, axis=-1, keepdims=True)
        acc_sc[...] = alpha * acc_sc[...] + _dot(p.astype(BF16), v)
        m_sc[...] = m_new

    @pl.when((seq == 0) & (step == 0))
    def _():
        _, _, ok = head_match(bias_sc.shape)
        bias_sc[...] = jnp.where(ok, 0.0, NEG)

    @pl.when(step == 0)
    def _():
        m_sc[...] = jnp.full(m_sc.shape, NEG, F32)
        l_sc[...] = jnp.zeros(l_sc.shape, F32)
        acc_sc[...] = jnp.zeros(acc_sc.shape, F32)
        s = _scores(qall, kn_ref[0])
        row, col, ok = head_match(s.shape)
        ok = ok & ((col // heads) <= (row % t))
        update(jnp.where(ok, s, NEG), vn_ref[0])

    update(_scores(qall, kbuf[slot].astype(BF16)) + bias_sc[...], vbuf[slot].astype(BF16))

    @pl.when(last)
    def _():
        for cp in page_copies(seq, step, 1 - slot):
            cp.wait()

    @pl.when(step == n_step - 1)
    def _():
        lam = _diff_lambda(lam_ref[...], lam_init)
        o = acc_sc[...] / l_sc[...]
        for h in range(heads):
            oh = o[h * qrows:h * qrows + t, :] - lam * o[h * qrows + t:(h + 1) * qrows, :]
            o_ref[0, :, h * A_V:(h + 1) * A_V] = (
                _rms(oh, hg_ref[...]) * (1.0 - lam_init)).astype(o_ref.dtype)


def _attn_sample(q, k_new, v_new, cache_k, cache_v, page_table, layer, lam_p, head_g, n, t, lam_init):
    hw = q.shape[1]
    heads = hw // A_V
    n_layers, n_pool, page = cache_k.shape[:3]
    n_pages = page_table.shape[1]
    prow = page * heads
    pps = 16
    while n_pages % pps:
        pps //= 2
    ck = cache_k.reshape(n_layers * n_pool * prow, A_V)
    cv = cache_v.reshape(n_layers * n_pool * prow, A_V)
    pad = ((0, 0), (0, prow - t * heads), (0, 0))
    kn = jnp.pad(k_new.astype(BF16).reshape(n, t * heads, A_V), pad)
    vn = jnp.pad(v_new.astype(BF16).reshape(n, t * heads, A_V), pad)
    qr = heads * 2 * t
    grid_spec = pltpu.PrefetchScalarGridSpec(
        num_scalar_prefetch=1,
        grid=(n, n_pages // pps),
        in_specs=[pl.BlockSpec(lam_p.shape, lambda b, i, pt: (0, 0)),
                  pl.BlockSpec((1, t, hw), lambda b, i, pt: (b, 0, 0)),
                  pl.BlockSpec((1, prow, A_V), lambda b, i, pt: (b, 0, 0)),
                  pl.BlockSpec((1, prow, A_V), lambda b, i, pt: (b, 0, 0)),
                  pl.BlockSpec((1, A_V), lambda b, i, pt: (0, 0)),
                  pl.BlockSpec(memory_space=pl.ANY), pl.BlockSpec(memory_space=pl.ANY)],
        out_specs=pl.BlockSpec((1, t, hw), lambda b, i, pt: (b, 0, 0)),
        scratch_shapes=[pltpu.VMEM((2, pps * prow, A_V), F32), pltpu.VMEM((2, pps * prow, A_V), F32),
                        pltpu.SemaphoreType.DMA((2, 2)),
                        pltpu.VMEM((qr, pps * prow), F32),
                        pltpu.VMEM((qr, 1), F32), pltpu.VMEM((qr, 1), F32), pltpu.VMEM((qr, A_V), F32)],
    )
    out = pl.pallas_call(
        functools.partial(_attn_sample_body, pps=pps, prow=prow, base=layer * n_pool, heads=heads, t=t,
                          lam_init=lam_init),
        grid_spec=grid_spec,
        out_shape=jax.ShapeDtypeStruct((n, t, hw), BF16),
        compiler_params=_cparams(2),
        name="attn_sample",
    )(page_table, lam_p, q.reshape(n, t, hw), kn, vn, head_g.reshape(1, A_V), ck, cv)
    return out.reshape(n * t, hw)


def _shift_rows(x, d, fill):
    row = lax.broadcasted_iota(jnp.int32, x.shape, 0)
    return jnp.where(row >= d, pltpu.roll(x, d, 0), fill)


def _linear_scan(a, b):
    d = 1
    while d < a.shape[0]:
        b = b + a * _shift_rows(b, d, 0.0)
        a = a * _shift_rows(a, d, 1.0)
        d *= 2
    return a, b


def _rglru_body(xb_ref, gb_ref, conv0_ref, h0_ref, cw_ref, cb_ref, wg_ref, gbias_ref, lam_ref,
                rec_ref, convn_ref, hl_ref, ext_sc, h_sc, *, tt, w):
    halo = 8
    step = pl.program_id(1)

    @pl.when(step == 0)
    def _():
        ext_sc[0:halo, :] = jnp.zeros((halo, w), F32)
        ext_sc[halo - (B_CONV - 1):halo, :] = conv0_ref[0]
        h_sc[...] = h0_ref[0]

    x = jnp.concatenate([xb_ref[c, 0] for c in range(w // LANES)], axis=-1)
    gb = jnp.concatenate([gb_ref[c, 0] for c in range(w // LANES)], axis=-1)
    ext_sc[halo:halo + tt, :] = x
    cw = cw_ref[...]
    xc = cb_ref[...] + x * cw[B_CONV - 1:B_CONV, :]
    for j in range(1, B_CONV):
        xc = xc + ext_sc[halo - j:halo - j + tt, :] * cw[B_CONV - 1 - j:B_CONV - j, :]

    a, b = _rglru_coeffs(xc, wg_ref[...], gbias_ref[...], lam_ref[...], w)
    a_cum, h_loc = _linear_scan(a, b)
    h = h_loc + a_cum * h_sc[...]
    rec_ref[0] = (_gelu_tanh(gb) * h).astype(rec_ref.dtype)

    h_sc[...] = h[tt - 1:tt, :]
    hl_ref[0] = h[tt - 1:tt, :]
    convn_ref[0] = ext_sc[halo + tt - (B_CONV - 1):halo + tt, :]
    ext_sc[0:halo, :] = ext_sc[tt:tt + halo, :]


def _rglru_coeffs(xc, gate_w, gate_b, lam, w):
    pre = _dot(xc.astype(BF16), gate_w) + gate_b
    r = jax.nn.sigmoid(pre[:, :w])
    i = jax.nn.sigmoid(pre[:, w:])
    softplus_neg = jnp.maximum(-lam, 0.0) + jnp.log1p(jnp.exp(-jnp.abs(lam)))
    a = jnp.exp((-C_RG) * r * softplus_neg)
    return a, jnp.sqrt(1.0 - a * a) * (i * xc)


def _rglru_seg_body(xb_ref, gb_ref, conv0_ref, h0_ref, cw_ref, cb_ref, wg_ref, gbias_ref, lam_ref,
                    rec_ref, convn_ref, hl_ref, prev_sc, h_sc, *, tt, w):
    seg = tt // SUBL
    hist = B_CONV - 1
    step = pl.program_id(1)

    @pl.when(step == 0)
    def _():
        c0 = conv0_ref[0]
        for i in range(hist):
            prev_sc[i * SUBL:(i + 1) * SUBL, :] = jnp.broadcast_to(c0[i:i + 1, :], (SUBL, w))
        h_sc[...] = h0_ref[0]

    cur = jnp.concatenate([xb_ref[c, 0] for c in range(w // LANES)], axis=-1)
    gb = jnp.concatenate([gb_ref[c, 0] for c in range(w // LANES)], axis=-1)
    row = lax.broadcasted_iota(jnp.int32, (SUBL, w), 0)
    tail = cur[(seg - hist) * SUBL:, :]
    halo = [pltpu.roll(jnp.where(row == SUBL - 1, prev_sc[i * SUBL:(i + 1) * SUBL, :],
                                 tail[i * SUBL:(i + 1) * SUBL, :]), 1, 0) for i in range(hist)]
    ext = jnp.concatenate(halo + [cur], axis=0)
    prev_sc[...] = tail
    convn_ref[0] = jnp.concatenate(
        [tail[i * SUBL + SUBL - 1:(i + 1) * SUBL, :] for i in range(hist)], axis=0)

    cw = cw_ref[...]
    xc = cb_ref[...] + jnp.zeros((tt, w), F32)
    for k in range(B_CONV):
        xc = xc + ext[k * SUBL:k * SUBL + tt, :] * cw[k:k + 1, :]
    a, b = _rglru_coeffs(xc, wg_ref[...], gbias_ref[...], lam_ref[...], w)

    grp = lambda v, j: v[j * SUBL:(j + 1) * SUBL, :]
    e, prod = grp(b, 0), grp(a, 0)
    for j in range(1, seg):
        e, prod = grp(a, j) * e + grp(b, j), grp(a, j) * prod
    c = h_sc[...]
    h_in = jnp.zeros((SUBL, w), F32)
    for k in range(SUBL):
        h_in = jnp.where(row == k, c, h_in)
        c = prod[k:k + 1, :] * c + e[k:k + 1, :]
    h_sc[...] = c
    hl_ref[0] = c
    hs, h = [], h_in
    for j in range(seg):
        h = grp(a, j) * h + grp(b, j)
        hs.append(h)
    rec = _gelu_tanh(gb) * jnp.concatenate(hs, axis=0)
    for cc in range(w // LANES):
        rec_ref[cc, 0] = rec[:, cc * LANES:(cc + 1) * LANES]


def _prep_gate_w(gate_w):
    _, nb, bs, _ = gate_w.shape
    eye = jnp.eye(nb, dtype=gate_w.dtype)
    dense = jnp.einsum('gbij,bc->gbicj', gate_w, eye).reshape(2, nb * bs, nb * bs)
    return jnp.concatenate([dense[0], dense[1]], axis=1).astype(BF16)


def _rglru(xb, gb, conv0, h0, conv_w, conv_b, gate_wd, gate_b, lam_p, n, t, seg):
    wc = xb.shape[0]
    w = wc * LANES
    per_n = lambda rows: pl.BlockSpec((1, rows, w), lambda b, i: (b, 0, 0))
    tt = SUBL * seg if seg > 1 else _row_tile(t, 256)
    cols = pl.BlockSpec((wc, 1, tt, LANES), lambda b, i: (0, b, i, 0))
    in_specs = [cols, cols, per_n(B_CONV - 1), per_n(1), _const_spec((B_CONV, w)), _const_spec((1, w)),
                _const_spec((w, 2 * w)), _const_spec((1, 2 * w)), _const_spec((1, w))]
    args = (xb.reshape(wc, n, t, LANES), gb.reshape(wc, n, t, LANES), conv0, h0.reshape(n, 1, w), conv_w,
            conv_b.reshape(1, w), gate_wd, gate_b.reshape(1, 2 * w), lam_p.reshape(1, w))
    state_shapes = [jax.ShapeDtypeStruct((n, B_CONV - 1, w), F32), jax.ShapeDtypeStruct((n, 1, w), F32)]
    if seg == 1:
        rec, cb, hb = pl.pallas_call(
            functools.partial(_rglru_body, tt=tt, w=w),
            grid=(n, t // tt),
            in_specs=in_specs,
            out_specs=[pl.BlockSpec((1, tt, w), lambda b, i: (b, i, 0)), per_n(B_CONV - 1), per_n(1)],
            out_shape=[jax.ShapeDtypeStruct((n, t, w), BF16)] + state_shapes,
            scratch_shapes=[pltpu.VMEM((tt + 8, w), F32), pltpu.VMEM((1, w), F32)],
            compiler_params=_cparams(2),
            name="rglru",
        )(*args)
        return rec.reshape(n * t, w), cb, hb
    assert seg >= B_CONV - 1
    rec, cb, hb = pl.pallas_call(
        functools.partial(_rglru_seg_body, tt=tt, w=w),
        grid=(n, t // tt),
        in_specs=in_specs,
        out_specs=[cols, per_n(B_CONV - 1), per_n(1)],
        out_shape=[jax.ShapeDtypeStruct((wc, n, t, LANES), F32)] + state_shapes,
        scratch_shapes=[pltpu.VMEM(((B_CONV - 1) * SUBL, w), F32), pltpu.VMEM((1, w), F32)],
        compiler_params=_cparams(2),
        name="rglru_seg",
    )(*args)
    return rec.reshape(wc, n * t, LANES), cb, hb


def _inproj_odd_body(x_ref, g_ref, w_ref, u_ref, gin_ref, *, hw, seg):
    h = _rms(x_ref[...], g_ref[...]).astype(BF16)
    proj = _dot(h, w_ref[...])
    _store_col_tiles(u_ref, proj[:, :hw], seg)
    _store_col_tiles(gin_ref, proj[:, hw:2 * hw] * jax.nn.sigmoid(proj[:, 2 * hw:]), seg)


def _inproj_odd(x, g, w_in, seg):
    m, d = x.shape
    hw = w_in.shape[1] // 3
    tm = SUBL * seg if seg > 1 else _row_tile(m, 512)
    row = pl.BlockSpec((tm, d), lambda i: (i, 0))
    cols = pl.BlockSpec((hw // LANES, tm, LANES), lambda i: (0, i, 0))
    return pl.pallas_call(
        functools.partial(_inproj_odd_body, hw=hw, seg=seg),
        grid=(m // tm,),
        in_specs=[row, _const_spec((1, d)), _const_spec(w_in.shape)],
        out_specs=[cols, cols],
        out_shape=[jax.ShapeDtypeStruct((hw // LANES, m, LANES), F32)] * 2,
        compiler_params=_cparams(1),
        name="inproj_odd",
    )(x, g.reshape(1, d), w_in)


def _cmul(ar, ai, br, bi):
    return ar * br - ai * bi, ar * bi + ai * br


def _s5_discretise(a_re, a_im, log_dt):
    dt = jnp.exp(log_dt)
    mag = jnp.exp(dt * a_re)
    ab_re = mag * jnp.cos(dt * a_im)
    ab_im = mag * jnp.sin(dt * a_im)
    den = a_re * a_re + a_im * a_im
    nr = ab_re - 1.0
    return ab_re, ab_im, (nr * a_re + ab_im * a_im) / den, (ab_im * a_re - nr * a_im) / den


def _s5_body(u_ref, re0_ref, im0_ref, are_ref, aim_ref, ldt_ref,
             bre_ref, bim_ref, cre_ref, cim_ref, d_ref, gw_ref, gbias_ref, o_ref, sre_ref, sim_ref,
             st_re, st_im, bf_re, bf_im, lam_re, lam_im, lseg_re, lseg_im, v_re, v_im,
             *, tt, w, sw):
    seg = tt // SUBL

    def step_rows(j):
        return pl.ds(pl.multiple_of(j * SUBL, SUBL), SUBL)

    @pl.when((pl.program_id(0) == 0) & (pl.program_id(1) == 0))
    def _():
        ab_re, ab_im, coef_re, coef_im = _s5_discretise(are_ref[...], aim_ref[...], ldt_ref[...])
        f_re, f_im = _cmul(coef_re, coef_im, bre_ref[...], bim_ref[...])
        bf_re[...] = f_re.astype(BF16)
        bf_im[...] = f_im.astype(BF16)
        lam_re[...] = jnp.broadcast_to(ab_re, lam_re.shape)
        lam_im[...] = jnp.broadcast_to(ab_im, lam_im.shape)
        pr, pi = ab_re, ab_im
        for _ in range(seg.bit_length() - 1):
            pr, pi = _cmul(pr, pi, pr, pi)
        lseg_re[...] = pr
        lseg_im[...] = pi

    @pl.when(pl.program_id(1) == 0)
    def _():
        st_re[...] = re0_ref[0]
        st_im[...] = im0_ref[0]

    u = jnp.concatenate([u_ref[c, 0] for c in range(w // LANES)], axis=-1)
    ub = u.astype(BF16)
    gpt = V7X_MXU_DIM // C_STATE
    kpt = V7X_MXU_DIM // C_GROUP
    for nt in range(sw // V7X_MXU_DIM):
        kt = (nt * gpt) // kpt
        lhs = ub[:, kt * V7X_MXU_DIM:(kt + 1) * V7X_MXU_DIM]
        rows = slice(kt * V7X_MXU_DIM, (kt + 1) * V7X_MXU_DIM)
        cols = slice(nt * V7X_MXU_DIM, (nt + 1) * V7X_MXU_DIM)
        v_re[:, cols] = _dot(lhs, bf_re[rows, cols])
        v_im[:, cols] = _dot(lhs, bf_im[rows, cols])

    for ch in range(sw // S5_SCAN_LANES):
        lanes = slice(ch * S5_SCAN_LANES, (ch + 1) * S5_SCAN_LANES)
        l_re, l_im = lam_re[:, lanes], lam_im[:, lanes]

        def advance(j, sr, si, lanes=lanes, l_re=l_re, l_im=l_im):
            pr, pi = _cmul(l_re, l_im, sr, si)
            return pr + v_re[step_rows(j), lanes], pi + v_im[step_rows(j), lanes]

        zero = jnp.zeros((SUBL, S5_SCAN_LANES), F32)
        e_re, e_im = lax.fori_loop(0, seg, lambda j, c, adv=advance: adv(j, *c), (zero, zero))

        row = lax.broadcasted_iota(jnp.int32, zero.shape, 0)
        c_re, c_im = st_re[:, lanes], st_im[:, lanes]
        in_re, in_im = zero, zero
        for k in range(SUBL):
            in_re = jnp.where(row == k, c_re, in_re)
            in_im = jnp.where(row == k, c_im, in_im)
            p_re, p_im = _cmul(lseg_re[:, lanes], lseg_im[:, lanes], c_re, c_im)
            c_re, c_im = p_re + e_re[k:k + 1, :], p_im + e_im[k:k + 1, :]
        st_re[:, lanes] = c_re
        st_im[:, lanes] = c_im
        sre_ref[0, :, lanes] = c_re
        sim_ref[0, :, lanes] = c_im

        def emit(j, c, lanes=lanes, adv=advance):
            sr, si = adv(j, *c)
            v_re[step_rows(j), lanes] = sr
            v_im[step_rows(j), lanes] = si
            return sr, si

        lax.fori_loop(0, seg, emit, (in_re, in_im))

    ys = []
    for ot in range(w // V7X_MXU_DIM):
        acc = None
        for nt in range(ot * kpt // gpt, (ot + 1) * kpt // gpt):
            rows = slice(nt * V7X_MXU_DIM, (nt + 1) * V7X_MXU_DIM)
            cols = slice(ot * V7X_MXU_DIM, (ot + 1) * V7X_MXU_DIM)
            part = (_dot(v_re[:, rows].astype(BF16), cre_ref[rows, cols])
                    - _dot(v_im[:, rows].astype(BF16), cim_ref[rows, cols]))
            acc = part if acc is None else acc + part
        ys.append(acc)
    y = jnp.concatenate(ys, axis=-1) + d_ref[...] * u
    z = _gelu_tanh(y)
    res = z * jax.nn.sigmoid(_dot(z.astype(BF16), gw_ref[...]) + gbias_ref[...])
    for c in range(w // LANES):
        o_ref[c, 0] = res[:, c * LANES:(c + 1) * LANES]


def _prep_s5(b_re, b_im, c_re, c_im):
    g = b_re.shape[0]
    eye = jnp.eye(g, dtype=b_re.dtype)
    bd_in = lambda b: jnp.einsum('gpc,gh->gchp', b, eye).reshape(g * C_GROUP, g * C_STATE)
    bd_out = lambda c: jnp.einsum('gcp,gh->gphc', c, eye).reshape(g * C_STATE, g * C_GROUP).astype(BF16)
    return bd_in(b_re), bd_in(b_im), bd_out(c_re), bd_out(c_im)


def _s5(u, re0, im0, a_re, a_im, log_dt, mats, d_skip, glu_w, glu_b, n, t):
    w = u.shape[0] * LANES
    groups = w // C_GROUP
    sw = groups * C_STATE
    tt = _row_tile(t, 512)
    bre, bim, cre, cim = mats
    wc = w // LANES
    seq = pl.BlockSpec((wc, 1, tt, LANES), lambda b, i: (0, b, i, 0))
    st = pl.BlockSpec((1, 1, sw), lambda b, i: (b, 0, 0))
    vm = lambda shape, dt: pltpu.VMEM(shape, dt)
    flat = lambda a: a.reshape(1, sw)
    out, s_re, s_im = pl.pallas_call(
        functools.partial(_s5_body, tt=tt, w=w, sw=sw),
        grid=(n, t // tt),
        in_specs=[seq, st, st] + [_const_spec((1, sw))] * 3
                 + [_const_spec((w, sw)), _const_spec((w, sw)), _const_spec((sw, w)), _const_spec((sw, w)),
                    _const_spec((1, w)), _const_spec((w, w)), _const_spec((1, w))],
        out_specs=[seq, st, st],
        out_shape=[jax.ShapeDtypeStruct((wc, n, t, LANES), F32),
                   jax.ShapeDtypeStruct((n, 1, sw), F32), jax.ShapeDtypeStruct((n, 1, sw), F32)],
        scratch_shapes=[vm((1, sw), F32), vm((1, sw), F32), vm((w, sw), BF16), vm((w, sw), BF16),
                        vm((SUBL, sw), F32), vm((SUBL, sw), F32), vm((1, sw), F32), vm((1, sw), F32),
                        vm((tt, sw), F32), vm((tt, sw), F32)],
        compiler_params=_cparams(2),
        name="s5",
    )(u.reshape(wc, n, t, LANES), re0.reshape(n, 1, sw), im0.reshape(n, 1, sw),
      flat(a_re), flat(a_im), flat(jnp.repeat(log_dt, C_STATE)), bre, bim, cre, cim,
      d_skip.reshape(1, w), glu_w.astype(BF16), glu_b.reshape(1, w))
    return out.reshape(wc, n * t, LANES), s_re, s_im


def _conf_body(g_ref, conv0_ref, cw_ref, cb_ref, lng_ref, lnb_ref, o_ref, convn_ref, ext_sc, *, tt, w):
    halo = 32
    hist = D_CONV - 1
    step = pl.program_id(1)

    @pl.when(step == 0)
    def _():
        ext_sc[0:halo, :] = jnp.zeros((halo, w), F32)
        ext_sc[halo - hist:halo, :] = conv0_ref[0]

    ext_sc[halo:halo + tt, :] = jnp.concatenate([g_ref[c, 0] for c in range(w // LANES)], axis=-1)
    cw = cw_ref[...]
    c = cb_ref[...] + jnp.zeros((tt, w), F32)
    for j in range(D_CONV):
        c = c + ext_sc[halo - hist + j:halo - hist + j + tt, :] * cw[j:j + 1, :]
    o_ref[0] = _ln_silu(c, lng_ref[...], lnb_ref[...]).astype(o_ref.dtype)
    convn_ref[0] = ext_sc[halo + tt - hist:halo + tt, :]
    ext_sc[0:halo, :] = ext_sc[tt:tt + halo, :]


def _ln_silu(c, g, b):
    cc = c - jnp.mean(c, axis=-1, keepdims=True)
    y = cc * lax.rsqrt(jnp.mean(cc * cc, axis=-1, keepdims=True) + EPS) * g + b
    return y * jax.nn.sigmoid(y)


def _conf_seg_body(g_ref, conv0_ref, cw_ref, cb_ref, lng_ref, lnb_ref, o_ref, convn_ref,
                   ext_sc, prev_sc, conv_sc, *, tt, w, rows_per_chunk):
    seg = tt // SUBL
    hist = D_CONV - 1
    step = pl.program_id(1)

    @pl.when(step == 0)
    def _():
        c0 = conv0_ref[0]
        for i in range(hist):
            prev_sc[i * SUBL:(i + 1) * SUBL, :] = jnp.broadcast_to(c0[i:i + 1, :], (SUBL, w))

    cur = jnp.concatenate([g_ref[c, 0] for c in range(w // LANES)], axis=-1)
    ext_sc[hist * SUBL:, :] = cur
    row = lax.broadcasted_iota(jnp.int32, (SUBL, w), 0)
    tail = cur[(seg - hist) * SUBL:, :]
    for i in range(hist):
        grp = jnp.where(row == SUBL - 1, prev_sc[i * SUBL:(i + 1) * SUBL, :],
                        tail[i * SUBL:(i + 1) * SUBL, :])
        ext_sc[i * SUBL:(i + 1) * SUBL, :] = pltpu.roll(grp, 1, 0)
    prev_sc[...] = tail
    convn_ref[0] = jnp.concatenate(
        [tail[i * SUBL + SUBL - 1:(i + 1) * SUBL, :] for i in range(hist)], axis=0)

    cw = cw_ref[...]
    gpc = rows_per_chunk // SUBL
    for c in range(w // LANES):
        lanes = slice(c * LANES, (c + 1) * LANES)
        taps = [jnp.broadcast_to(cw[k:k + 1, lanes], (SUBL, LANES)) for k in range(D_CONV)]
        bias = jnp.broadcast_to(cb_ref[:, lanes], (SUBL, LANES))

        def chunk(i, _, lanes=lanes, taps=taps, bias=bias):
            base = pl.multiple_of(i * rows_per_chunk, rows_per_chunk)
            accs = [bias] * gpc
            for g in range(gpc + D_CONV - 1):
                xg = ext_sc[pl.ds(base + g * SUBL, SUBL), lanes]
                for jj in range(gpc):
                    if 0 <= g - jj < D_CONV:
                        accs[jj] = accs[jj] + xg * taps[g - jj]
            conv_sc[pl.ds(base, rows_per_chunk), lanes] = jnp.concatenate(accs, axis=0)
            return 0

        lax.fori_loop(0, tt // rows_per_chunk, chunk, 0)

    y = _ln_silu(conv_sc[...], lng_ref[...], lnb_ref[...])
    for c in range(w // LANES):
        o_ref[c, 0] = y[:, c * LANES:(c + 1) * LANES]


def _conf(g_in, conv0, conv_w, conv_b, ln_g, ln_b, n, t, seg):
    wc = g_in.shape[0]
    w = wc * LANES
    hist = pl.BlockSpec((1, D_CONV - 1, w), lambda b, i: (b, 0, 0))
    params = (conv0, conv_w, conv_b.reshape(1, w), ln_g.reshape(1, w), ln_b.reshape(1, w))
    param_specs = [hist, _const_spec((D_CONV, w)), _const_spec((1, w)), _const_spec((1, w)),
                   _const_spec((1, w))]
    hist_shape = jax.ShapeDtypeStruct((n, D_CONV - 1, w), F32)
    if seg == 1:
        tt = _row_tile(t, 256)
        cols = pl.BlockSpec((wc, 1, tt, LANES), lambda b, i: (0, b, i, 0))
        out, cd = pl.pallas_call(
            functools.partial(_conf_body, tt=tt, w=w),
            grid=(n, t // tt),
            in_specs=[cols] + param_specs,
            out_specs=[pl.BlockSpec((1, tt, w), lambda b, i: (b, i, 0)), hist],
            out_shape=[jax.ShapeDtypeStruct((n, t, w), BF16), hist_shape],
            scratch_shapes=[pltpu.VMEM((tt + 32, w), F32)],
            compiler_params=_cparams(2),
            name="conformer_conv",
        )(g_in.reshape(wc, n, t, LANES), *params)
        return out.reshape(n * t, w), cd
    assert seg >= D_CONV - 1
    tt = SUBL * seg
    cols = pl.BlockSpec((wc, 1, tt, LANES), lambda b, i: (0, b, i, 0))
    out, cd = pl.pallas_call(
        functools.partial(_conf_seg_body, tt=tt, w=w, rows_per_chunk=64),
        grid=(n, t // tt),
        in_specs=[cols] + param_specs,
        out_specs=[cols, hist],
        out_shape=[jax.ShapeDtypeStruct((wc, n, t, LANES), F32), hist_shape],
        scratch_shapes=[pltpu.VMEM(((seg + D_CONV - 1) * SUBL, w), F32),
                        pltpu.VMEM(((D_CONV - 1) * SUBL, w), F32), pltpu.VMEM((tt, w), F32)],
        compiler_params=_cparams(2),
        name="conformer_conv_seg",
    )(g_in.reshape(wc, n, t, LANES), *params)
    return out.reshape(wc, n * t, LANES), cd


def _run_group(x3, cache, conv_b0, h_b0, c_re0, c_im0, conv_d0, p, prepped):
    n, t, d = x3.shape
    depth = p['ffn1_g'].shape[0]
    x = x3.reshape(n * t, d)
    ks, vs, cbs, hbs, cres, cims, cds = [], [], [], [], [], [], []
    for l in range(depth):
        x = _ffn(x, p['ffn1_g'][l], prepped['ffn1'], l)
        if l % 2 == 0:
            e = l // 2
            hw = p['even_w_in'].shape[2] // 5
            heads = hw // A_V
            seg = SEG_BLOCK // SUBL if t % SEG_BLOCK == 0 else 1
            q, k, kb, v, vb, xb, gb = _inproj_even(x, p['mix_g'][l], prepped['even_w_in'][e],
                                                   p['a_q_g'][e], p['a_k_g'][e], seg)
            lam_init = 0.8 - 0.6 * math.exp(-0.3 * l)
            if cache is None:
                att = _attn_prompt(q, kb, vb, p['a_lambda'][e], p['a_head_g'][e], n, t, lam_init)
            else:
                cache_k, cache_v, page_table = cache
                att = _attn_sample(q, k, v, cache_k, cache_v, page_table, e, p['a_lambda'][e],
                                   p['a_head_g'][e], n, t, lam_init)
            rec, cb, hb = _rglru(xb, gb, conv_b0[e], h_b0[e], p['b_conv_w'][e], p['b_conv_b'][e],
                                 prepped['b_gate_w'][e], p['b_gate_b'][e], p['b_lambda'][e], n, t, seg)
            mixer, mixer_seg = (att, rec, prepped['even_w_out'][e]), seg
            ks.append(k.reshape(n, t, heads, 2 * A_QK))
            vs.append(v.reshape(n, t, heads, A_V))
            cbs.append(cb)
            hbs.append(hb.reshape(n, hw))
        else:
            o = l // 2
            seg = SEG_BLOCK // SUBL if t % SEG_BLOCK == 0 else 1
            u, g_in = _inproj_odd(x, p['mix_g'][l], prepped['odd_w_in'][o], seg)
            groups = u.shape[0] * LANES // C_GROUP
            c_out, cre, cim = _s5(u, c_re0[o], c_im0[o], p['c_a_re'][o], p['c_a_im'][o], p['c_log_dt'][o],
                                  prepped['s5'][o], p['c_d'][o], p['c_glu_w'][o], p['c_glu_b'][o], n, t)
            d_out, cd = _conf(g_in, conv_d0[o], p['d_conv_w'][o], p['d_conv_b'][o],
                              p['d_ln_g'][o], p['d_ln_b'][o], n, t, seg)
            mixer, mixer_seg = (c_out, d_out, prepped['odd_w_out'][o]), seg
            cres.append(cre.reshape(n, groups, C_STATE))
            cims.append(cim.reshape(n, groups, C_STATE))
            cds.append(cd)
        x = _ffn(x, p['ffn2_g'][l], prepped['ffn2'], l, mixer, mixer_seg)
    return (x.reshape(n, t, d), jnp.stack(ks), jnp.stack(vs), jnp.stack(cbs), jnp.stack(hbs),
            jnp.stack(cres), jnp.stack(cims), jnp.stack(cds))


def kernel(x_prompt, x_sample, cache_k, cache_v, state_conv_b, state_h_b, state_c_re, state_c_im,
           state_conv_d, page_table, ffn1_g, ffn1_w_gu, ffn1_w_down, mix_g, ffn2_g, ffn2_w_gu,
           ffn2_w_down, even_w_in, even_w_out, a_q_g, a_k_g, a_lambda, a_head_g, b_conv_w, b_conv_b,
           b_gate_w, b_gate_b, b_lambda, odd_w_in, odd_w_out, c_a_re, c_a_im, c_log_dt, c_b_re, c_b_im,
           c_c_re, c_c_im, c_d, c_glu_w, c_glu_b, d_conv_w, d_conv_b, d_ln_g, d_ln_b):
    p = dict(ffn1_g=ffn1_g, mix_g=mix_g, ffn2_g=ffn2_g, even_w_in=even_w_in, a_q_g=a_q_g, a_k_g=a_k_g,
             a_lambda=a_lambda, a_head_g=a_head_g, b_conv_w=b_conv_w, b_conv_b=b_conv_b,
             b_gate_b=b_gate_b, b_lambda=b_lambda, c_a_re=c_a_re, c_a_im=c_a_im, c_log_dt=c_log_dt,
             c_d=c_d, c_glu_w=c_glu_w, c_glu_b=c_glu_b, d_conv_w=d_conv_w, d_conv_b=d_conv_b,
             d_ln_g=d_ln_g, d_ln_b=d_ln_b)
    depth = ffn1_g.shape[0]
    n_even, n_odd = even_w_in.shape[0], odd_w_in.shape[0]
    prepped = dict(
        ffn1=_prep_ffn(ffn1_w_gu, ffn1_w_down),
        ffn2=_prep_ffn(ffn2_w_gu, ffn2_w_down),
        even_w_in=[even_w_in[e].astype(BF16) for e in range(n_even)],
        even_w_out=[even_w_out[e].astype(BF16) for e in range(n_even)],
        odd_w_in=[odd_w_in[o].astype(BF16) for o in range(n_odd)],
        odd_w_out=[odd_w_out[o].astype(BF16) for o in range(n_odd)],
        b_gate_w=[_prep_gate_w(b_gate_w[e]) for e in range(n_even)],
        s5=[_prep_s5(c_b_re[o], c_b_im[o], c_c_re[o], c_c_im[o]) for o in range(n_odd)],
    )
    b = x_prompt.shape[0]
    dt = x_prompt.dtype
    hw = even_w_in.shape[2] // 5
    groups, states = state_c_re.shape[2], state_c_re.shape[3]
    prompt = _run_group(
        x_prompt, None,
        jnp.zeros((n_even, b, B_CONV - 1, hw), dt), jnp.zeros((n_even, b, hw), dt),
        jnp.zeros((n_odd, b, groups, states), dt), jnp.zeros((n_odd, b, groups, states), dt),
        jnp.zeros((n_odd, b, D_CONV - 1, hw), dt), p, prepped)
    sample = _run_group(
        x_sample, (cache_k, cache_v, page_table), state_conv_b, state_h_b, state_c_re, state_c_im,
        state_conv_d, p, prepped)
    return (prompt[0], sample[0]) + prompt[1:] + sample[1:]
```

```python
import functools
import math

import jax
import jax.numpy as jnp
from jax import lax
from jax.experimental import pallas as pl
from jax.experimental.pallas import tpu as pltpu

F32 = jnp.float32
BF16 = jnp.bfloat16
EPS = 1e-6
NEG = -1e30
LOG2E = 1.4426950408889634
ATTN_SCORE_BOUND = 64.0

LANES = 128
SUBL = 8
V7X_MXU_DIM = 256
V7X_VMEM_LIMIT = 56 * 1024 * 1024

A_HEADS = 4
A_QK = 64
A_V = 128
B_BLOCKS = 8
B_CONV = 4
C_RG = 8.0
C_GROUP = 16
C_STATE = 64
D_CONV = 31
SEG_BLOCK = 512
S5_SCAN_LANES = 1024


def _cparams(n_axes):
    return pltpu.CompilerParams(dimension_semantics=("arbitrary",) * n_axes,
                                vmem_limit_bytes=V7X_VMEM_LIMIT)


def _const_spec(shape):
    nd = len(shape)
    return pl.BlockSpec(shape, lambda *_: (0,) * nd)


def _rms(x, g):
    return x * lax.rsqrt(jnp.mean(x * x, axis=-1, keepdims=True) + EPS) * g


def _gelu_tanh(x):
    return 0.5 * x * (1.0 + jnp.tanh(math.sqrt(2.0 / math.pi) * (x + 0.044715 * (x * x * x))))


def _dot(a, b):
    return jnp.dot(a, b, preferred_element_type=F32)


def _row_tile(m, want):
    t = min(m, want)
    while m % t:
        t //= 2
    return t


def _store_col_tiles(ref, val, seg):
    for c in range(val.shape[1] // LANES):
        tile = val[:, c * LANES:(c + 1) * LANES]
        if seg == 1:
            ref[c] = tile
        else:
            for s in range(SUBL):
                ref[c, pl.ds(s, seg, stride=SUBL), :] = tile[s * seg:(s + 1) * seg, :]


def _load_col_tiles(ref, seg):
    cols = []
    for c in range(ref.shape[0]):
        if seg == 1:
            cols.append(ref[c])
        else:
            cols.append(jnp.concatenate(
                [ref[c, pl.ds(s, seg, stride=SUBL), :] for s in range(SUBL)], axis=0))
    return jnp.concatenate(cols, axis=-1)


def _mixer_half(ref, seg):
    if len(ref.shape) == 2:
        return ref[...]
    return _load_col_tiles(ref, seg).astype(BF16)


def _ffn_body(x_ref, g_ref, wg_ref, wu_ref, wd_ref, *rest, ck, with_mixer, seg):
    x = x_ref[...]
    if with_mixer:
        a_ref, b_ref, wo_ref, o_ref = rest
        hw = wo_ref.shape[0] // 2
        x = (x + _dot(_mixer_half(a_ref, seg), wo_ref[:hw, :])
             + _dot(_mixer_half(b_ref, seg), wo_ref[hw:, :]))
    else:
        o_ref, = rest
    h = _rms(x, g_ref[...]).astype(BF16)
    acc = jnp.zeros(x.shape, F32)
    for c in range(wg_ref.shape[1] // ck):
        sl = slice(c * ck, (c + 1) * ck)
        gate = _dot(h, wg_ref[:, sl])
        up = _dot(h, wu_ref[:, sl])
        a = (gate * jax.nn.sigmoid(gate) * up).astype(BF16)
        acc = acc + _dot(a, wd_ref[sl, :])
    o_ref[...] = x + 0.5 * acc


def _split_gu_body(w_ref, g_ref, u_ref, *, f, fp):
    x = w_ref[0]
    lane = lax.broadcasted_iota(jnp.int32, (x.shape[0], fp), 1)
    g_ref[0] = jnp.where(lane < f, x[:, :fp], 0.0).astype(BF16)
    start = f // LANES * LANES
    up = pltpu.roll(x[:, start:start + fp], fp - (f - start), 1)
    u_ref[0] = jnp.where(lane < f, up, 0.0).astype(BF16)


def _pad_rows_body(w_ref, o_ref, *, f, rows):
    r = lax.broadcasted_iota(jnp.int32, w_ref.shape[1:], 0) + pl.program_id(1) * rows
    o_ref[0] = jnp.where(r < f, w_ref[0], 0.0).astype(BF16)


def _prep_ffn(w_gu, w_down):
    layers, d, f2 = w_gu.shape
    f = f2 // 2
    fp = -(-f // V7X_MXU_DIM) * V7X_MXU_DIM
    start = f // LANES * LANES
    if start + fp > f2 or d % V7X_MXU_DIM:
        pad_c = ((0, 0), (0, 0), (0, fp - f))
        return (jnp.pad(w_gu[..., :f], pad_c).astype(BF16), jnp.pad(w_gu[..., f:], pad_c).astype(BF16),
                jnp.pad(w_down, ((0, 0), (0, fp - f), (0, 0))).astype(BF16))
    rows = V7X_MXU_DIM
    half = pl.BlockSpec((1, rows, fp), lambda l, i: (l, i, 0))
    wg, wu = pl.pallas_call(
        functools.partial(_split_gu_body, f=f, fp=fp),
        grid=(layers, d // rows),
        in_specs=[pl.BlockSpec((1, rows, f2), lambda l, i: (l, i, 0))],
        out_specs=[half, half],
        out_shape=[jax.ShapeDtypeStruct((layers, d, fp), BF16)] * 2,
        compiler_params=_cparams(2),
        name="ffn_split_gate_up",
    )(w_gu)
    rows = fp // 2 if fp % (2 * SUBL) == 0 else rows
    blk = pl.BlockSpec((1, rows, d), lambda l, i: (l, i, 0))
    wd = pl.pallas_call(
        functools.partial(_pad_rows_body, f=f, rows=rows),
        grid=(layers, fp // rows),
        in_specs=[blk],
        out_specs=blk,
        out_shape=jax.ShapeDtypeStruct((layers, fp, d), BF16),
        compiler_params=_cparams(2),
        name="ffn_pad_down",
    )(w_down)
    return wg, wu, wd


def _ffn(x, g, w, layer, mixer=None, seg=1):
    wg, wu, wd = w
    m, d = x.shape
    fp = wg.shape[2]
    tm = SUBL * seg if seg > 1 else _row_tile(m, 512)
    row = pl.BlockSpec((tm, d), lambda i: (i, 0))
    of_layer = lambda r, c: pl.BlockSpec((None, r, c), lambda i: (layer, 0, 0))
    in_specs = [row, _const_spec((1, d)), of_layer(d, fp), of_layer(d, fp), of_layer(fp, d)]
    args = [x, g.reshape(1, d), wg, wu, wd]
    if mixer is not None:
        a, b, w_out = mixer
        hw = w_out.shape[0] // 2
        spec = lambda v: (pl.BlockSpec((tm, hw), lambda i: (i, 0)) if v.ndim == 2 else
                          pl.BlockSpec((v.shape[0], tm, LANES), lambda i: (0, i, 0)))
        in_specs += [spec(a), spec(b), _const_spec((2 * hw, d))]
        args += [a, b, w_out]
    return pl.pallas_call(
        functools.partial(_ffn_body, ck=V7X_MXU_DIM, with_mixer=mixer is not None, seg=seg),
        grid=(m // tm,),
        in_specs=in_specs,
        out_specs=row,
        out_shape=jax.ShapeDtypeStruct((m, d), F32),
        compiler_params=_cparams(1),
        name="ffn_mix" if mixer is not None else "ffn",
    )(*args)


def _inproj_even_body(x_ref, g_ref, w_ref, gq_ref, gk_ref, ones_ref,
                      q_ref, k_ref, kb_ref, v_ref, vb_ref, xb_ref, gb_ref, *, hw, tm, seg):
    x = x_ref[...]
    h = _rms(x, g_ref[...]).astype(BF16)
    proj = _dot(h, w_ref[...])

    def group_rms(t, gain):
        ss = t * t
        hi = ss.astype(BF16)
        lo = (ss - hi.astype(F32)).astype(BF16)
        parts = []
        for c in range(hw // V7X_MXU_DIM):
            sl = slice(c * V7X_MXU_DIM, (c + 1) * V7X_MXU_DIM)
            parts.append(_dot(hi[:, sl], ones_ref[...]) + _dot(lo[:, sl], ones_ref[...]))
        gs = jnp.concatenate(parts, axis=-1)
        return t * lax.rsqrt(gs * (1.0 / A_QK) + EPS) * gain

    q = group_rms(proj[:, 0:hw], gq_ref[...]) * (A_QK ** -0.5 * LOG2E)
    k = group_rms(proj[:, hw:2 * hw], gk_ref[...])
    v = proj[:, 2 * hw:3 * hw]
    q_ref[...] = q.astype(BF16)
    kb_ref[...] = k.astype(BF16)
    vb_ref[...] = v.astype(BF16)
    heads = hw // A_V
    for hd in range(heads):
        k_ref[pl.ds(hd, tm, stride=heads), :] = k[:, hd * A_V:(hd + 1) * A_V]
        v_ref[pl.ds(hd, tm, stride=heads), :] = v[:, hd * A_V:(hd + 1) * A_V]
    _store_col_tiles(xb_ref, proj[:, 3 * hw:4 * hw], seg)
    _store_col_tiles(gb_ref, proj[:, 4 * hw:5 * hw], seg)


def _inproj_even(x, g, w_in, gq, gk, seg):
    m, d = x.shape
    hw = w_in.shape[1] // 5
    heads = hw // A_V
    tm = SUBL * seg if seg > 1 else _row_tile(m, 512)
    row = pl.BlockSpec((tm, d), lambda i: (i, 0))
    orow = pl.BlockSpec((tm, hw), lambda i: (i, 0))
    hrow = pl.BlockSpec((tm * heads, A_V), lambda i: (i, 0))
    cols = pl.BlockSpec((hw // LANES, tm, LANES), lambda i: (0, i, 0))
    idx = jnp.arange(V7X_MXU_DIM) // A_QK
    ones_bd = (idx[:, None] == idx[None, :]).astype(BF16)
    tile = lambda t: jnp.tile(t, hw // A_QK).reshape(1, hw)
    sds = lambda dt: jax.ShapeDtypeStruct((m, hw), dt)
    hsds = jax.ShapeDtypeStruct((m * heads, A_V), F32)
    csds = jax.ShapeDtypeStruct((hw // LANES, m, LANES), F32)
    return pl.pallas_call(
        functools.partial(_inproj_even_body, hw=hw, tm=tm, seg=seg),
        grid=(m // tm,),
        in_specs=[row, _const_spec((1, d)), _const_spec(w_in.shape), _const_spec((1, hw)),
                  _const_spec((1, hw)), _const_spec(ones_bd.shape)],
        out_specs=[orow, hrow, orow, hrow, orow, cols, cols],
        out_shape=[sds(BF16), hsds, sds(BF16), hsds, sds(BF16), csds, csds],
        compiler_params=_cparams(1),
        name="inproj_even",
    )(x, g.reshape(1, d), w_in, tile(gq), tile(gk), ones_bd)


def _diff_lambda(lv, lam_init):
    s01 = jnp.sum(lv[0:1, :] * lv[1:2, :], axis=-1, keepdims=True)
    s23 = jnp.sum(lv[2:3, :] * lv[3:4, :], axis=-1, keepdims=True)
    return jnp.exp(s01) - jnp.exp(s23) + lam_init


def _split_maps(q):
    lane = lax.broadcasted_iota(jnp.int32, q.shape, 1)
    zero = jnp.zeros_like(q)
    return jnp.where(lane < A_QK, q, zero), jnp.where(lane >= A_QK, q, zero)


def _scores(q, k):
    return lax.dot_general(q, k, (((1,), (1,)), ((), ())), preferred_element_type=F32)


def _attn_prompt_body(lam_ref, q_ref, k_ref, v_ref, hg_ref, o_ref, vext_sc, kn2_sc,
                      *, tq, tk, hps, lam_init):
    qi = pl.program_id(2)

    def max_row_norm2(x):
        xf = x.astype(F32)
        per_head = [jnp.sum(xf[:, hh * A_V:(hh + 1) * A_V] ** 2, axis=-1, keepdims=True)
                    for hh in range(hps)]
        return jnp.max(functools.reduce(jnp.maximum, per_head), axis=0, keepdims=True)

    @pl.when(qi == 0)
    def _():
        for hh in range(hps):
            vext_sc[:, 2 * hh * A_V:(2 * hh + 1) * A_V] = v_ref[:, hh * A_V:(hh + 1) * A_V]
            vext_sc[:, (2 * hh + 1) * A_V:(2 * hh + 2) * A_V] = jnp.ones((v_ref.shape[0], A_V), BF16)
        kn2_sc[...] = max_row_norm2(k_ref[...])

    lam = _diff_lambda(lam_ref[...], lam_init)
    q = q_ref[...]
    qs = [m for hh in range(hps) for m in _split_maps(q[:, hh * A_V:(hh + 1) * A_V])]

    per_q = tq // tk

    def chain_inputs(kb, c):
        rows = pl.ds(pl.multiple_of(kb * tk, tk), tk)
        hh = c // 2
        s = _scores(qs[c], k_ref[rows, hh * A_V:(hh + 1) * A_V])
        return s, vext_sc[rows, 2 * hh * A_V:(2 * hh + 2) * A_V]

    def causal(shape, d):
        col = lax.broadcasted_iota(jnp.int32, shape, 1)
        return col + d * tk <= lax.broadcasted_iota(jnp.int32, shape, 0)

    def finish(accs):
        for hh in range(hps):
            a0, a1 = accs[2 * hh], accs[2 * hh + 1]
            o = a0[:, :A_V] / a0[:, A_V:] - lam * (a1[:, :A_V] / a1[:, A_V:])
            o_ref[:, hh * A_V:(hh + 1) * A_V] = (
                _rms(o, hg_ref[...]) * (1.0 - lam_init)).astype(o_ref.dtype)

    def bounded():
        def step(kb, accs, diag):
            new = []
            for c in range(2 * hps):
                s, v = chain_inputs(kb, c)
                p = jnp.exp2(s)
                if diag is not None:
                    p = jnp.where(causal(p.shape, diag), p, 0.0)
                new.append(accs[c] + _dot(p.astype(BF16), v))
            return tuple(new)

        def diag_halves(accs):
            hk = tk // 2
            base = qi * tk
            new = []
            for c in range(2 * hps):
                hh = c // 2
                kcol = slice(hh * A_V, (hh + 1) * A_V)
                vcol = slice(2 * hh * A_V, (2 * hh + 2) * A_V)
                lo = pl.ds(pl.multiple_of(base, hk), hk)
                hi = pl.ds(pl.multiple_of(base + hk, hk), hk)
                p_lo = jnp.exp2(_scores(qs[c], k_ref[lo, kcol]))
                p_lo = jnp.where(causal(p_lo.shape, 0), p_lo, 0.0)
                p_hi = jnp.exp2(_scores(qs[c][hk:, :], k_ref[hi, kcol]))
                p_hi = jnp.where(causal(p_hi.shape, 0), p_hi, 0.0)
                pv_hi = _dot(p_hi.astype(BF16), vext_sc[hi, vcol])
                pv = _dot(p_lo.astype(BF16), vext_sc[lo, vcol])
                new.append(accs[c] + pv + jnp.concatenate([jnp.zeros_like(pv_hi), pv_hi], axis=0))
            return new

        zero = jnp.zeros((tq, 2 * A_V), F32)
        accs = lax.fori_loop(0, qi * per_q, lambda kb, a: step(kb, a, None), (zero,) * (2 * hps))
        if per_q == 1 and tk % (2 * V7X_MXU_DIM) == 0:
            accs = diag_halves(accs)
        else:
            for d in range(per_q):
                accs = step(qi * per_q + d, accs, d)
        finish(accs)

    def general():
        def step(kb, carry, diag):
            new = []
            for c in range(2 * hps):
                m, acc = carry[c]
                s, v = chain_inputs(kb, c)
                if diag is not None:
                    s = jnp.where(causal(s.shape, diag), s, NEG)
                m_new = jnp.maximum(m, jnp.max(s, axis=-1, keepdims=True))
                p = jnp.exp2(s - m_new)
                new.append((m_new, jnp.exp2(m - m_new) * acc + _dot(p.astype(BF16), v)))
            return tuple(new)

        init_one = (jnp.full((tq, 1), NEG, F32), jnp.zeros((tq, 2 * A_V), F32))
        carry = lax.fori_loop(0, qi * per_q, lambda kb, c: step(kb, c, None), (init_one,) * (2 * hps))
        for d in range(per_q):
            carry = step(qi * per_q + d, carry, d)
        finish([acc for _, acc in carry])

    bound2 = max_row_norm2(q) * kn2_sc[...]
    lax.cond(bound2[0, 0] <= ATTN_SCORE_BOUND ** 2, bounded, general)


def _attn_prompt(q, k, v, lam_p, head_g, n, t, lam_init):
    hw = q.shape[1]
    heads = hw // A_V
    hps = next(c for c in (4, 2, 1) if heads % c == 0)
    tk = _row_tile(t, 512)
    tq = tk
    nq = t // tq
    qspec = pl.BlockSpec((tq, hps * A_V), lambda b, h, i: (b * nq + i, h))
    kspec = pl.BlockSpec((t, hps * A_V), lambda b, h, i: (b, h))
    return pl.pallas_call(
        functools.partial(_attn_prompt_body, tq=tq, tk=tk, hps=hps, lam_init=lam_init),
        grid=(n, heads // hps, nq),
        in_specs=[_const_spec(lam_p.shape), qspec, kspec, kspec, _const_spec((1, A_V))],
        out_specs=qspec,
        out_shape=jax.ShapeDtypeStruct((n * t, hw), BF16),
        scratch_shapes=[pltpu.VMEM((t, hps * 2 * A_V), BF16), pltpu.VMEM((1, 1), F32)],
        compiler_params=_cparams(3),
        name="attn_prompt",
    )(lam_p, q, k, v, head_g.reshape(1, A_V))


def _attn_sample_body(pt_ref, lam_ref, q_ref, kn_ref, vn_ref, hg_ref, ck_ref, cv_ref, o_ref,
                      kbuf, vbuf, sem, bias_sc, m_sc, l_sc, acc_sc,
                      *, pps, prow, base, heads, t, lam_init):
    seq, step = pl.program_id(0), pl.program_id(1)
    n_seq, n_step = pl.num_programs(0), pl.num_programs(1)
    flat = seq * n_step + step
    slot = flat % 2

    def page_copies(b, i, s):
        cps = []
        for r in range(pps):
            src = pl.ds(pl.multiple_of((base + pt_ref[b, i * pps + r]) * prow, prow), prow)
            dst = pl.ds(r * prow, prow)
            cps.append(pltpu.make_async_copy(ck_ref.at[src, :], kbuf.at[s, dst, :], sem.at[s, 0]))
            cps.append(pltpu.make_async_copy(cv_ref.at[src, :], vbuf.at[s, dst, :], sem.at[s, 1]))
        return cps

    @pl.when(flat == 0)
    def _():
        for cp in page_copies(seq, step, slot):
            cp.start()

    last = flat == n_seq * n_step - 1
    wrap = step == n_step - 1
    nb = jnp.where(last, seq, jnp.where(wrap, seq + 1, seq))
    ni = jnp.where(last, step, jnp.where(wrap, 0, step + 1))
    for cp in page_copies(nb, ni, 1 - slot):
        cp.start()
    for cp in page_copies(seq, step, slot):
        cp.wait()

    qrows = 2 * t
    q = q_ref[0]
    qall = jnp.concatenate(
        [piece for h in range(heads) for piece in _split_maps(q[:, h * A_V:(h + 1) * A_V])], axis=0)

    def head_match(shape):
        row = lax.broadcasted_iota(jnp.int32, shape, 0)
        col = lax.broadcasted_iota(jnp.int32, shape, 1)
        return row, col, (col % heads) == (row // qrows)

    def update(s, v):
        m = m_sc[...]
        m_new = jnp.maximum(m, jnp.max(s, axis=-1, keepdims=True))
        alpha = jnp.exp2(m - m_new)
        p = jnp.exp2(s - m_new)
        l_sc[...] = alpha * l_sc[...] + jnp.sum(p, axis=-1, keepdims=True)
        acc_sc[...] = alpha * acc_sc[...] + _dot(p.astype(BF16), v)
        m_sc[...] = m_new

    @pl.when((seq == 0) & (step == 0))
    def _():
        _, _, ok = head_match(bias_sc.shape)
        bias_sc[...] = jnp.where(ok, 0.0, NEG)

    @pl.when(step == 0)
    def _():
        m_sc[...] = jnp.full(m_sc.shape, NEG, F32)
        l_sc[...] = jnp.zeros(l_sc.shape, F32)
        acc_sc[...] = jnp.zeros(acc_sc.shape, F32)
        s = _scores(qall, kn_ref[0])
        row, col, ok = head_match(s.shape)
        ok = ok & ((col // heads) <= (row % t))
        update(jnp.where(ok, s, NEG), vn_ref[0])

    update(_scores(qall, kbuf[slot].astype(BF16)) + bias_sc[...], vbuf[slot].astype(BF16))

    @pl.when(last)
    def _():
        for cp in page_copies(seq, step, 1 - slot):
            cp.wait()

    @pl.when(step == n_step - 1)
    def _():
        lam = _diff_lambda(lam_ref[...], lam_init)
        o = acc_sc[...] / l_sc[...]
        for h in range(heads):
            oh = o[h * qrows:h * qrows + t, :] - lam * o[h * qrows + t:(h + 1) * qrows, :]
            o_ref[0, :, h * A_V:(h + 1) * A_V] = (
                _rms(oh, hg_ref[...]) * (1.0 - lam_init)).astype(o_ref.dtype)


def _attn_sample(q, k_new, v_new, cache_k, cache_v, page_table, layer, lam_p, head_g, n, t, lam_init):
    hw = q.shape[1]
    heads = hw // A_V
    n_layers, n_pool, page = cache_k.shape[:3]
    n_pages = page_table.shape[1]
    prow = page * heads
    pps = 16
    while n_pages % pps:
        pps //= 2
    ck = cache_k.reshape(n_layers * n_pool * prow, A_V)
    cv = cache_v.reshape(n_layers * n_pool * prow, A_V)
    pad = ((0, 0), (0, prow - t * heads), (0, 0))
    kn = jnp.pad(k_new.astype(BF16).reshape(n, t * heads, A_V), pad)
    vn = jnp.pad(v_new.astype(BF16).reshape(n, t * heads, A_V), pad)
    qr = heads * 2 * t
    grid_spec = pltpu.PrefetchScalarGridSpec(
        num_scalar_prefetch=1,
        grid=(n, n_pages // pps),
        in_specs=[pl.BlockSpec(lam_p.shape, lambda b, i, pt: (0, 0)),
                  pl.BlockSpec((1, t, hw), lambda b, i, pt: (b, 0, 0)),
                  pl.BlockSpec((1, prow, A_V), lambda b, i, pt: (b, 0, 0)),
                  pl.BlockSpec((1, prow, A_V), lambda b, i, pt: (b, 0, 0)),
                  pl.BlockSpec((1, A_V), lambda b, i, pt: (0, 0)),
                  pl.BlockSpec(memory_space=pl.ANY), pl.BlockSpec(memory_space=pl.ANY)],
        out_specs=pl.BlockSpec((1, t, hw), lambda b, i, pt: (b, 0, 0)),
        scratch_shapes=[pltpu.VMEM((2, pps * prow, A_V), F32), pltpu.VMEM((2, pps * prow, A_V), F32),
                        pltpu.SemaphoreType.DMA((2, 2)),
                        pltpu.VMEM((qr, pps * prow), F32),
                        pltpu.VMEM((qr, 1), F32), pltpu.VMEM((qr, 1), F32), pltpu.VMEM((qr, A_V), F32)],
    )
    out = pl.pallas_call(
        functools.partial(_attn_sample_body, pps=pps, prow=prow, base=layer * n_pool, heads=heads, t=t,
                          lam_init=lam_init),
        grid_spec=grid_spec,
        out_shape=jax.ShapeDtypeStruct((n, t, hw), BF16),
        compiler_params=_cparams(2),
        name="attn_sample",
    )(page_table, lam_p, q.reshape(n, t, hw), kn, vn, head_g.reshape(1, A_V), ck, cv)
    return out.reshape(n * t, hw)


def _shift_rows(x, d, fill):
    row = lax.broadcasted_iota(jnp.int32, x.shape, 0)
    return jnp.where(row >= d, pltpu.roll(x, d, 0), fill)


def _linear_scan(a, b):
    d = 1
    while d < a.shape[0]:
        b = b + a * _shift_rows(b, d, 0.0)
        a = a * _shift_rows(a, d, 1.0)
        d *= 2
    return a, b


def _rglru_body(xb_ref, gb_ref, conv0_ref, h0_ref, cw_ref, cb_ref, wg_ref, gbias_ref, lam_ref,
                rec_ref, convn_ref, hl_ref, ext_sc, h_sc, *, tt, w):
    halo = 8
    step = pl.program_id(1)

    @pl.when(step == 0)
    def _():
        ext_sc[0:halo, :] = jnp.zeros((halo, w), F32)
        ext_sc[halo - (B_CONV - 1):halo, :] = conv0_ref[0]
        h_sc[...] = h0_ref[0]

    x = jnp.concatenate([xb_ref[c, 0] for c in range(w // LANES)], axis=-1)
    gb = jnp.concatenate([gb_ref[c, 0] for c in range(w // LANES)], axis=-1)
    ext_sc[halo:halo + tt, :] = x
    cw = cw_ref[...]
    xc = cb_ref[...] + x * cw[B_CONV - 1:B_CONV, :]
    for j in range(1, B_CONV):
        xc = xc + ext_sc[halo - j:halo - j + tt, :] * cw[B_CONV - 1 - j:B_CONV - j, :]

    a, b = _rglru_coeffs(xc, wg_ref[...], gbias_ref[...], lam_ref[...], w)
    a_cum, h_loc = _linear_scan(a, b)
    h = h_loc + a_cum * h_sc[...]
    rec_ref[0] = (_gelu_tanh(gb) * h).astype(rec_ref.dtype)

    h_sc[...] = h[tt - 1:tt, :]
    hl_ref[0] = h[tt - 1:tt, :]
    convn_ref[0] = ext_sc[halo + tt - (B_CONV - 1):halo + tt, :]
    ext_sc[0:halo, :] = ext_sc[tt:tt + halo, :]


def _rglru_coeffs(xc, gate_w, gate_b, lam, w):
    pre = _dot(xc.astype(BF16), gate_w) + gate_b
    r = jax.nn.sigmoid(pre[:, :w])
    i = jax.nn.sigmoid(pre[:, w:])
    softplus_neg = jnp.maximum(-lam, 0.0) + jnp.log1p(jnp.exp(-jnp.abs(lam)))
    a = jnp.exp((-C_RG) * r * softplus_neg)
    return a, jnp.sqrt(1.0 - a * a) * (i * xc)


def _rglru_seg_body(xb_ref, gb_ref, conv0_ref, h0_ref, cw_ref, cb_ref, wg_ref, gbias_ref, lam_ref,
                    rec_ref, convn_ref, hl_ref, prev_sc, h_sc, *, tt, w):
    seg = tt // SUBL
    hist = B_CONV - 1
    step = pl.program_id(1)

    @pl.when(step == 0)
    def _():
        c0 = conv0_ref[0]
        for i in range(hist):
            prev_sc[i * SUBL:(i + 1) * SUBL, :] = jnp.broadcast_to(c0[i:i + 1, :], (SUBL, w))
        h_sc[...] = h0_ref[0]

    cur = jnp.concatenate([xb_ref[c, 0] for c in range(w // LANES)], axis=-1)
    gb = jnp.concatenate([gb_ref[c, 0] for c in range(w // LANES)], axis=-1)
    row = lax.broadcasted_iota(jnp.int32, (SUBL, w), 0)
    tail = cur[(seg - hist) * SUBL:, :]
    halo = [pltpu.roll(jnp.where(row == SUBL - 1, prev_sc[i * SUBL:(i + 1) * SUBL, :],
                                 tail[i * SUBL:(i + 1) * SUBL, :]), 1, 0) for i in range(hist)]
    ext = jnp.concatenate(halo + [cur], axis=0)
    prev_sc[...] = tail
    convn_ref[0] = jnp.concatenate(
        [tail[i * SUBL + SUBL - 1:(i + 1) * SUBL, :] for i in range(hist)], axis=0)

    cw = cw_ref[...]
    xc = cb_ref[...] + jnp.zeros((tt, w), F32)
    for k in range(B_CONV):
        xc = xc + ext[k * SUBL:k * SUBL + tt, :] * cw[k:k + 1, :]
    a, b = _rglru_coeffs(xc, wg_ref[...], gbias_ref[...], lam_ref[...], w)

    grp = lambda v, j: v[j * SUBL:(j + 1) * SUBL, :]
    e, prod = grp(b, 0), grp(a, 0)
    for j in range(1, seg):
        e, prod = grp(a, j) * e + grp(b, j), grp(a, j) * prod
    c = h_sc[...]
    h_in = jnp.zeros((SUBL, w), F32)
    for k in range(SUBL):
        h_in = jnp.where(row == k, c, h_in)
        c = prod[k:k + 1, :] * c + e[k:k + 1, :]
    h_sc[...] = c
    hl_ref[0] = c
    hs, h = [], h_in
    for j in range(seg):
        h = grp(a, j) * h + grp(b, j)
        hs.append(h)
    rec = _gelu_tanh(gb) * jnp.concatenate(hs, axis=0)
    for cc in range(w // LANES):
        rec_ref[cc, 0] = rec[:, cc * LANES:(cc + 1) * LANES]


def _prep_gate_w(gate_w):
    _, nb, bs, _ = gate_w.shape
    eye = jnp.eye(nb, dtype=gate_w.dtype)
    dense = jnp.einsum('gbij,bc->gbicj', gate_w, eye).reshape(2, nb * bs, nb * bs)
    return jnp.concatenate([dense[0], dense[1]], axis=1).astype(BF16)


def _rglru(xb, gb, conv0, h0, conv_w, conv_b, gate_wd, gate_b, lam_p, n, t, seg):
    wc = xb.shape[0]
    w = wc * LANES
    per_n = lambda rows: pl.BlockSpec((1, rows, w), lambda b, i: (b, 0, 0))
    tt = SUBL * seg if seg > 1 else _row_tile(t, 256)
    cols = pl.BlockSpec((wc, 1, tt, LANES), lambda b, i: (0, b, i, 0))
    in_specs = [cols, cols, per_n(B_CONV - 1), per_n(1), _const_spec((B_CONV, w)), _const_spec((1, w)),
                _const_spec((w, 2 * w)), _const_spec((1, 2 * w)), _const_spec((1, w))]
    args = (xb.reshape(wc, n, t, LANES), gb.reshape(wc, n, t, LANES), conv0, h0.reshape(n, 1, w), conv_w,
            conv_b.reshape(1, w), gate_wd, gate_b.reshape(1, 2 * w), lam_p.reshape(1, w))
    state_shapes = [jax.ShapeDtypeStruct((n, B_CONV - 1, w), F32), jax.ShapeDtypeStruct((n, 1, w), F32)]
    if seg == 1:
        rec, cb, hb = pl.pallas_call(
            functools.partial(_rglru_body, tt=tt, w=w),
            grid=(n, t // tt),
            in_specs=in_specs,
            out_specs=[pl.BlockSpec((1, tt, w), lambda b, i: (b, i, 0)), per_n(B_CONV - 1), per_n(1)],
            out_shape=[jax.ShapeDtypeStruct((n, t, w), BF16)] + state_shapes,
            scratch_shapes=[pltpu.VMEM((tt + 8, w), F32), pltpu.VMEM((1, w), F32)],
            compiler_params=_cparams(2),
            name="rglru",
        )(*args)
        return rec.reshape(n * t, w), cb, hb
    assert seg >= B_CONV - 1
    rec, cb, hb = pl.pallas_call(
        functools.partial(_rglru_seg_body, tt=tt, w=w),
        grid=(n, t // tt),
        in_specs=in_specs,
        out_specs=[cols, per_n(B_CONV - 1), per_n(1)],
        out_shape=[jax.ShapeDtypeStruct((wc, n, t, LANES), F32)] + state_shapes,
        scratch_shapes=[pltpu.VMEM(((B_CONV - 1) * SUBL, w), F32), pltpu.VMEM((1, w), F32)],
        compiler_params=_cparams(2),
        name="rglru_seg",
    )(*args)
    return rec.reshape(wc, n * t, LANES), cb, hb


def _inproj_odd_body(x_ref, g_ref, w_ref, u_ref, gin_ref, *, hw, seg):
    h = _rms(x_ref[...], g_ref[...]).astype(BF16)
    proj = _dot(h, w_ref[...])
    _store_col_tiles(u_ref, proj[:, :hw], seg)
    _store_col_tiles(gin_ref, proj[:, hw:2 * hw] * jax.nn.sigmoid(proj[:, 2 * hw:]), seg)


def _inproj_odd(x, g, w_in, seg):
    m, d = x.shape
    hw = w_in.shape[1] // 3
    tm = SUBL * seg if seg > 1 else _row_tile(m, 512)
    row = pl.BlockSpec((tm, d), lambda i: (i, 0))
    cols = pl.BlockSpec((hw // LANES, tm, LANES), lambda i: (0, i, 0))
    return pl.pallas_call(
        functools.partial(_inproj_odd_body, hw=hw, seg=seg),
        grid=(m // tm,),
        in_specs=[row, _const_spec((1, d)), _const_spec(w_in.shape)],
        out_specs=[cols, cols],
        out_shape=[jax.ShapeDtypeStruct((hw // LANES, m, LANES), F32)] * 2,
        compiler_params=_cparams(1),
        name="inproj_odd",
    )(x, g.reshape(1, d), w_in)


def _cmul(ar, ai, br, bi):
    return ar * br - ai * bi, ar * bi + ai * br


def _s5_discretise(a_re, a_im, log_dt):
    dt = jnp.exp(log_dt)
    mag = jnp.exp(dt * a_re)
    ab_re = mag * jnp.cos(dt * a_im)
    ab_im = mag * jnp.sin(dt * a_im)
    den = a_re * a_re + a_im * a_im
    nr = ab_re - 1.0
    return ab_re, ab_im, (nr * a_re + ab_im * a_im) / den, (ab_im * a_re - nr * a_im) / den


def _s5_body(u_ref, re0_ref, im0_ref, are_ref, aim_ref, ldt_ref,
             bre_ref, bim_ref, cre_ref, cim_ref, d_ref, gw_ref, gbias_ref, o_ref, sre_ref, sim_ref,
             st_re, st_im, bf_re, bf_im, lam_re, lam_im, lseg_re, lseg_im, v_re, v_im,
             *, tt, w, sw):
    seg = tt // SUBL

    def step_rows(j):
        return pl.ds(pl.multiple_of(j * SUBL, SUBL), SUBL)

    @pl.when((pl.program_id(0) == 0) & (pl.program_id(1) == 0))
    def _():
        ab_re, ab_im, coef_re, coef_im = _s5_discretise(are_ref[...], aim_ref[...], ldt_ref[...])
        f_re, f_im = _cmul(coef_re, coef_im, bre_ref[...], bim_ref[...])
        bf_re[...] = f_re.astype(BF16)
        bf_im[...] = f_im.astype(BF16)
        lam_re[...] = jnp.broadcast_to(ab_re, lam_re.shape)
        lam_im[...] = jnp.broadcast_to(ab_im, lam_im.shape)
        pr, pi = ab_re, ab_im
        for _ in range(seg.bit_length() - 1):
            pr, pi = _cmul(pr, pi, pr, pi)
        lseg_re[...] = pr
        lseg_im[...] = pi

    @pl.when(pl.program_id(1) == 0)
    def _():
        st_re[...] = re0_ref[0]
        st_im[...] = im0_ref[0]

    u = jnp.concatenate([u_ref[c, 0] for c in range(w // LANES)], axis=-1)
    ub = u.astype(BF16)
    gpt = V7X_MXU_DIM // C_STATE
    kpt = V7X_MXU_DIM // C_GROUP
    for nt in range(sw // V7X_MXU_DIM):
        kt = (nt * gpt) // kpt
        lhs = ub[:, kt * V7X_MXU_DIM:(kt + 1) * V7X_MXU_DIM]
        rows = slice(kt * V7X_MXU_DIM, (kt + 1) * V7X_MXU_DIM)
        cols = slice(nt * V7X_MXU_DIM, (nt + 1) * V7X_MXU_DIM)
        v_re[:, cols] = _dot(lhs, bf_re[rows, cols])
        v_im[:, cols] = _dot(lhs, bf_im[rows, cols])

    for ch in range(sw // S5_SCAN_LANES):
        lanes = slice(ch * S5_SCAN_LANES, (ch + 1) * S5_SCAN_LANES)
        l_re, l_im = lam_re[:, lanes], lam_im[:, lanes]

        def advance(j, sr, si, lanes=lanes, l_re=l_re, l_im=l_im):
            pr, pi = _cmul(l_re, l_im, sr, si)
            return pr + v_re[step_rows(j), lanes], pi + v_im[step_rows(j), lanes]

        zero = jnp.zeros((SUBL, S5_SCAN_LANES), F32)
        e_re, e_im = lax.fori_loop(0, seg, lambda j, c, adv=advance: adv(j, *c), (zero, zero))

        row = lax.broadcasted_iota(jnp.int32, zero.shape, 0)
        c_re, c_im = st_re[:, lanes], st_im[:, lanes]
        in_re, in_im = zero, zero
        for k in range(SUBL):
            in_re = jnp.where(row == k, c_re, in_re)
            in_im = jnp.where(row == k, c_im, in_im)
            p_re, p_im = _cmul(lseg_re[:, lanes], lseg_im[:, lanes], c_re, c_im)
            c_re, c_im = p_re + e_re[k:k + 1, :], p_im + e_im[k:k + 1, :]
        st_re[:, lanes] = c_re
        st_im[:, lanes] = c_im
        sre_ref[0, :, lanes] = c_re
        sim_ref[0, :, lanes] = c_im

        def emit(j, c, lanes=lanes, adv=advance):
            sr, si = adv(j, *c)
            v_re[step_rows(j), lanes] = sr
            v_im[step_rows(j), lanes] = si
            return sr, si

        lax.fori_loop(0, seg, emit, (in_re, in_im))

    ys = []
    for ot in range(w // V7X_MXU_DIM):
        acc = None
        for nt in range(ot * kpt // gpt, (ot + 1) * kpt // gpt):
            rows = slice(nt * V7X_MXU_DIM, (nt + 1) * V7X_MXU_DIM)
            cols = slice(ot * V7X_MXU_DIM, (ot + 1) * V7X_MXU_DIM)
            part = (_dot(v_re[:, rows].astype(BF16), cre_ref[rows, cols])
                    - _dot(v_im[:, rows].astype(BF16), cim_ref[rows, cols]))
            acc = part if acc is None else acc + part
        ys.append(acc)
    y = jnp.concatenate(ys, axis=-1) + d_ref[...] * u
    z = _gelu_tanh(y)
    res = z * jax.nn.sigmoid(_dot(z.astype(BF16), gw_ref[...]) + gbias_ref[...])
    for c in range(w // LANES):
        o_ref[c, 0] = res[:, c * LANES:(c + 1) * LANES]


def _prep_s5(b_re, b_im, c_re, c_im):
    g = b_re.shape[0]
    eye = jnp.eye(g, dtype=b_re.dtype)
    bd_in = lambda b: jnp.einsum('gpc,gh->gchp', b, eye).reshape(g * C_GROUP, g * C_STATE)
    bd_out = lambda c: jnp.einsum('gcp,gh->gphc', c, eye).reshape(g * C_STATE, g * C_GROUP).astype(BF16)
    return bd_in(b_re), bd_in(b_im), bd_out(c_re), bd_out(c_im)


def _s5(u, re0, im0, a_re, a_im, log_dt, mats, d_skip, glu_w, glu_b, n, t):
    w = u.shape[0] * LANES
    groups = w // C_GROUP
    sw = groups * C_STATE
    tt = _row_tile(t, 512)
    bre, bim, cre, cim = mats
    wc = w // LANES
    seq = pl.BlockSpec((wc, 1, tt, LANES), lambda b, i: (0, b, i, 0))
    st = pl.BlockSpec((1, 1, sw), lambda b, i: (b, 0, 0))
    vm = lambda shape, dt: pltpu.VMEM(shape, dt)
    flat = lambda a: a.reshape(1, sw)
    out, s_re, s_im = pl.pallas_call(
        functools.partial(_s5_body, tt=tt, w=w, sw=sw),
        grid=(n, t // tt),
        in_specs=[seq, st, st] + [_const_spec((1, sw))] * 3
                 + [_const_spec((w, sw)), _const_spec((w, sw)), _const_spec((sw, w)), _const_spec((sw, w)),
                    _const_spec((1, w)), _const_spec((w, w)), _const_spec((1, w))],
        out_specs=[seq, st, st],
        out_shape=[jax.ShapeDtypeStruct((wc, n, t, LANES), F32),
                   jax.ShapeDtypeStruct((n, 1, sw), F32), jax.ShapeDtypeStruct((n, 1, sw), F32)],
        scratch_shapes=[vm((1, sw), F32), vm((1, sw), F32), vm((w, sw), BF16), vm((w, sw), BF16),
                        vm((SUBL, sw), F32), vm((SUBL, sw), F32), vm((1, sw), F32), vm((1, sw), F32),
                        vm((tt, sw), F32), vm((tt, sw), F32)],
        compiler_params=_cparams(2),
        name="s5",
    )(u.reshape(wc, n, t, LANES), re0.reshape(n, 1, sw), im0.reshape(n, 1, sw),
      flat(a_re), flat(a_im), flat(jnp.repeat(log_dt, C_STATE)), bre, bim, cre, cim,
      d_skip.reshape(1, w), glu_w.astype(BF16), glu_b.reshape(1, w))
    return out.reshape(wc, n * t, LANES), s_re, s_im


def _conf_body(g_ref, conv0_ref, cw_ref, cb_ref, lng_ref, lnb_ref, o_ref, convn_ref, ext_sc, *, tt, w):
    halo = 32
    hist = D_CONV - 1
    step = pl.program_id(1)

    @pl.when(step == 0)
    def _():
        ext_sc[0:halo, :] = jnp.zeros((halo, w), F32)
        ext_sc[halo - hist:halo, :] = conv0_ref[0]

    ext_sc[halo:halo + tt, :] = jnp.concatenate([g_ref[c, 0] for c in range(w // LANES)], axis=-1)
    cw = cw_ref[...]
    c = cb_ref[...] + jnp.zeros((tt, w), F32)
    for j in range(D_CONV):
        c = c + ext_sc[halo - hist + j:halo - hist + j + tt, :] * cw[j:j + 1, :]
    o_ref[0] = _ln_silu(c, lng_ref[...], lnb_ref[...]).astype(o_ref.dtype)
    convn_ref[0] = ext_sc[halo + tt - hist:halo + tt, :]
    ext_sc[0:halo, :] = ext_sc[tt:tt + halo, :]


def _ln_silu(c, g, b):
    cc = c - jnp.mean(c, axis=-1, keepdims=True)
    y = cc * lax.rsqrt(jnp.mean(cc * cc, axis=-1, keepdims=True) + EPS) * g + b
    return y * jax.nn.sigmoid(y)


def _conf_seg_body(g_ref, conv0_ref, cw_ref, cb_ref, lng_ref, lnb_ref, o_ref, convn_ref,
                   ext_sc, prev_sc, conv_sc, *, tt, w, rows_per_chunk):
    seg = tt // SUBL
    hist = D_CONV - 1
    step = pl.program_id(1)

    @pl.when(step == 0)
    def _():
        c0 = conv0_ref[0]
        for i in range(hist):
            prev_sc[i * SUBL:(i + 1) * SUBL, :] = jnp.broadcast_to(c0[i:i + 1, :], (SUBL, w))

    cur = jnp.concatenate([g_ref[c, 0] for c in range(w // LANES)], axis=-1)
    ext_sc[hist * SUBL:, :] = cur
    row = lax.broadcasted_iota(jnp.int32, (SUBL, w), 0)
    tail = cur[(seg - hist) * SUBL:, :]
    for i in range(hist):
        grp = jnp.where(row == SUBL - 1, prev_sc[i * SUBL:(i + 1) * SUBL, :],
                        tail[i * SUBL:(i + 1) * SUBL, :])
        ext_sc[i * SUBL:(i + 1) * SUBL, :] = pltpu.roll(grp, 1, 0)
    prev_sc[...] = tail
    convn_ref[0] = jnp.concatenate(
        [tail[i * SUBL + SUBL - 1:(i + 1) * SUBL, :] for i in range(hist)], axis=0)

    cw = cw_ref[...]
    gpc = rows_per_chunk // SUBL
    for c in range(w // LANES):
        lanes = slice(c * LANES, (c + 1) * LANES)
        taps = [jnp.broadcast_to(cw[k:k + 1, lanes], (SUBL, LANES)) for k in range(D_CONV)]
        bias = jnp.broadcast_to(cb_ref[:, lanes], (SUBL, LANES))

        def chunk(i, _, lanes=lanes, taps=taps, bias=bias):
            base = pl.multiple_of(i * rows_per_chunk, rows_per_chunk)
            accs = [bias] * gpc
            for g in range(gpc + D_CONV - 1):
                xg = ext_sc[pl.ds(base + g * SUBL, SUBL), lanes]
                for jj in range(gpc):
                    if 0 <= g - jj < D_CONV:
                        accs[jj] = accs[jj] + xg * taps[g - jj]
            conv_sc[pl.ds(base, rows_per_chunk), lanes] = jnp.concatenate(accs, axis=0)
            return 0

        lax.fori_loop(0, tt // rows_per_chunk, chunk, 0)

    y = _ln_silu(conv_sc[...], lng_ref[...], lnb_ref[...])
    for c in range(w // LANES):
        o_ref[c, 0] = y[:, c * LANES:(c + 1) * LANES]


def _conf(g_in, conv0, conv_w, conv_b, ln_g, ln_b, n, t, seg):
    wc = g_in.shape[0]
    w = wc * LANES
    hist = pl.BlockSpec((1, D_CONV - 1, w), lambda b, i: (b, 0, 0))
    params = (conv0, conv_w, conv_b.reshape(1, w), ln_g.reshape(1, w), ln_b.reshape(1, w))
    param_specs = [hist, _const_spec((D_CONV, w)), _const_spec((1, w)), _const_spec((1, w)),
                   _const_spec((1, w))]
    hist_shape = jax.ShapeDtypeStruct((n, D_CONV - 1, w), F32)
    if seg == 1:
        tt = _row_tile(t, 256)
        cols = pl.BlockSpec((wc, 1, tt, LANES), lambda b, i: (0, b, i, 0))
        out, cd = pl.pallas_call(
            functools.partial(_conf_body, tt=tt, w=w),
            grid=(n, t // tt),
            in_specs=[cols] + param_specs,
            out_specs=[pl.BlockSpec((1, tt, w), lambda b, i: (b, i, 0)), hist],
            out_shape=[jax.ShapeDtypeStruct((n, t, w), BF16), hist_shape],
            scratch_shapes=[pltpu.VMEM((tt + 32, w), F32)],
            compiler_params=_cparams(2),
            name="conformer_conv",
        )(g_in.reshape(wc, n, t, LANES), *params)
        return out.reshape(n * t, w), cd
    assert seg >= D_CONV - 1
    tt = SUBL * seg
    cols = pl.BlockSpec((wc, 1, tt, LANES), lambda b, i: (0, b, i, 0))
    out, cd = pl.pallas_call(
        functools.partial(_conf_seg_body, tt=tt, w=w, rows_per_chunk=64),
        grid=(n, t // tt),
        in_specs=[cols] + param_specs,
        out_specs=[cols, hist],
        out_shape=[jax.ShapeDtypeStruct((wc, n, t, LANES), F32), hist_shape],
        scratch_shapes=[pltpu.VMEM(((seg + D_CONV - 1) * SUBL, w), F32),
                        pltpu.VMEM(((D_CONV - 1) * SUBL, w), F32), pltpu.VMEM((tt, w), F32)],
        compiler_params=_cparams(2),
        name="conformer_conv_seg",
    )(g_in.reshape(wc, n, t, LANES), *params)
    return out.reshape(wc, n * t, LANES), cd


def _run_group(x3, cache, conv_b0, h_b0, c_re0, c_im0, conv_d0, p, prepped):
    n, t, d = x3.shape
    depth = p['ffn1_g'].shape[0]
    x = x3.reshape(n * t, d)
    ks, vs, cbs, hbs, cres, cims, cds = [], [], [], [], [], [], []
    for l in range(depth):
        x = _ffn(x, p['ffn1_g'][l], prepped['ffn1'], l)
        if l % 2 == 0:
            e = l // 2
            hw = p['even_w_in'].shape[2] // 5
            heads = hw // A_V
            seg = SEG_BLOCK // SUBL if t % SEG_BLOCK == 0 else 1
            q, k, kb, v, vb, xb, gb = _inproj_even(x, p['mix_g'][l], prepped['even_w_in'][e],
                                                   p['a_q_g'][e], p['a_k_g'][e], seg)
            lam_init = 0.8 - 0.6 * math.exp(-0.3 * l)
            if cache is None:
                att = _attn_prompt(q, kb, vb, p['a_lambda'][e], p['a_head_g'][e], n, t, lam_init)
            else:
                cache_k, cache_v, page_table = cache
                att = _attn_sample(q, k, v, cache_k, cache_v, page_table, e, p['a_lambda'][e],
                                   p['a_head_g'][e], n, t, lam_init)
            rec, cb, hb = _rglru(xb, gb, conv_b0[e], h_b0[e], p['b_conv_w'][e], p['b_conv_b'][e],
                                 prepped['b_gate_w'][e], p['b_gate_b'][e], p['b_lambda'][e], n, t, seg)
            mixer, mixer_seg = (att, rec, prepped['even_w_out'][e]), seg
            ks.append(k.reshape(n, t, heads, 2 * A_QK))
            vs.append(v.reshape(n, t, heads, A_V))
            cbs.append(cb)
            hbs.append(hb.reshape(n, hw))
        else:
            o = l // 2
            seg = SEG_BLOCK // SUBL if t % SEG_BLOCK == 0 else 1
            u, g_in = _inproj_odd(x, p['mix_g'][l], prepped['odd_w_in'][o], seg)
            groups = u.shape[0] * LANES // C_GROUP
            c_out, cre, cim = _s5(u, c_re0[o], c_im0[o], p['c_a_re'][o], p['c_a_im'][o], p['c_log_dt'][o],
                                  prepped['s5'][o], p['c_d'][o], p['c_glu_w'][o], p['c_glu_b'][o], n, t)
            d_out, cd = _conf(g_in, conv_d0[o], p['d_conv_w'][o], p['d_conv_b'][o],
                              p['d_ln_g'][o], p['d_ln_b'][o], n, t, seg)
            mixer, mixer_seg = (c_out, d_out, prepped['odd_w_out'][o]), seg
            cres.append(cre.reshape(n, groups, C_STATE))
            cims.append(cim.reshape(n, groups, C_STATE))
            cds.append(cd)
        x = _ffn(x, p['ffn2_g'][l], prepped['ffn2'], l, mixer, mixer_seg)
    stack = lambda xs: xs[0][None] if len(xs) == 1 else jnp.stack(xs)
    return (x.reshape(n, t, d), stack(ks), stack(vs), stack(cbs), stack(hbs),
            stack(cres), stack(cims), stack(cds))


def kernel(x_prompt, x_sample, cache_k, cache_v, state_conv_b, state_h_b, state_c_re, state_c_im,
           state_conv_d, page_table, ffn1_g, ffn1_w_gu, ffn1_w_down, mix_g, ffn2_g, ffn2_w_gu,
           ffn2_w_down, even_w_in, even_w_out, a_q_g, a_k_g, a_lambda, a_head_g, b_conv_w, b_conv_b,
           b_gate_w, b_gate_b, b_lambda, odd_w_in, odd_w_out, c_a_re, c_a_im, c_log_dt, c_b_re, c_b_im,
           c_c_re, c_c_im, c_d, c_glu_w, c_glu_b, d_conv_w, d_conv_b, d_ln_g, d_ln_b):
    p = dict(ffn1_g=ffn1_g, mix_g=mix_g, ffn2_g=ffn2_g, even_w_in=even_w_in, a_q_g=a_q_g, a_k_g=a_k_g,
             a_lambda=a_lambda, a_head_g=a_head_g, b_conv_w=b_conv_w, b_conv_b=b_conv_b,
             b_gate_b=b_gate_b, b_lambda=b_lambda, c_a_re=c_a_re, c_a_im=c_a_im, c_log_dt=c_log_dt,
             c_d=c_d, c_glu_w=c_glu_w, c_glu_b=c_glu_b, d_conv_w=d_conv_w, d_conv_b=d_conv_b,
             d_ln_g=d_ln_g, d_ln_b=d_ln_b)
    depth = ffn1_g.shape[0]
    n_even, n_odd = even_w_in.shape[0], odd_w_in.shape[0]
    prepped = dict(
        ffn1=_prep_ffn(ffn1_w_gu, ffn1_w_down),
        ffn2=_prep_ffn(ffn2_w_gu, ffn2_w_down),
        even_w_in=[even_w_in[e].astype(BF16) for e in range(n_even)],
        even_w_out=[even_w_out[e].astype(BF16) for e in range(n_even)],
        odd_w_in=[odd_w_in[o].astype(BF16) for o in range(n_odd)],
        odd_w_out=[odd_w_out[o].astype(BF16) for o in range(n_odd)],
        b_gate_w=[_prep_gate_w(b_gate_w[e]) for e in range(n_even)],
        s5=[_prep_s5(c_b_re[o], c_b_im[o], c_c_re[o], c_c_im[o]) for o in range(n_odd)],
    )
    b = x_prompt.shape[0]
    dt = x_prompt.dtype
    hw = even_w_in.shape[2] // 5
    groups, states = state_c_re.shape[2], state_c_re.shape[3]
    prompt = _run_group(
        x_prompt, None,
        jnp.zeros((n_even, b, B_CONV - 1, hw), dt), jnp.zeros((n_even, b, hw), dt),
        jnp.zeros((n_odd, b, groups, states), dt), jnp.zeros((n_odd, b, groups, states), dt),
        jnp.zeros((n_odd, b, D_CONV - 1, hw), dt), p, prepped)
    sample = _run_group(
        x_sample, (cache_k, cache_v, page_table), state_conv_b, state_h_b, state_c_re, state_c_im,
        state_conv_d, p, prepped)
    return (prompt[0], sample[0]) + prompt[1:] + sample[1:]
```

```python
import functools
import math

import jax
import jax.numpy as jnp
from jax import lax
from jax.experimental import pallas as pl
from jax.experimental.pallas import tpu as pltpu

F32 = jnp.float32
BF16 = jnp.bfloat16
EPS = 1e-6
NEG = -1e30
LOG2E = 1.4426950408889634
ATTN_SCORE_BOUND = 64.0

LANES = 128
SUBL = 8
V7X_MXU_DIM = 256
V7X_VMEM_LIMIT = 56 * 1024 * 1024

A_HEADS = 4
A_QK = 64
A_V = 128
B_BLOCKS = 8
B_CONV = 4
C_RG = 8.0
C_GROUP = 16
C_STATE = 64
D_CONV = 31
SEG_BLOCK = 512
S5_SCAN_LANES = 1024
S5_SCAN_UNROLL = 4


def _cparams(n_axes):
    return pltpu.CompilerParams(dimension_semantics=("arbitrary",) * n_axes,
                                vmem_limit_bytes=V7X_VMEM_LIMIT)


def _const_spec(shape):
    nd = len(shape)
    return pl.BlockSpec(shape, lambda *_: (0,) * nd)


def _rms(x, g):
    return x * lax.rsqrt(jnp.mean(x * x, axis=-1, keepdims=True) + EPS) * g


def _gelu_tanh(x):
    return 0.5 * x * (1.0 + jnp.tanh(math.sqrt(2.0 / math.pi) * (x + 0.044715 * (x * x * x))))


def _dot(a, b):
    return jnp.dot(a, b, preferred_element_type=F32)


def _row_tile(m, want):
    t = min(m, want)
    while m % t:
        t //= 2
    return t


def _store_col_tiles(ref, val, seg):
    for c in range(val.shape[1] // LANES):
        tile = val[:, c * LANES:(c + 1) * LANES]
        if seg == 1:
            ref[c] = tile
        else:
            for s in range(SUBL):
                ref[c, pl.ds(s, seg, stride=SUBL), :] = tile[s * seg:(s + 1) * seg, :]


def _load_col_tiles(ref, seg):
    cols = []
    for c in range(ref.shape[0]):
        if seg == 1:
            cols.append(ref[c])
        else:
            cols.append(jnp.concatenate(
                [ref[c, pl.ds(s, seg, stride=SUBL), :] for s in range(SUBL)], axis=0))
    return jnp.concatenate(cols, axis=-1)


def _mixer_half(ref, seg):
    if len(ref.shape) == 2:
        return ref[...]
    return _load_col_tiles(ref, seg).astype(BF16)


def _ffn_body(x_ref, g_ref, wg_ref, wu_ref, wd_ref, *rest, ck, with_mixer, seg):
    x = x_ref[...]
    if with_mixer:
        a_ref, b_ref, wo_ref, o_ref = rest
        hw = wo_ref.shape[0] // 2
        x = (x + _dot(_mixer_half(a_ref, seg), wo_ref[:hw, :])
             + _dot(_mixer_half(b_ref, seg), wo_ref[hw:, :]))
    else:
        o_ref, = rest
    h = _rms(x, g_ref[...]).astype(BF16)
    acc = jnp.zeros(x.shape, F32)
    for c in range(wg_ref.shape[1] // ck):
        sl = slice(c * ck, (c + 1) * ck)
        gate = _dot(h, wg_ref[:, sl])
        up = _dot(h, wu_ref[:, sl])
        a = (gate * jax.nn.sigmoid(gate) * up).astype(BF16)
        acc = acc + _dot(a, wd_ref[sl, :])
    o_ref[...] = x + 0.5 * acc


def _split_gu_body(w_ref, g_ref, u_ref, *, f, fp):
    x = w_ref[0]
    lane = lax.broadcasted_iota(jnp.int32, (x.shape[0], fp), 1)
    g_ref[0] = jnp.where(lane < f, x[:, :fp], 0.0).astype(BF16)
    start = f // LANES * LANES
    up = pltpu.roll(x[:, start:start + fp], fp - (f - start), 1)
    u_ref[0] = jnp.where(lane < f, up, 0.0).astype(BF16)


def _pad_rows_body(w_ref, o_ref, *, f, rows):
    r = lax.broadcasted_iota(jnp.int32, w_ref.shape[1:], 0) + pl.program_id(1) * rows
    o_ref[0] = jnp.where(r < f, w_ref[0], 0.0).astype(BF16)


def _prep_ffn(w_gu, w_down):
    layers, d, f2 = w_gu.shape
    f = f2 // 2
    fp = -(-f // V7X_MXU_DIM) * V7X_MXU_DIM
    start = f // LANES * LANES
    if start + fp > f2 or d % V7X_MXU_DIM:
        pad_c = ((0, 0), (0, 0), (0, fp - f))
        return (jnp.pad(w_gu[..., :f], pad_c).astype(BF16), jnp.pad(w_gu[..., f:], pad_c).astype(BF16),
                jnp.pad(w_down, ((0, 0), (0, fp - f), (0, 0))).astype(BF16))
    rows = V7X_MXU_DIM
    half = pl.BlockSpec((1, rows, fp), lambda l, i: (l, i, 0))
    wg, wu = pl.pallas_call(
        functools.partial(_split_gu_body, f=f, fp=fp),
        grid=(layers, d // rows),
        in_specs=[pl.BlockSpec((1, rows, f2), lambda l, i: (l, i, 0))],
        out_specs=[half, half],
        out_shape=[jax.ShapeDtypeStruct((layers, d, fp), BF16)] * 2,
        compiler_params=_cparams(2),
        name="ffn_split_gate_up",
    )(w_gu)
    rows = fp // 2 if fp % (2 * SUBL) == 0 else rows
    blk = pl.BlockSpec((1, rows, d), lambda l, i: (l, i, 0))
    wd = pl.pallas_call(
        functools.partial(_pad_rows_body, f=f, rows=rows),
        grid=(layers, fp // rows),
        in_specs=[blk],
        out_specs=blk,
        out_shape=jax.ShapeDtypeStruct((layers, fp, d), BF16),
        compiler_params=_cparams(2),
        name="ffn_pad_down",
    )(w_down)
    return wg, wu, wd


def _ffn(x, g, w, layer, mixer=None, seg=1):
    wg, wu, wd = w
    m, d = x.shape
    fp = wg.shape[2]
    tm = SUBL * seg if seg > 1 else _row_tile(m, 512)
    row = pl.BlockSpec((tm, d), lambda i: (i, 0))
    of_layer = lambda r, c: pl.BlockSpec((None, r, c), lambda i: (layer, 0, 0))
    in_specs = [row, _const_spec((1, d)), of_layer(d, fp), of_layer(d, fp), of_layer(fp, d)]
    args = [x, g.reshape(1, d), wg, wu, wd]
    if mixer is not None:
        a, b, w_out = mixer
        hw = w_out.shape[0] // 2
        spec = lambda v: (pl.BlockSpec((tm, hw), lambda i: (i, 0)) if v.ndim == 2 else
                          pl.BlockSpec((v.shape[0], tm, LANES), lambda i: (0, i, 0)))
        in_specs += [spec(a), spec(b), _const_spec((2 * hw, d))]
        args += [a, b, w_out]
    return pl.pallas_call(
        functools.partial(_ffn_body, ck=V7X_MXU_DIM, with_mixer=mixer is not None, seg=seg),
        grid=(m // tm,),
        in_specs=in_specs,
        out_specs=row,
        out_shape=jax.ShapeDtypeStruct((m, d), F32),
        compiler_params=_cparams(1),
        name="ffn_mix" if mixer is not None else "ffn",
    )(*args)


def _inproj_even_body(x_ref, g_ref, w_ref, gq_ref, gk_ref, ones_ref,
                      q_ref, k_ref, kb_ref, v_ref, vb_ref, xb_ref, gb_ref, *, hw, tm, seg):
    x = x_ref[...]
    h = _rms(x, g_ref[...]).astype(BF16)
    proj = _dot(h, w_ref[...])

    def group_rms(t, gain):
        ss = t * t
        hi = ss.astype(BF16)
        lo = (ss - hi.astype(F32)).astype(BF16)
        parts = []
        for c in range(hw // V7X_MXU_DIM):
            sl = slice(c * V7X_MXU_DIM, (c + 1) * V7X_MXU_DIM)
            parts.append(_dot(hi[:, sl], ones_ref[...]) + _dot(lo[:, sl], ones_ref[...]))
        gs = jnp.concatenate(parts, axis=-1)
        return t * lax.rsqrt(gs * (1.0 / A_QK) + EPS) * gain

    q = group_rms(proj[:, 0:hw], gq_ref[...]) * (A_QK ** -0.5 * LOG2E)
    k = group_rms(proj[:, hw:2 * hw], gk_ref[...])
    v = proj[:, 2 * hw:3 * hw]
    q_ref[...] = q.astype(BF16)
    kb_ref[...] = k.astype(BF16)
    vb_ref[...] = v.astype(BF16)
    heads = hw // A_V
    for hd in range(heads):
        k_ref[pl.ds(hd, tm, stride=heads), :] = k[:, hd * A_V:(hd + 1) * A_V]
        v_ref[pl.ds(hd, tm, stride=heads), :] = v[:, hd * A_V:(hd + 1) * A_V]
    _store_col_tiles(xb_ref, proj[:, 3 * hw:4 * hw], seg)
    _store_col_tiles(gb_ref, proj[:, 4 * hw:5 * hw], seg)


def _inproj_even(x, g, w_in, gq, gk, seg):
    m, d = x.shape
    hw = w_in.shape[1] // 5
    heads = hw // A_V
    tm = SUBL * seg if seg > 1 else _row_tile(m, 512)
    row = pl.BlockSpec((tm, d), lambda i: (i, 0))
    orow = pl.BlockSpec((tm, hw), lambda i: (i, 0))
    hrow = pl.BlockSpec((tm * heads, A_V), lambda i: (i, 0))
    cols = pl.BlockSpec((hw // LANES, tm, LANES), lambda i: (0, i, 0))
    idx = jnp.arange(V7X_MXU_DIM) // A_QK
    ones_bd = (idx[:, None] == idx[None, :]).astype(BF16)
    tile = lambda t: jnp.tile(t, hw // A_QK).reshape(1, hw)
    sds = lambda dt: jax.ShapeDtypeStruct((m, hw), dt)
    hsds = jax.ShapeDtypeStruct((m * heads, A_V), F32)
    csds = jax.ShapeDtypeStruct((hw // LANES, m, LANES), F32)
    return pl.pallas_call(
        functools.partial(_inproj_even_body, hw=hw, tm=tm, seg=seg),
        grid=(m // tm,),
        in_specs=[row, _const_spec((1, d)), _const_spec(w_in.shape), _const_spec((1, hw)),
                  _const_spec((1, hw)), _const_spec(ones_bd.shape)],
        out_specs=[orow, hrow, orow, hrow, orow, cols, cols],
        out_shape=[sds(BF16), hsds, sds(BF16), hsds, sds(BF16), csds, csds],
        compiler_params=_cparams(1),
        name="inproj_even",
    )(x, g.reshape(1, d), w_in, tile(gq), tile(gk), ones_bd)


def _diff_lambda(lv, lam_init):
    s01 = jnp.sum(lv[0:1, :] * lv[1:2, :], axis=-1, keepdims=True)
    s23 = jnp.sum(lv[2:3, :] * lv[3:4, :], axis=-1, keepdims=True)
    return jnp.exp(s01) - jnp.exp(s23) + lam_init


def _split_maps(q):
    lane = lax.broadcasted_iota(jnp.int32, q.shape, 1)
    zero = jnp.zeros_like(q)
    return jnp.where(lane < A_QK, q, zero), jnp.where(lane >= A_QK, q, zero)


def _scores(q, k):
    return lax.dot_general(q, k, (((1,), (1,)), ((), ())), preferred_element_type=F32)


def _attn_prompt_body(lam_ref, q_ref, k_ref, v_ref, hg_ref, o_ref, vext_sc, kn2_sc,
                      *, tq, tk, hps, lam_init):
    qi = pl.program_id(2)

    def max_row_norm2(x):
        xf = x.astype(F32)
        per_head = [jnp.sum(xf[:, hh * A_V:(hh + 1) * A_V] ** 2, axis=-1, keepdims=True)
                    for hh in range(hps)]
        return jnp.max(functools.reduce(jnp.maximum, per_head), axis=0, keepdims=True)

    @pl.when(qi == 0)
    def _():
        for hh in range(hps):
            vext_sc[:, 2 * hh * A_V:(2 * hh + 1) * A_V] = v_ref[:, hh * A_V:(hh + 1) * A_V]
            vext_sc[:, (2 * hh + 1) * A_V:(2 * hh + 2) * A_V] = jnp.ones((v_ref.shape[0], A_V), BF16)
        kn2_sc[...] = max_row_norm2(k_ref[...])

    lam = _diff_lambda(lam_ref[...], lam_init)
    q = q_ref[...]
    qs = [m for hh in range(hps) for m in _split_maps(q[:, hh * A_V:(hh + 1) * A_V])]

    per_q = tq // tk

    def chain_inputs(kb, c):
        rows = pl.ds(pl.multiple_of(kb * tk, tk), tk)
        hh = c // 2
        s = _scores(qs[c], k_ref[rows, hh * A_V:(hh + 1) * A_V])
        return s, vext_sc[rows, 2 * hh * A_V:(2 * hh + 2) * A_V]

    def causal(shape, d):
        col = lax.broadcasted_iota(jnp.int32, shape, 1)
        return col + d * tk <= lax.broadcasted_iota(jnp.int32, shape, 0)

    def finish(accs):
        for hh in range(hps):
            a0, a1 = accs[2 * hh], accs[2 * hh + 1]
            o = a0[:, :A_V] / a0[:, A_V:] - lam * (a1[:, :A_V] / a1[:, A_V:])
            o_ref[:, hh * A_V:(hh + 1) * A_V] = (
                _rms(o, hg_ref[...]) * (1.0 - lam_init)).astype(o_ref.dtype)

    def bounded():
        def step(kb, accs, diag):
            new = []
            for c in range(2 * hps):
                s, v = chain_inputs(kb, c)
                p = jnp.exp2(s)
                if diag is not None:
                    p = jnp.where(causal(p.shape, diag), p, 0.0)
                new.append(accs[c] + _dot(p.astype(BF16), v))
            return tuple(new)

        def diag_halves(accs):
            hk = tk // 2
            base = qi * tk
            new = []
            for c in range(2 * hps):
                hh = c // 2
                kcol = slice(hh * A_V, (hh + 1) * A_V)
                vcol = slice(2 * hh * A_V, (2 * hh + 2) * A_V)
                lo = pl.ds(pl.multiple_of(base, hk), hk)
                hi = pl.ds(pl.multiple_of(base + hk, hk), hk)
                p_lo = jnp.exp2(_scores(qs[c], k_ref[lo, kcol]))
                p_lo = jnp.where(causal(p_lo.shape, 0), p_lo, 0.0)
                p_hi = jnp.exp2(_scores(qs[c][hk:, :], k_ref[hi, kcol]))
                p_hi = jnp.where(causal(p_hi.shape, 0), p_hi, 0.0)
                pv_hi = _dot(p_hi.astype(BF16), vext_sc[hi, vcol])
                pv = _dot(p_lo.astype(BF16), vext_sc[lo, vcol])
                new.append(accs[c] + pv + jnp.concatenate([jnp.zeros_like(pv_hi), pv_hi], axis=0))
            return new

        zero = jnp.zeros((tq, 2 * A_V), F32)
        accs = lax.fori_loop(0, qi * per_q, lambda kb, a: step(kb, a, None), (zero,) * (2 * hps))
        if per_q == 1 and tk % (2 * V7X_MXU_DIM) == 0:
            accs = diag_halves(accs)
        else:
            for d in range(per_q):
                accs = step(qi * per_q + d, accs, d)
        finish(accs)

    def general():
        def step(kb, carry, diag):
            new = []
            for c in range(2 * hps):
                m, acc = carry[c]
                s, v = chain_inputs(kb, c)
                if diag is not None:
                    s = jnp.where(causal(s.shape, diag), s, NEG)
                m_new = jnp.maximum(m, jnp.max(s, axis=-1, keepdims=True))
                p = jnp.exp2(s - m_new)
                new.append((m_new, jnp.exp2(m - m_new) * acc + _dot(p.astype(BF16), v)))
            return tuple(new)

        init_one = (jnp.full((tq, 1), NEG, F32), jnp.zeros((tq, 2 * A_V), F32))
        carry = lax.fori_loop(0, qi * per_q, lambda kb, c: step(kb, c, None), (init_one,) * (2 * hps))
        for d in range(per_q):
            carry = step(qi * per_q + d, carry, d)
        finish([acc for _, acc in carry])

    bound2 = max_row_norm2(q) * kn2_sc[...]
    lax.cond(bound2[0, 0] <= ATTN_SCORE_BOUND ** 2, bounded, general)


def _attn_prompt(q, k, v, lam_p, head_g, n, t, lam_init):
    hw = q.shape[1]
    heads = hw // A_V
    hps = next(c for c in (4, 2, 1) if heads % c == 0)
    tk = _row_tile(t, 512)
    tq = tk
    nq = t // tq
    qspec = pl.BlockSpec((tq, hps * A_V), lambda b, h, i: (b * nq + i, h))
    kspec = pl.BlockSpec((t, hps * A_V), lambda b, h, i: (b, h))
    return pl.pallas_call(
        functools.partial(_attn_prompt_body, tq=tq, tk=tk, hps=hps, lam_init=lam_init),
        grid=(n, heads // hps, nq),
        in_specs=[_const_spec(lam_p.shape), qspec, kspec, kspec, _const_spec((1, A_V))],
        out_specs=qspec,
        out_shape=jax.ShapeDtypeStruct((n * t, hw), BF16),
        scratch_shapes=[pltpu.VMEM((t, hps * 2 * A_V), BF16), pltpu.VMEM((1, 1), F32)],
        compiler_params=_cparams(3),
        name="attn_prompt",
    )(lam_p, q, k, v, head_g.reshape(1, A_V))


def _attn_sample_body(pt_ref, lam_ref, q_ref, kn_ref, vn_ref, hg_ref, ck_ref, cv_ref, o_ref,
                      kbuf, vbuf, sem, bias_sc, m_sc, l_sc, acc_sc,
                      *, pps, prow, base, heads, t, lam_init):
    seq, step = pl.program_id(0), pl.program_id(1)
    n_seq, n_step = pl.num_programs(0), pl.num_programs(1)
    flat = seq * n_step + step
    slot = flat % 2

    def page_copies(b, i, s):
        cps = []
        for r in range(pps):
            src = pl.ds(pl.multiple_of((base + pt_ref[b, i * pps + r]) * prow, prow), prow)
            dst = pl.ds(r * prow, prow)
            cps.append(pltpu.make_async_copy(ck_ref.at[src, :], kbuf.at[s, dst, :], sem.at[s, 0]))
            cps.append(pltpu.make_async_copy(cv_ref.at[src, :], vbuf.at[s, dst, :], sem.at[s, 1]))
        return cps

    @pl.when(flat == 0)
    def _():
        for cp in page_copies(seq, step, slot):
            cp.start()

    last = flat == n_seq * n_step - 1
    wrap = step == n_step - 1
    nb = jnp.where(last, seq, jnp.where(wrap, seq + 1, seq))
    ni = jnp.where(last, step, jnp.where(wrap, 0, step + 1))
    for cp in page_copies(nb, ni, 1 - slot):
        cp.start()
    for cp in page_copies(seq, step, slot):
        cp.wait()

    qrows = 2 * t
    q = q_ref[0]
    qall = jnp.concatenate(
        [piece for h in range(heads) for piece in _split_maps(q[:, h * A_V:(h + 1) * A_V])], axis=0)

    def head_match(shape):
        row = lax.broadcasted_iota(jnp.int32, shape, 0)
        col = lax.broadcasted_iota(jnp.int32, shape, 1)
        return row, col, (col % heads) == (row // qrows)

    def update(s, v):
        m = m_sc[...]
        m_new = jnp.maximum(m, jnp.max(s, axis=-1, keepdims=True))
        alpha = jnp.exp2(m - m_new)
        p = jnp.exp2(s - m_new)
        l_sc[...] = alpha * l_sc[...] + jnp.sum(p, axis=-1, keepdims=True)
        acc_sc[...] = alpha * acc_sc[...] + _dot(p.astype(BF16), v)
        m_sc[...] = m_new

    @pl.when((seq == 0) & (step == 0))
    def _():
        _, _, ok = head_match(bias_sc.shape)
        bias_sc[...] = jnp.where(ok, 0.0, NEG)

    @pl.when(step == 0)
    def _():
        m_sc[...] = jnp.full(m_sc.shape, NEG, F32)
        l_sc[...] = jnp.zeros(l_sc.shape, F32)
        acc_sc[...] = jnp.zeros(acc_sc.shape, F32)
        s = _scores(qall, kn_ref[0])
        row, col, ok = head_match(s.shape)
        ok = ok & ((col // heads) <= (row % t))
        update(jnp.where(ok, s, NEG), vn_ref[0])

    update(_scores(qall, kbuf[slot].astype(BF16)) + bias_sc[...], vbuf[slot].astype(BF16))

    @pl.when(last)
    def _():
        for cp in page_copies(seq, step, 1 - slot):
            cp.wait()

    @pl.when(step == n_step - 1)
    def _():
        lam = _diff_lambda(lam_ref[...], lam_init)
        o = acc_sc[...] / l_sc[...]
        for h in range(heads):
            oh = o[h * qrows:h * qrows + t, :] - lam * o[h * qrows + t:(h + 1) * qrows, :]
            o_ref[0, :, h * A_V:(h + 1) * A_V] = (
                _rms(oh, hg_ref[...]) * (1.0 - lam_init)).astype(o_ref.dtype)


def _attn_sample(q, k_new, v_new, cache_k, cache_v, page_table, layer, lam_p, head_g, n, t, lam_init):
    hw = q.shape[1]
    heads = hw // A_V
    n_layers, n_pool, page = cache_k.shape[:3]
    n_pages = page_table.shape[1]
    prow = page * heads
    pps = 16
    while n_pages % pps:
        pps //= 2
    ck = cache_k.reshape(n_layers * n_pool * prow, A_V)
    cv = cache_v.reshape(n_layers * n_pool * prow, A_V)
    pad = ((0, 0), (0, prow - t * heads), (0, 0))
    kn = jnp.pad(k_new.astype(BF16).reshape(n, t * heads, A_V), pad)
    vn = jnp.pad(v_new.astype(BF16).reshape(n, t * heads, A_V), pad)
    qr = heads * 2 * t
    grid_spec = pltpu.PrefetchScalarGridSpec(
        num_scalar_prefetch=1,
        grid=(n, n_pages // pps),
        in_specs=[pl.BlockSpec(lam_p.shape, lambda b, i, pt: (0, 0)),
                  pl.BlockSpec((1, t, hw), lambda b, i, pt: (b, 0, 0)),
                  pl.BlockSpec((1, prow, A_V), lambda b, i, pt: (b, 0, 0)),
                  pl.BlockSpec((1, prow, A_V), lambda b, i, pt: (b, 0, 0)),
                  pl.BlockSpec((1, A_V), lambda b, i, pt: (0, 0)),
                  pl.BlockSpec(memory_space=pl.ANY), pl.BlockSpec(memory_space=pl.ANY)],
        out_specs=pl.BlockSpec((1, t, hw), lambda b, i, pt: (b, 0, 0)),
        scratch_shapes=[pltpu.VMEM((2, pps * prow, A_V), F32), pltpu.VMEM((2, pps * prow, A_V), F32),
                        pltpu.SemaphoreType.DMA((2, 2)),
                        pltpu.VMEM((qr, pps * prow), F32),
                        pltpu.VMEM((qr, 1), F32), pltpu.VMEM((qr, 1), F32), pltpu.VMEM((qr, A_V), F32)],
    )
    out = pl.pallas_call(
        functools.partial(_attn_sample_body, pps=pps, prow=prow, base=layer * n_pool, heads=heads, t=t,
                          lam_init=lam_init),
        grid_spec=grid_spec,
        out_shape=jax.ShapeDtypeStruct((n, t, hw), BF16),
        compiler_params=_cparams(2),
        name="attn_sample",
    )(page_table, lam_p, q.reshape(n, t, hw), kn, vn, head_g.reshape(1, A_V), ck, cv)
    return out.reshape(n * t, hw)


def _shift_rows(x, d, fill):
    row = lax.broadcasted_iota(jnp.int32, x.shape, 0)
    return jnp.where(row >= d, pltpu.roll(x, d, 0), fill)


def _linear_scan(a, b):
    d = 1
    while d < a.shape[0]:
        b = b + a * _shift_rows(b, d, 0.0)
        a = a * _shift_rows(a, d, 1.0)
        d *= 2
    return a, b


def _rglru_body(xb_ref, gb_ref, conv0_ref, h0_ref, cw_ref, cb_ref, wg_ref, gbias_ref, lam_ref,
                rec_ref, convn_ref, hl_ref, ext_sc, h_sc, *, tt, w):
    halo = 8
    step = pl.program_id(1)

    @pl.when(step == 0)
    def _():
        ext_sc[0:halo, :] = jnp.zeros((halo, w), F32)
        ext_sc[halo - (B_CONV - 1):halo, :] = conv0_ref[0]
        h_sc[...] = h0_ref[0]

    x = jnp.concatenate([xb_ref[c, 0] for c in range(w // LANES)], axis=-1)
    gb = jnp.concatenate([gb_ref[c, 0] for c in range(w // LANES)], axis=-1)
    ext_sc[halo:halo + tt, :] = x
    cw = cw_ref[...]
    xc = cb_ref[...] + x * cw[B_CONV - 1:B_CONV, :]
    for j in range(1, B_CONV):
        xc = xc + ext_sc[halo - j:halo - j + tt, :] * cw[B_CONV - 1 - j:B_CONV - j, :]

    a, b = _rglru_coeffs(xc, wg_ref[...], gbias_ref[...], lam_ref[...], w)
    a_cum, h_loc = _linear_scan(a, b)
    h = h_loc + a_cum * h_sc[...]
    rec_ref[0] = (_gelu_tanh(gb) * h).astype(rec_ref.dtype)

    h_sc[...] = h[tt - 1:tt, :]
    hl_ref[0] = h[tt - 1:tt, :]
    convn_ref[0] = ext_sc[halo + tt - (B_CONV - 1):halo + tt, :]
    ext_sc[0:halo, :] = ext_sc[tt:tt + halo, :]


def _rglru_coeffs(xc, gate_w, gate_b, lam, w):
    pre = _dot(xc.astype(BF16), gate_w) + gate_b
    r = jax.nn.sigmoid(pre[:, :w])
    i = jax.nn.sigmoid(pre[:, w:])
    softplus_neg = jnp.maximum(-lam, 0.0) + jnp.log1p(jnp.exp(-jnp.abs(lam)))
    a = jnp.exp((-C_RG) * r * softplus_neg)
    return a, jnp.sqrt(1.0 - a * a) * (i * xc)


def _rglru_seg_body(xb_ref, gb_ref, conv0_ref, h0_ref, cw_ref, cb_ref, wg_ref, gbias_ref, lam_ref,
                    rec_ref, convn_ref, hl_ref, prev_sc, h_sc, *, tt, w):
    seg = tt // SUBL
    hist = B_CONV - 1
    step = pl.program_id(1)

    @pl.when(step == 0)
    def _():
        c0 = conv0_ref[0]
        for i in range(hist):
            prev_sc[i * SUBL:(i + 1) * SUBL, :] = jnp.broadcast_to(c0[i:i + 1, :], (SUBL, w))
        h_sc[...] = h0_ref[0]

    cur = jnp.concatenate([xb_ref[c, 0] for c in range(w // LANES)], axis=-1)
    gb = jnp.concatenate([gb_ref[c, 0] for c in range(w // LANES)], axis=-1)
    row = lax.broadcasted_iota(jnp.int32, (SUBL, w), 0)
    tail = cur[(seg - hist) * SUBL:, :]
    halo = [pltpu.roll(jnp.where(row == SUBL - 1, prev_sc[i * SUBL:(i + 1) * SUBL, :],
                                 tail[i * SUBL:(i + 1) * SUBL, :]), 1, 0) for i in range(hist)]
    ext = jnp.concatenate(halo + [cur], axis=0)
    prev_sc[...] = tail
    convn_ref[0] = jnp.concatenate(
        [tail[i * SUBL + SUBL - 1:(i + 1) * SUBL, :] for i in range(hist)], axis=0)

    cw = cw_ref[...]
    xc = cb_ref[...] + jnp.zeros((tt, w), F32)
    for k in range(B_CONV):
        xc = xc + ext[k * SUBL:k * SUBL + tt, :] * cw[k:k + 1, :]
    a, b = _rglru_coeffs(xc, wg_ref[...], gbias_ref[...], lam_ref[...], w)

    grp = lambda v, j: v[j * SUBL:(j + 1) * SUBL, :]
    e, prod = grp(b, 0), grp(a, 0)
    for j in range(1, seg):
        e, prod = grp(a, j) * e + grp(b, j), grp(a, j) * prod
    c = h_sc[...]
    h_in = jnp.zeros((SUBL, w), F32)
    for k in range(SUBL):
        h_in = jnp.where(row == k, c, h_in)
        c = prod[k:k + 1, :] * c + e[k:k + 1, :]
    h_sc[...] = c
    hl_ref[0] = c
    hs, h = [], h_in
    for j in range(seg):
        h = grp(a, j) * h + grp(b, j)
        hs.append(h)
    rec = _gelu_tanh(gb) * jnp.concatenate(hs, axis=0)
    for cc in range(w // LANES):
        rec_ref[cc, 0] = rec[:, cc * LANES:(cc + 1) * LANES]


def _prep_gate_w(gate_w):
    _, nb, bs, _ = gate_w.shape
    eye = jnp.eye(nb, dtype=gate_w.dtype)
    dense = jnp.einsum('gbij,bc->gbicj', gate_w, eye).reshape(2, nb * bs, nb * bs)
    return jnp.concatenate([dense[0], dense[1]], axis=1).astype(BF16)


def _rglru(xb, gb, conv0, h0, conv_w, conv_b, gate_wd, gate_b, lam_p, n, t, seg):
    wc = xb.shape[0]
    w = wc * LANES
    per_n = lambda rows: pl.BlockSpec((1, rows, w), lambda b, i: (b, 0, 0))
    tt = SUBL * seg if seg > 1 else _row_tile(t, 256)
    cols = pl.BlockSpec((wc, 1, tt, LANES), lambda b, i: (0, b, i, 0))
    in_specs = [cols, cols, per_n(B_CONV - 1), per_n(1), _const_spec((B_CONV, w)), _const_spec((1, w)),
                _const_spec((w, 2 * w)), _const_spec((1, 2 * w)), _const_spec((1, w))]
    args = (xb.reshape(wc, n, t, LANES), gb.reshape(wc, n, t, LANES), conv0, h0.reshape(n, 1, w), conv_w,
            conv_b.reshape(1, w), gate_wd, gate_b.reshape(1, 2 * w), lam_p.reshape(1, w))
    state_shapes = [jax.ShapeDtypeStruct((n, B_CONV - 1, w), F32), jax.ShapeDtypeStruct((n, 1, w), F32)]
    if seg == 1:
        rec, cb, hb = pl.pallas_call(
            functools.partial(_rglru_body, tt=tt, w=w),
            grid=(n, t // tt),
            in_specs=in_specs,
            out_specs=[pl.BlockSpec((1, tt, w), lambda b, i: (b, i, 0)), per_n(B_CONV - 1), per_n(1)],
            out_shape=[jax.ShapeDtypeStruct((n, t, w), BF16)] + state_shapes,
            scratch_shapes=[pltpu.VMEM((tt + 8, w), F32), pltpu.VMEM((1, w), F32)],
            compiler_params=_cparams(2),
            name="rglru",
        )(*args)
        return rec.reshape(n * t, w), cb, hb
    assert seg >= B_CONV - 1
    rec, cb, hb = pl.pallas_call(
        functools.partial(_rglru_seg_body, tt=tt, w=w),
        grid=(n, t // tt),
        in_specs=in_specs,
        out_specs=[cols, per_n(B_CONV - 1), per_n(1)],
        out_shape=[jax.ShapeDtypeStruct((wc, n, t, LANES), F32)] + state_shapes,
        scratch_shapes=[pltpu.VMEM(((B_CONV - 1) * SUBL, w), F32), pltpu.VMEM((1, w), F32)],
        compiler_params=_cparams(2),
        name="rglru_seg",
    )(*args)
    return rec.reshape(wc, n * t, LANES), cb, hb


def _inproj_odd_body(x_ref, g_ref, w_ref, u_ref, gin_ref, *, hw, seg):
    h = _rms(x_ref[...], g_ref[...]).astype(BF16)
    proj = _dot(h, w_ref[...])
    _store_col_tiles(u_ref, proj[:, :hw], seg)
    _store_col_tiles(gin_ref, proj[:, hw:2 * hw] * jax.nn.sigmoid(proj[:, 2 * hw:]), seg)


def _inproj_odd(x, g, w_in, seg):
    m, d = x.shape
    hw = w_in.shape[1] // 3
    tm = SUBL * seg if seg > 1 else _row_tile(m, 512)
    row = pl.BlockSpec((tm, d), lambda i: (i, 0))
    cols = pl.BlockSpec((hw // LANES, tm, LANES), lambda i: (0, i, 0))
    return pl.pallas_call(
        functools.partial(_inproj_odd_body, hw=hw, seg=seg),
        grid=(m // tm,),
        in_specs=[row, _const_spec((1, d)), _const_spec(w_in.shape)],
        out_specs=[cols, cols],
        out_shape=[jax.ShapeDtypeStruct((hw // LANES, m, LANES), F32)] * 2,
        compiler_params=_cparams(1),
        name="inproj_odd",
    )(x, g.reshape(1, d), w_in)


def _cmul(ar, ai, br, bi):
    return ar * br - ai * bi, ar * bi + ai * br


def _s5_discretise(a_re, a_im, log_dt):
    dt = jnp.exp(log_dt)
    mag = jnp.exp(dt * a_re)
    ab_re = mag * jnp.cos(dt * a_im)
    ab_im = mag * jnp.sin(dt * a_im)
    den = a_re * a_re + a_im * a_im
    nr = ab_re - 1.0
    return ab_re, ab_im, (nr * a_re + ab_im * a_im) / den, (ab_im * a_re - nr * a_im) / den


def _s5_body(u_ref, re0_ref, im0_ref, are_ref, aim_ref, ldt_ref,
             bre_ref, bim_ref, cre_ref, cim_ref, d_ref, gw_ref, gbias_ref, o_ref, sre_ref, sim_ref,
             st_re, st_im, bf_re, bf_im, lam_re, lam_im, lseg_re, lseg_im, v_re, v_im,
             *, tt, w, sw):
    seg = tt // SUBL

    def step_rows(j):
        return pl.ds(pl.multiple_of(j * SUBL, SUBL), SUBL)

    @pl.when((pl.program_id(0) == 0) & (pl.program_id(1) == 0))
    def _():
        ab_re, ab_im, coef_re, coef_im = _s5_discretise(are_ref[...], aim_ref[...], ldt_ref[...])
        f_re, f_im = _cmul(coef_re, coef_im, bre_ref[...], bim_ref[...])
        bf_re[...] = f_re.astype(BF16)
        bf_im[...] = f_im.astype(BF16)
        lam_re[...] = jnp.broadcast_to(ab_re, lam_re.shape)
        lam_im[...] = jnp.broadcast_to(ab_im, lam_im.shape)
        pr, pi = ab_re, ab_im
        for _ in range(seg.bit_length() - 1):
            pr, pi = _cmul(pr, pi, pr, pi)
        lseg_re[...] = pr
        lseg_im[...] = pi

    @pl.when(pl.program_id(1) == 0)
    def _():
        st_re[...] = re0_ref[0]
        st_im[...] = im0_ref[0]

    u = jnp.concatenate([u_ref[c, 0] for c in range(w // LANES)], axis=-1)
    ub = u.astype(BF16)
    gpt = V7X_MXU_DIM // C_STATE
    kpt = V7X_MXU_DIM // C_GROUP
    for nt in range(sw // V7X_MXU_DIM):
        kt = (nt * gpt) // kpt
        lhs = ub[:, kt * V7X_MXU_DIM:(kt + 1) * V7X_MXU_DIM]
        rows = slice(kt * V7X_MXU_DIM, (kt + 1) * V7X_MXU_DIM)
        cols = slice(nt * V7X_MXU_DIM, (nt + 1) * V7X_MXU_DIM)
        v_re[:, cols] = _dot(lhs, bf_re[rows, cols])
        v_im[:, cols] = _dot(lhs, bf_im[rows, cols])

    for ch in range(sw // S5_SCAN_LANES):
        lanes = slice(ch * S5_SCAN_LANES, (ch + 1) * S5_SCAN_LANES)
        l_re, l_im = lam_re[:, lanes], lam_im[:, lanes]

        def advance(j, sr, si, lanes=lanes, l_re=l_re, l_im=l_im):
            pr, pi = _cmul(l_re, l_im, sr, si)
            return pr + v_re[step_rows(j), lanes], pi + v_im[step_rows(j), lanes]

        zero = jnp.zeros((SUBL, S5_SCAN_LANES), F32)
        e_re, e_im = lax.fori_loop(0, seg, lambda j, c, adv=advance: adv(j, *c), (zero, zero),
                                   unroll=min(seg, S5_SCAN_UNROLL))

        row = lax.broadcasted_iota(jnp.int32, zero.shape, 0)
        c_re, c_im = st_re[:, lanes], st_im[:, lanes]
        in_re, in_im = zero, zero
        for k in range(SUBL):
            in_re = jnp.where(row == k, c_re, in_re)
            in_im = jnp.where(row == k, c_im, in_im)
            p_re, p_im = _cmul(lseg_re[:, lanes], lseg_im[:, lanes], c_re, c_im)
            c_re, c_im = p_re + e_re[k:k + 1, :], p_im + e_im[k:k + 1, :]
        st_re[:, lanes] = c_re
        st_im[:, lanes] = c_im
        sre_ref[0, :, lanes] = c_re
        sim_ref[0, :, lanes] = c_im

        def emit(j, c, lanes=lanes, adv=advance):
            sr, si = adv(j, *c)
            v_re[step_rows(j), lanes] = sr
            v_im[step_rows(j), lanes] = si
            return sr, si

        lax.fori_loop(0, seg, emit, (in_re, in_im), unroll=min(seg, S5_SCAN_UNROLL))

    ys = []
    for ot in range(w // V7X_MXU_DIM):
        acc = None
        for nt in range(ot * kpt // gpt, (ot + 1) * kpt // gpt):
            rows = slice(nt * V7X_MXU_DIM, (nt + 1) * V7X_MXU_DIM)
            cols = slice(ot * V7X_MXU_DIM, (ot + 1) * V7X_MXU_DIM)
            part = (_dot(v_re[:, rows].astype(BF16), cre_ref[rows, cols])
                    - _dot(v_im[:, rows].astype(BF16), cim_ref[rows, cols]))
            acc = part if acc is None else acc + part
        ys.append(acc)
    y = jnp.concatenate(ys, axis=-1) + d_ref[...] * u
    z = _gelu_tanh(y)
    res = z * jax.nn.sigmoid(_dot(z.astype(BF16), gw_ref[...]) + gbias_ref[...])
    for c in range(w // LANES):
        o_ref[c, 0] = res[:, c * LANES:(c + 1) * LANES]


def _prep_s5(b_re, b_im, c_re, c_im):
    g = b_re.shape[0]
    eye = jnp.eye(g, dtype=b_re.dtype)
    bd_in = lambda b: jnp.einsum('gpc,gh->gchp', b, eye).reshape(g * C_GROUP, g * C_STATE)
    bd_out = lambda c: jnp.einsum('gcp,gh->gphc', c, eye).reshape(g * C_STATE, g * C_GROUP).astype(BF16)
    return bd_in(b_re), bd_in(b_im), bd_out(c_re), bd_out(c_im)


def _s5(u, re0, im0, a_re, a_im, log_dt, mats, d_skip, glu_w, glu_b, n, t):
    w = u.shape[0] * LANES
    groups = w // C_GROUP
    sw = groups * C_STATE
    tt = _row_tile(t, 512)
    bre, bim, cre, cim = mats
    wc = w // LANES
    seq = pl.BlockSpec((wc, 1, tt, LANES), lambda b, i: (0, b, i, 0))
    st = pl.BlockSpec((1, 1, sw), lambda b, i: (b, 0, 0))
    vm = lambda shape, dt: pltpu.VMEM(shape, dt)
    flat = lambda a: a.reshape(1, sw)
    out, s_re, s_im = pl.pallas_call(
        functools.partial(_s5_body, tt=tt, w=w, sw=sw),
        grid=(n, t // tt),
        in_specs=[seq, st, st] + [_const_spec((1, sw))] * 3
                 + [_const_spec((w, sw)), _const_spec((w, sw)), _const_spec((sw, w)), _const_spec((sw, w)),
                    _const_spec((1, w)), _const_spec((w, w)), _const_spec((1, w))],
        out_specs=[seq, st, st],
        out_shape=[jax.ShapeDtypeStruct((wc, n, t, LANES), F32),
                   jax.ShapeDtypeStruct((n, 1, sw), F32), jax.ShapeDtypeStruct((n, 1, sw), F32)],
        scratch_shapes=[vm((1, sw), F32), vm((1, sw), F32), vm((w, sw), BF16), vm((w, sw), BF16),
                        vm((SUBL, sw), F32), vm((SUBL, sw), F32), vm((1, sw), F32), vm((1, sw), F32),
                        vm((tt, sw), F32), vm((tt, sw), F32)],
        compiler_params=_cparams(2),
        name="s5",
    )(u.reshape(wc, n, t, LANES), re0.reshape(n, 1, sw), im0.reshape(n, 1, sw),
      flat(a_re), flat(a_im), flat(jnp.repeat(log_dt, C_STATE)), bre, bim, cre, cim,
      d_skip.reshape(1, w), glu_w.astype(BF16), glu_b.reshape(1, w))
    return out.reshape(wc, n * t, LANES), s_re, s_im


def _conf_body(g_ref, conv0_ref, cw_ref, cb_ref, lng_ref, lnb_ref, o_ref, convn_ref, ext_sc, *, tt, w):
    halo = 32
    hist = D_CONV - 1
    step = pl.program_id(1)

    @pl.when(step == 0)
    def _():
        ext_sc[0:halo, :] = jnp.zeros((halo, w), F32)
        ext_sc[halo - hist:halo, :] = conv0_ref[0]

    ext_sc[halo:halo + tt, :] = jnp.concatenate([g_ref[c, 0] for c in range(w // LANES)], axis=-1)
    cw = cw_ref[...]
    c = cb_ref[...] + jnp.zeros((tt, w), F32)
    for j in range(D_CONV):
        c = c + ext_sc[halo - hist + j:halo - hist + j + tt, :] * cw[j:j + 1, :]
    o_ref[0] = _ln_silu(c, lng_ref[...], lnb_ref[...]).astype(o_ref.dtype)
    convn_ref[0] = ext_sc[halo + tt - hist:halo + tt, :]
    ext_sc[0:halo, :] = ext_sc[tt:tt + halo, :]


def _ln_silu(c, g, b):
    cc = c - jnp.mean(c, axis=-1, keepdims=True)
    y = cc * lax.rsqrt(jnp.mean(cc * cc, axis=-1, keepdims=True) + EPS) * g + b
    return y * jax.nn.sigmoid(y)


def _conf_seg_body(g_ref, conv0_ref, cw_ref, cb_ref, lng_ref, lnb_ref, o_ref, convn_ref,
                   ext_sc, prev_sc, conv_sc, *, tt, w, rows_per_chunk):
    seg = tt // SUBL
    hist = D_CONV - 1
    step = pl.program_id(1)

    @pl.when(step == 0)
    def _():
        c0 = conv0_ref[0]
        for i in range(hist):
            prev_sc[i * SUBL:(i + 1) * SUBL, :] = jnp.broadcast_to(c0[i:i + 1, :], (SUBL, w))

    cur = jnp.concatenate([g_ref[c, 0] for c in range(w // LANES)], axis=-1)
    ext_sc[hist * SUBL:, :] = cur
    row = lax.broadcasted_iota(jnp.int32, (SUBL, w), 0)
    tail = cur[(seg - hist) * SUBL:, :]
    for i in range(hist):
        grp = jnp.where(row == SUBL - 1, prev_sc[i * SUBL:(i + 1) * SUBL, :],
                        tail[i * SUBL:(i + 1) * SUBL, :])
        ext_sc[i * SUBL:(i + 1) * SUBL, :] = pltpu.roll(grp, 1, 0)
    prev_sc[...] = tail
    convn_ref[0] = jnp.concatenate(
        [tail[i * SUBL + SUBL - 1:(i + 1) * SUBL, :] for i in range(hist)], axis=0)

    cw = cw_ref[...]
    gpc = rows_per_chunk // SUBL
    for c in range(w // LANES):
        lanes = slice(c * LANES, (c + 1) * LANES)
        taps = [jnp.broadcast_to(cw[k:k + 1, lanes], (SUBL, LANES)) for k in range(D_CONV)]
        bias = jnp.broadcast_to(cb_ref[:, lanes], (SUBL, LANES))

        def chunk(i, _, lanes=lanes, taps=taps, bias=bias):
            base = pl.multiple_of(i * rows_per_chunk, rows_per_chunk)
            accs = [bias] * gpc
            for g in range(gpc + D_CONV - 1):
                xg = ext_sc[pl.ds(base + g * SUBL, SUBL), lanes]
                for jj in range(gpc):
                    if 0 <= g - jj < D_CONV:
                        accs[jj] = accs[jj] + xg * taps[g - jj]
            conv_sc[pl.ds(base, rows_per_chunk), lanes] = jnp.concatenate(accs, axis=0)
            return 0

        lax.fori_loop(0, tt // rows_per_chunk, chunk, 0)

    y = _ln_silu(conv_sc[...], lng_ref[...], lnb_ref[...])
    for c in range(w // LANES):
        o_ref[c, 0] = y[:, c * LANES:(c + 1) * LANES]


def _conf(g_in, conv0, conv_w, conv_b, ln_g, ln_b, n, t, seg):
    wc = g_in.shape[0]
    w = wc * LANES
    hist = pl.BlockSpec((1, D_CONV - 1, w), lambda b, i: (b, 0, 0))
    params = (conv0, conv_w, conv_b.reshape(1, w), ln_g.reshape(1, w), ln_b.reshape(1, w))
    param_specs = [hist, _const_spec((D_CONV, w)), _const_spec((1, w)), _const_spec((1, w)),
                   _const_spec((1, w))]
    hist_shape = jax.ShapeDtypeStruct((n, D_CONV - 1, w), F32)
    if seg == 1:
        tt = _row_tile(t, 256)
        cols = pl.BlockSpec((wc, 1, tt, LANES), lambda b, i: (0, b, i, 0))
        out, cd = pl.pallas_call(
            functools.partial(_conf_body, tt=tt, w=w),
            grid=(n, t // tt),
            in_specs=[cols] + param_specs,
            out_specs=[pl.BlockSpec((1, tt, w), lambda b, i: (b, i, 0)), hist],
            out_shape=[jax.ShapeDtypeStruct((n, t, w), BF16), hist_shape],
            scratch_shapes=[pltpu.VMEM((tt + 32, w), F32)],
            compiler_params=_cparams(2),
            name="conformer_conv",
        )(g_in.reshape(wc, n, t, LANES), *params)
        return out.reshape(n * t, w), cd
    assert seg >= D_CONV - 1
    tt = SUBL * seg
    cols = pl.BlockSpec((wc, 1, tt, LANES), lambda b, i: (0, b, i, 0))
    out, cd = pl.pallas_call(
        functools.partial(_conf_seg_body, tt=tt, w=w, rows_per_chunk=64),
        grid=(n, t // tt),
        in_specs=[cols] + param_specs,
        out_specs=[cols, hist],
        out_shape=[jax.ShapeDtypeStruct((wc, n, t, LANES), F32), hist_shape],
        scratch_shapes=[pltpu.VMEM(((seg + D_CONV - 1) * SUBL, w), F32),
                        pltpu.VMEM(((D_CONV - 1) * SUBL, w), F32), pltpu.VMEM((tt, w), F32)],
        compiler_params=_cparams(2),
        name="conformer_conv_seg",
    )(g_in.reshape(wc, n, t, LANES), *params)
    return out.reshape(wc, n * t, LANES), cd


def _run_group(x3, cache, conv_b0, h_b0, c_re0, c_im0, conv_d0, p, prepped):
    n, t, d = x3.shape
    depth = p['ffn1_g'].shape[0]
    x = x3.reshape(n * t, d)
    ks, vs, cbs, hbs, cres, cims, cds = [], [], [], [], [], [], []
    for l in range(depth):
        x = _ffn(x, p['ffn1_g'][l], prepped['ffn1'], l)
        if l % 2 == 0:
            e = l // 2
            hw = p['even_w_in'].shape[2] // 5
            heads = hw // A_V
            seg = SEG_BLOCK // SUBL if t % SEG_BLOCK == 0 else 1
            q, k, kb, v, vb, xb, gb = _inproj_even(x, p['mix_g'][l], prepped['even_w_in'][e],
                                                   p['a_q_g'][e], p['a_k_g'][e], seg)
            lam_init = 0.8 - 0.6 * math.exp(-0.3 * l)
            if cache is None:
                att = _attn_prompt(q, kb, vb, p['a_lambda'][e], p['a_head_g'][e], n, t, lam_init)
            else:
                cache_k, cache_v, page_table = cache
                att = _attn_sample(q, k, v, cache_k, cache_v, page_table, e, p['a_lambda'][e],
                                   p['a_head_g'][e], n, t, lam_init)
            rec, cb, hb = _rglru(xb, gb, conv_b0[e], h_b0[e], p['b_conv_w'][e], p['b_conv_b'][e],
                                 prepped['b_gate_w'][e], p['b_gate_b'][e], p['b_lambda'][e], n, t, seg)
            mixer, mixer_seg = (att, rec, prepped['even_w_out'][e]), seg
            ks.append(k.reshape(n, t, heads, 2 * A_QK))
            vs.append(v.reshape(n, t, heads, A_V))
            cbs.append(cb)
            hbs.append(hb.reshape(n, hw))
        else:
            o = l // 2
            seg = SEG_BLOCK // SUBL if t % SEG_BLOCK == 0 else 1
            u, g_in = _inproj_odd(x, p['mix_g'][l], prepped['odd_w_in'][o], seg)
            groups = u.shape[0] * LANES // C_GROUP
            c_out, cre, cim = _s5(u, c_re0[o], c_im0[o], p['c_a_re'][o], p['c_a_im'][o], p['c_log_dt'][o],
                                  prepped['s5'][o], p['c_d'][o], p['c_glu_w'][o], p['c_glu_b'][o], n, t)
            d_out, cd = _conf(g_in, conv_d0[o], p['d_conv_w'][o], p['d_conv_b'][o],
                              p['d_ln_g'][o], p['d_ln_b'][o], n, t, seg)
            mixer, mixer_seg = (c_out, d_out, prepped['odd_w_out'][o]), seg
            cres.append(cre.reshape(n, groups, C_STATE))
            cims.append(cim.reshape(n, groups, C_STATE))
            cds.append(cd)
        x = _ffn(x, p['ffn2_g'][l], prepped['ffn2'], l, mixer, mixer_seg)
    stack = lambda xs: xs[0][None] if len(xs) == 1 else jnp.stack(xs)
    return (x.reshape(n, t, d), stack(ks), stack(vs), stack(cbs), stack(hbs),
            stack(cres), stack(cims), stack(cds))


def kernel(x_prompt, x_sample, cache_k, cache_v, state_conv_b, state_h_b, state_c_re, state_c_im,
           state_conv_d, page_table, ffn1_g, ffn1_w_gu, ffn1_w_down, mix_g, ffn2_g, ffn2_w_gu,
           ffn2_w_down, even_w_in, even_w_out, a_q_g, a_k_g, a_lambda, a_head_g, b_conv_w, b_conv_b,
           b_gate_w, b_gate_b, b_lambda, odd_w_in, odd_w_out, c_a_re, c_a_im, c_log_dt, c_b_re, c_b_im,
           c_c_re, c_c_im, c_d, c_glu_w, c_glu_b, d_conv_w, d_conv_b, d_ln_g, d_ln_b):
    p = dict(ffn1_g=ffn1_g, mix_g=mix_g, ffn2_g=ffn2_g, even_w_in=even_w_in, a_q_g=a_q_g, a_k_g=a_k_g,
             a_lambda=a_lambda, a_head_g=a_head_g, b_conv_w=b_conv_w, b_conv_b=b_conv_b,
             b_gate_b=b_gate_b, b_lambda=b_lambda, c_a_re=c_a_re, c_a_im=c_a_im, c_log_dt=c_log_dt,
             c_d=c_d, c_glu_w=c_glu_w, c_glu_b=c_glu_b, d_conv_w=d_conv_w, d_conv_b=d_conv_b,
             d_ln_g=d_ln_g, d_ln_b=d_ln_b)
    depth = ffn1_g.shape[0]
    n_even, n_odd = even_w_in.shape[0], odd_w_in.shape[0]
    prepped = dict(
        ffn1=_prep_ffn(ffn1_w_gu, ffn1_w_down),
        ffn2=_prep_ffn(ffn2_w_gu, ffn2_w_down),
        even_w_in=[even_w_in[e].astype(BF16) for e in range(n_even)],
        even_w_out=[even_w_out[e].astype(BF16) for e in range(n_even)],
        odd_w_in=[odd_w_in[o].astype(BF16) for o in range(n_odd)],
        odd_w_out=[odd_w_out[o].astype(BF16) for o in range(n_odd)],
        b_gate_w=[_prep_gate_w(b_gate_w[e]) for e in range(n_even)],
        s5=[_prep_s5(c_b_re[o], c_b_im[o], c_c_re[o], c_c_im[o]) for o in range(n_odd)],
    )
    b = x_prompt.shape[0]
    dt = x_prompt.dtype
    hw = even_w_in.shape[2] // 5
    groups, states = state_c_re.shape[2], state_c_re.shape[3]
    prompt = _run_group(
        x_prompt, None,
        jnp.zeros((n_even, b, B_CONV - 1, hw), dt), jnp.zeros((n_even, b, hw), dt),
        jnp.zeros((n_odd, b, groups, states), dt), jnp.zeros((n_odd, b, groups, states), dt),
        jnp.zeros((n_odd, b, D_CONV - 1, hw), dt), p, prepped)
    sample = _run_group(
        x_sample, (cache_k, cache_v, page_table), state_conv_b, state_h_b, state_c_re, state_c_im,
        state_conv_d, p, prepped)
    return (prompt[0], sample[0]) + prompt[1:] + sample[1:]
```

```python
import functools
import math

import jax
import jax.numpy as jnp
from jax import lax
from jax.experimental import pallas as pl
from jax.experimental.pallas import tpu as pltpu

F32 = jnp.float32
BF16 = jnp.bfloat16
EPS = 1e-6
NEG = -1e30
LOG2E = 1.4426950408889634
ATTN_SCORE_BOUND = 64.0

LANES = 128
SUBL = 8
V7X_MXU_DIM = 256
V7X_VMEM_LIMIT = 56 * 1024 * 1024

A_HEADS = 4
A_QK = 64
A_V = 128
B_BLOCKS = 8
B_CONV = 4
C_RG = 8.0
C_GROUP = 16
C_STATE = 64
D_CONV = 31
SEG_BLOCK = 512
S5_SCAN_LANES = 1024
S5_SCAN_UNROLL = 8


def _cparams(n_axes):
    return pltpu.CompilerParams(dimension_semantics=("arbitrary",) * n_axes,
                                vmem_limit_bytes=V7X_VMEM_LIMIT)


def _const_spec(shape):
    nd = len(shape)
    return pl.BlockSpec(shape, lambda *_: (0,) * nd)


def _rms(x, g):
    return x * lax.rsqrt(jnp.mean(x * x, axis=-1, keepdims=True) + EPS) * g


def _gelu_tanh(x):
    return 0.5 * x * (1.0 + jnp.tanh(math.sqrt(2.0 / math.pi) * (x + 0.044715 * (x * x * x))))


def _dot(a, b):
    return jnp.dot(a, b, preferred_element_type=F32)


def _row_tile(m, want):
    t = min(m, want)
    while m % t:
        t //= 2
    return t


def _store_col_tiles(ref, val, seg):
    for c in range(val.shape[1] // LANES):
        tile = val[:, c * LANES:(c + 1) * LANES]
        if seg == 1:
            ref[c] = tile
        else:
            for s in range(SUBL):
                ref[c, pl.ds(s, seg, stride=SUBL), :] = tile[s * seg:(s + 1) * seg, :]


def _load_col_tiles(ref, seg):
    cols = []
    for c in range(ref.shape[0]):
        if seg == 1:
            cols.append(ref[c])
        else:
            cols.append(jnp.concatenate(
                [ref[c, pl.ds(s, seg, stride=SUBL), :] for s in range(SUBL)], axis=0))
    return jnp.concatenate(cols, axis=-1)


def _mixer_half(ref, seg):
    if len(ref.shape) == 2:
        return ref[...]
    return _load_col_tiles(ref, seg).astype(BF16)


def _ffn_body(x_ref, g_ref, wg_ref, wu_ref, wd_ref, *rest, ck, with_mixer, seg):
    x = x_ref[...]
    if with_mixer:
        a_ref, b_ref, wo_ref, o_ref = rest
        hw = wo_ref.shape[0] // 2
        x = (x + _dot(_mixer_half(a_ref, seg), wo_ref[:hw, :])
             + _dot(_mixer_half(b_ref, seg), wo_ref[hw:, :]))
    else:
        o_ref, = rest
    h = _rms(x, g_ref[...]).astype(BF16)
    acc = jnp.zeros(x.shape, F32)
    for c in range(wg_ref.shape[1] // ck):
        sl = slice(c * ck, (c + 1) * ck)
        gate = _dot(h, wg_ref[:, sl])
        up = _dot(h, wu_ref[:, sl])
        a = (gate * jax.nn.sigmoid(gate) * up).astype(BF16)
        acc = acc + _dot(a, wd_ref[sl, :])
    o_ref[...] = x + 0.5 * acc


def _split_gu_body(w_ref, g_ref, u_ref, *, f, fp):
    x = w_ref[0]
    lane = lax.broadcasted_iota(jnp.int32, (x.shape[0], fp), 1)
    g_ref[0] = jnp.where(lane < f, x[:, :fp], 0.0).astype(BF16)
    start = f // LANES * LANES
    up = pltpu.roll(x[:, start:start + fp], fp - (f - start), 1)
    u_ref[0] = jnp.where(lane < f, up, 0.0).astype(BF16)


def _pad_rows_body(w_ref, o_ref, *, f, rows):
    r = lax.broadcasted_iota(jnp.int32, w_ref.shape[1:], 0) + pl.program_id(1) * rows
    o_ref[0] = jnp.where(r < f, w_ref[0], 0.0).astype(BF16)


def _prep_ffn(w_gu, w_down):
    layers, d, f2 = w_gu.shape
    f = f2 // 2
    fp = -(-f // V7X_MXU_DIM) * V7X_MXU_DIM
    start = f // LANES * LANES
    if start + fp > f2 or d % V7X_MXU_DIM:
        pad_c = ((0, 0), (0, 0), (0, fp - f))
        return (jnp.pad(w_gu[..., :f], pad_c).astype(BF16), jnp.pad(w_gu[..., f:], pad_c).astype(BF16),
                jnp.pad(w_down, ((0, 0), (0, fp - f), (0, 0))).astype(BF16))
    rows = V7X_MXU_DIM
    half = pl.BlockSpec((1, rows, fp), lambda l, i: (l, i, 0))
    wg, wu = pl.pallas_call(
        functools.partial(_split_gu_body, f=f, fp=fp),
        grid=(layers, d // rows),
        in_specs=[pl.BlockSpec((1, rows, f2), lambda l, i: (l, i, 0))],
        out_specs=[half, half],
        out_shape=[jax.ShapeDtypeStruct((layers, d, fp), BF16)] * 2,
        compiler_params=_cparams(2),
        name="ffn_split_gate_up",
    )(w_gu)
    rows = fp // 2 if fp % (2 * SUBL) == 0 else rows
    blk = pl.BlockSpec((1, rows, d), lambda l, i: (l, i, 0))
    wd = pl.pallas_call(
        functools.partial(_pad_rows_body, f=f, rows=rows),
        grid=(layers, fp // rows),
        in_specs=[blk],
        out_specs=blk,
        out_shape=jax.ShapeDtypeStruct((layers, fp, d), BF16),
        compiler_params=_cparams(2),
        name="ffn_pad_down",
    )(w_down)
    return wg, wu, wd


def _ffn(x, g, w, layer, mixer=None, seg=1):
    wg, wu, wd = w
    m, d = x.shape
    fp = wg.shape[2]
    tm = SUBL * seg if seg > 1 else _row_tile(m, 512)
    row = pl.BlockSpec((tm, d), lambda i: (i, 0))
    of_layer = lambda r, c: pl.BlockSpec((None, r, c), lambda i: (layer, 0, 0))
    in_specs = [row, _const_spec((1, d)), of_layer(d, fp), of_layer(d, fp), of_layer(fp, d)]
    args = [x, g.reshape(1, d), wg, wu, wd]
    if mixer is not None:
        a, b, w_out = mixer
        hw = w_out.shape[0] // 2
        spec = lambda v: (pl.BlockSpec((tm, hw), lambda i: (i, 0)) if v.ndim == 2 else
                          pl.BlockSpec((v.shape[0], tm, LANES), lambda i: (0, i, 0)))
        in_specs += [spec(a), spec(b), _const_spec((2 * hw, d))]
        args += [a, b, w_out]
    return pl.pallas_call(
        functools.partial(_ffn_body, ck=V7X_MXU_DIM, with_mixer=mixer is not None, seg=seg),
        grid=(m // tm,),
        in_specs=in_specs,
        out_specs=row,
        out_shape=jax.ShapeDtypeStruct((m, d), F32),
        compiler_params=_cparams(1),
        name="ffn_mix" if mixer is not None else "ffn",
    )(*args)


def _inproj_even_body(x_ref, g_ref, w_ref, gq_ref, gk_ref, ones_ref,
                      q_ref, k_ref, kb_ref, v_ref, vb_ref, xb_ref, gb_ref, *, hw, tm, seg):
    x = x_ref[...]
    h = _rms(x, g_ref[...]).astype(BF16)
    proj = _dot(h, w_ref[...])

    def group_rms(t, gain):
        ss = (t * t).astype(BF16)
        gs = jnp.concatenate(
            [_dot(ss[:, c * V7X_MXU_DIM:(c + 1) * V7X_MXU_DIM], ones_ref[...])
             for c in range(hw // V7X_MXU_DIM)], axis=-1)
        return t * lax.rsqrt(gs * (1.0 / A_QK) + EPS) * gain

    q = group_rms(proj[:, 0:hw], gq_ref[...]) * (A_QK ** -0.5 * LOG2E)
    k = group_rms(proj[:, hw:2 * hw], gk_ref[...])
    v = proj[:, 2 * hw:3 * hw]
    q_ref[...] = q.astype(BF16)
    kb_ref[...] = k.astype(BF16)
    vb_ref[...] = v.astype(BF16)
    heads = hw // A_V
    for hd in range(heads):
        k_ref[pl.ds(hd, tm, stride=heads), :] = k[:, hd * A_V:(hd + 1) * A_V]
        v_ref[pl.ds(hd, tm, stride=heads), :] = v[:, hd * A_V:(hd + 1) * A_V]
    _store_col_tiles(xb_ref, proj[:, 3 * hw:4 * hw], seg)
    _store_col_tiles(gb_ref, proj[:, 4 * hw:5 * hw], seg)


def _inproj_even(x, g, w_in, gq, gk, seg):
    m, d = x.shape
    hw = w_in.shape[1] // 5
    heads = hw // A_V
    tm = SUBL * seg if seg > 1 else _row_tile(m, 512)
    row = pl.BlockSpec((tm, d), lambda i: (i, 0))
    orow = pl.BlockSpec((tm, hw), lambda i: (i, 0))
    hrow = pl.BlockSpec((tm * heads, A_V), lambda i: (i, 0))
    cols = pl.BlockSpec((hw // LANES, tm, LANES), lambda i: (0, i, 0))
    idx = jnp.arange(V7X_MXU_DIM) // A_QK
    ones_bd = (idx[:, None] == idx[None, :]).astype(BF16)
    tile = lambda t: jnp.tile(t, hw // A_QK).reshape(1, hw)
    sds = lambda dt: jax.ShapeDtypeStruct((m, hw), dt)
    hsds = jax.ShapeDtypeStruct((m * heads, A_V), F32)
    csds = jax.ShapeDtypeStruct((hw // LANES, m, LANES), F32)
    return pl.pallas_call(
        functools.partial(_inproj_even_body, hw=hw, tm=tm, seg=seg),
        grid=(m // tm,),
        in_specs=[row, _const_spec((1, d)), _const_spec(w_in.shape), _const_spec((1, hw)),
                  _const_spec((1, hw)), _const_spec(ones_bd.shape)],
        out_specs=[orow, hrow, orow, hrow, orow, cols, cols],
        out_shape=[sds(BF16), hsds, sds(BF16), hsds, sds(BF16), csds, csds],
        compiler_params=_cparams(1),
        name="inproj_even",
    )(x, g.reshape(1, d), w_in, tile(gq), tile(gk), ones_bd)


def _diff_lambda(lv, lam_init):
    s01 = jnp.sum(lv[0:1, :] * lv[1:2, :], axis=-1, keepdims=True)
    s23 = jnp.sum(lv[2:3, :] * lv[3:4, :], axis=-1, keepdims=True)
    return jnp.exp(s01) - jnp.exp(s23) + lam_init


def _split_maps(q):
    lane = lax.broadcasted_iota(jnp.int32, q.shape, 1)
    zero = jnp.zeros_like(q)
    return jnp.where(lane < A_QK, q, zero), jnp.where(lane >= A_QK, q, zero)


def _scores(q, k):
    return lax.dot_general(q, k, (((1,), (1,)), ((), ())), preferred_element_type=F32)


def _attn_prompt_body(lam_ref, q_ref, k_ref, v_ref, hg_ref, o_ref, vext_sc, kn2_sc,
                      *, tq, tk, hps, lam_init):
    qi = pl.program_id(2)

    def max_row_norm2(x):
        xf = x.astype(F32)
        per_head = [jnp.sum(xf[:, hh * A_V:(hh + 1) * A_V] ** 2, axis=-1, keepdims=True)
                    for hh in range(hps)]
        return jnp.max(functools.reduce(jnp.maximum, per_head), axis=0, keepdims=True)

    @pl.when(qi == 0)
    def _():
        for hh in range(hps):
            vext_sc[:, 2 * hh * A_V:(2 * hh + 1) * A_V] = v_ref[:, hh * A_V:(hh + 1) * A_V]
            vext_sc[:, (2 * hh + 1) * A_V:(2 * hh + 2) * A_V] = jnp.ones((v_ref.shape[0], A_V), BF16)
        kn2_sc[...] = max_row_norm2(k_ref[...])

    lam = _diff_lambda(lam_ref[...], lam_init)
    q = q_ref[...]
    qs = [m for hh in range(hps) for m in _split_maps(q[:, hh * A_V:(hh + 1) * A_V])]

    per_q = tq // tk

    def chain_inputs(kb, c):
        rows = pl.ds(pl.multiple_of(kb * tk, tk), tk)
        hh = c // 2
        s = _scores(qs[c], k_ref[rows, hh * A_V:(hh + 1) * A_V])
        return s, vext_sc[rows, 2 * hh * A_V:(2 * hh + 2) * A_V]

    def causal(shape, d):
        col = lax.broadcasted_iota(jnp.int32, shape, 1)
        return col + d * tk <= lax.broadcasted_iota(jnp.int32, shape, 0)

    def finish(accs):
        for hh in range(hps):
            a0, a1 = accs[2 * hh], accs[2 * hh + 1]
            o = a0[:, :A_V] / a0[:, A_V:] - lam * (a1[:, :A_V] / a1[:, A_V:])
            o_ref[:, hh * A_V:(hh + 1) * A_V] = (
                _rms(o, hg_ref[...]) * (1.0 - lam_init)).astype(o_ref.dtype)

    def bounded():
        def step(kb, accs, diag):
            new = []
            for c in range(2 * hps):
                s, v = chain_inputs(kb, c)
                p = jnp.exp2(s)
                if diag is not None:
                    p = jnp.where(causal(p.shape, diag), p, 0.0)
                new.append(accs[c] + _dot(p.astype(BF16), v))
            return tuple(new)

        def diag_halves(accs):
            hk = tk // 2
            base = qi * tk
            new = []
            for c in range(2 * hps):
                hh = c // 2
                kcol = slice(hh * A_V, (hh + 1) * A_V)
                vcol = slice(2 * hh * A_V, (2 * hh + 2) * A_V)
                lo = pl.ds(pl.multiple_of(base, hk), hk)
                hi = pl.ds(pl.multiple_of(base + hk, hk), hk)
                p_lo = jnp.exp2(_scores(qs[c], k_ref[lo, kcol]))
                p_lo = jnp.where(causal(p_lo.shape, 0), p_lo, 0.0)
                p_hi = jnp.exp2(_scores(qs[c][hk:, :], k_ref[hi, kcol]))
                p_hi = jnp.where(causal(p_hi.shape, 0), p_hi, 0.0)
                pv_hi = _dot(p_hi.astype(BF16), vext_sc[hi, vcol])
                pv = _dot(p_lo.astype(BF16), vext_sc[lo, vcol])
                new.append(accs[c] + pv + jnp.concatenate([jnp.zeros_like(pv_hi), pv_hi], axis=0))
            return new

        zero = jnp.zeros((tq, 2 * A_V), F32)
        accs = lax.fori_loop(0, qi * per_q, lambda kb, a: step(kb, a, None), (zero,) * (2 * hps))
        if per_q == 1 and tk % (2 * V7X_MXU_DIM) == 0:
            accs = diag_halves(accs)
        else:
            for d in range(per_q):
                accs = step(qi * per_q + d, accs, d)
        finish(accs)

    def general():
        def step(kb, carry, diag):
            new = []
            for c in range(2 * hps):
                m, acc = carry[c]
                s, v = chain_inputs(kb, c)
                if diag is not None:
                    s = jnp.where(causal(s.shape, diag), s, NEG)
                m_new = jnp.maximum(m, jnp.max(s, axis=-1, keepdims=True))
                p = jnp.exp2(s - m_new)
                new.append((m_new, jnp.exp2(m - m_new) * acc + _dot(p.astype(BF16), v)))
            return tuple(new)

        init_one = (jnp.full((tq, 1), NEG, F32), jnp.zeros((tq, 2 * A_V), F32))
        carry = lax.fori_loop(0, qi * per_q, lambda kb, c: step(kb, c, None), (init_one,) * (2 * hps))
        for d in range(per_q):
            carry = step(qi * per_q + d, carry, d)
        finish([acc for _, acc in carry])

    bound2 = max_row_norm2(q) * kn2_sc[...]
    lax.cond(bound2[0, 0] <= ATTN_SCORE_BOUND ** 2, bounded, general)


def _attn_prompt(q, k, v, lam_p, head_g, n, t, lam_init):
    hw = q.shape[1]
    heads = hw // A_V
    hps = next(c for c in (4, 2, 1) if heads % c == 0)
    tk = _row_tile(t, 512)
    tq = tk
    nq = t // tq
    qspec = pl.BlockSpec((tq, hps * A_V), lambda b, h, i: (b * nq + i, h))
    kspec = pl.BlockSpec((t, hps * A_V), lambda b, h, i: (b, h))
    return pl.pallas_call(
        functools.partial(_attn_prompt_body, tq=tq, tk=tk, hps=hps, lam_init=lam_init),
        grid=(n, heads // hps, nq),
        in_specs=[_const_spec(lam_p.shape), qspec, kspec, kspec, _const_spec((1, A_V))],
        out_specs=qspec,
        out_shape=jax.ShapeDtypeStruct((n * t, hw), BF16),
        scratch_shapes=[pltpu.VMEM((t, hps * 2 * A_V), BF16), pltpu.VMEM((1, 1), F32)],
        compiler_params=_cparams(3),
        name="attn_prompt",
    )(lam_p, q, k, v, head_g.reshape(1, A_V))


def _attn_sample_body(pt_ref, lam_ref, q_ref, kn_ref, vn_ref, hg_ref, ck_ref, cv_ref, o_ref,
                      kbuf, vbuf, sem, bias_sc, m_sc, l_sc, acc_sc,
                      *, pps, prow, base, heads, t, lam_init):
    seq, step = pl.program_id(0), pl.program_id(1)
    n_seq, n_step = pl.num_programs(0), pl.num_programs(1)
    flat = seq * n_step + step
    slot = flat % 2

    def page_copies(b, i, s):
        cps = []
        for r in range(pps):
            src = pl.ds(pl.multiple_of((base + pt_ref[b, i * pps + r]) * prow, prow), prow)
            dst = pl.ds(r * prow, prow)
            cps.append(pltpu.make_async_copy(ck_ref.at[src, :], kbuf.at[s, dst, :], sem.at[s, 0]))
            cps.append(pltpu.make_async_copy(cv_ref.at[src, :], vbuf.at[s, dst, :], sem.at[s, 1]))
        return cps

    @pl.when(flat == 0)
    def _():
        for cp in page_copies(seq, step, slot):
            cp.start()

    last = flat == n_seq * n_step - 1
    wrap = step == n_step - 1
    nb = jnp.where(last, seq, jnp.where(wrap, seq + 1, seq))
    ni = jnp.where(last, step, jnp.where(wrap, 0, step + 1))
    for cp in page_copies(nb, ni, 1 - slot):
        cp.start()
    for cp in page_copies(seq, step, slot):
        cp.wait()

    qrows = 2 * t
    q = q_ref[0]
    qall = jnp.concatenate(
        [piece for h in range(heads) for piece in _split_maps(q[:, h * A_V:(h + 1) * A_V])], axis=0)

    def head_match(shape):
        row = lax.broadcasted_iota(jnp.int32, shape, 0)
        col = lax.broadcasted_iota(jnp.int32, shape, 1)
        return row, col, (col % heads) == (row // qrows)

    def update(s, v):
        m = m_sc[...]
        m_new = jnp.maximum(m, jnp.max(s, axis=-1, keepdims=True))
        alpha = jnp.exp2(m - m_new)
        p = jnp.exp2(s - m_new)
        l_sc[...] = alpha * l_sc[...] + jnp.sum(p, axis=-1, keepdims=True)
        acc_sc[...] = alpha * acc_sc[...] + _dot(p.astype(BF16), v)
        m_sc[...] = m_new

    @pl.when((seq == 0) & (step == 0))
    def _():
        _, _, ok = head_match(bias_sc.shape)
        bias_sc[...] = jnp.where(ok, 0.0, NEG)

    @pl.when(step == 0)
    def _():
        m_sc[...] = jnp.full(m_sc.shape, NEG, F32)
        l_sc[...] = jnp.zeros(l_sc.shape, F32)
        acc_sc[...] = jnp.zeros(acc_sc.shape, F32)
        s = _scores(qall, kn_ref[0])
        row, col, ok = head_match(s.shape)
        ok = ok & ((col // heads) <= (row % t))
        update(jnp.where(ok, s, NEG), vn_ref[0])

    update(_scores(qall, kbuf[slot].astype(BF16)) + bias_sc[...], vbuf[slot].astype(BF16))

    @pl.when(last)
    def _():
        for cp in page_copies(seq, step, 1 - slot):
            cp.wait()

    @pl.when(step == n_step - 1)
    def _():
        lam = _diff_lambda(lam_ref[...], lam_init)
        o = acc_sc[...] / l_sc[...]
        for h in range(heads):
            oh = o[h * qrows:h * qrows + t, :] - lam * o[h * qrows + t:(h + 1) * qrows, :]
            o_ref[0, :, h * A_V:(h + 1) * A_V] = (
                _rms(oh, hg_ref[...]) * (1.0 - lam_init)).astype(o_ref.dtype)


def _attn_sample(q, k_new, v_new, cache_k, cache_v, page_table, layer, lam_p, head_g, n, t, lam_init):
    hw = q.shape[1]
    heads = hw // A_V
    n_layers, n_pool, page = cache_k.shape[:3]
    n_pages = page_table.shape[1]
    prow = page * heads
    pps = 16
    while n_pages % pps:
        pps //= 2
    ck = cache_k.reshape(n_layers * n_pool * prow, A_V)
    cv = cache_v.reshape(n_layers * n_pool * prow, A_V)
    pad = ((0, 0), (0, prow - t * heads), (0, 0))
    kn = jnp.pad(k_new.astype(BF16).reshape(n, t * heads, A_V), pad)
    vn = jnp.pad(v_new.astype(BF16).reshape(n, t * heads, A_V), pad)
    qr = heads * 2 * t
    grid_spec = pltpu.PrefetchScalarGridSpec(
        num_scalar_prefetch=1,
        grid=(n, n_pages // pps),
        in_specs=[pl.BlockSpec(lam_p.shape, lambda b, i, pt: (0, 0)),
                  pl.BlockSpec((1, t, hw), lambda b, i, pt: (b, 0, 0)),
                  pl.BlockSpec((1, prow, A_V), lambda b, i, pt: (b, 0, 0)),
                  pl.BlockSpec((1, prow, A_V), lambda b, i, pt: (b, 0, 0)),
                  pl.BlockSpec((1, A_V), lambda b, i, pt: (0, 0)),
                  pl.BlockSpec(memory_space=pl.ANY), pl.BlockSpec(memory_space=pl.ANY)],
        out_specs=pl.BlockSpec((1, t, hw), lambda b, i, pt: (b, 0, 0)),
        scratch_shapes=[pltpu.VMEM((2, pps * prow, A_V), F32), pltpu.VMEM((2, pps * prow, A_V), F32),
                        pltpu.SemaphoreType.DMA((2, 2)),
                        pltpu.VMEM((qr, pps * prow), F32),
                        pltpu.VMEM((qr, 1), F32), pltpu.VMEM((qr, 1), F32), pltpu.VMEM((qr, A_V), F32)],
    )
    out = pl.pallas_call(
        functools.partial(_attn_sample_body, pps=pps, prow=prow, base=layer * n_pool, heads=heads, t=t,
                          lam_init=lam_init),
        grid_spec=grid_spec,
        out_shape=jax.ShapeDtypeStruct((n, t, hw), BF16),
        compiler_params=_cparams(2),
        name="attn_sample",
    )(page_table, lam_p, q.reshape(n, t, hw), kn, vn, head_g.reshape(1, A_V), ck, cv)
    return out.reshape(n * t, hw)


def _shift_rows(x, d, fill):
    row = lax.broadcasted_iota(jnp.int32, x.shape, 0)
    return jnp.where(row >= d, pltpu.roll(x, d, 0), fill)


def _linear_scan(a, b):
    d = 1
    while d < a.shape[0]:
        b = b + a * _shift_rows(b, d, 0.0)
        a = a * _shift_rows(a, d, 1.0)
        d *= 2
    return a, b


def _rglru_body(xb_ref, gb_ref, conv0_ref, h0_ref, cw_ref, cb_ref, wg_ref, gbias_ref, lam_ref,
                rec_ref, convn_ref, hl_ref, ext_sc, h_sc, *, tt, w):
    halo = 8
    step = pl.program_id(1)

    @pl.when(step == 0)
    def _():
        ext_sc[0:halo, :] = jnp.zeros((halo, w), F32)
        ext_sc[halo - (B_CONV - 1):halo, :] = conv0_ref[0]
        h_sc[...] = h0_ref[0]

    x = jnp.concatenate([xb_ref[c, 0] for c in range(w // LANES)], axis=-1)
    gb = jnp.concatenate([gb_ref[c, 0] for c in range(w // LANES)], axis=-1)
    ext_sc[halo:halo + tt, :] = x
    cw = cw_ref[...]
    xc = cb_ref[...] + x * cw[B_CONV - 1:B_CONV, :]
    for j in range(1, B_CONV):
        xc = xc + ext_sc[halo - j:halo - j + tt, :] * cw[B_CONV - 1 - j:B_CONV - j, :]

    a, b = _rglru_coeffs(xc, wg_ref[...], gbias_ref[...], lam_ref[...], w)
    a_cum, h_loc = _linear_scan(a, b)
    h = h_loc + a_cum * h_sc[...]
    rec_ref[0] = (_gelu_tanh(gb) * h).astype(rec_ref.dtype)

    h_sc[...] = h[tt - 1:tt, :]
    hl_ref[0] = h[tt - 1:tt, :]
    convn_ref[0] = ext_sc[halo + tt - (B_CONV - 1):halo + tt, :]
    ext_sc[0:halo, :] = ext_sc[tt:tt + halo, :]


def _rglru_coeffs(xc, gate_w, gate_b, lam, w):
    pre = _dot(xc.astype(BF16), gate_w) + gate_b
    r = jax.nn.sigmoid(pre[:, :w])
    i = jax.nn.sigmoid(pre[:, w:])
    softplus_neg = jnp.maximum(-lam, 0.0) + jnp.log1p(jnp.exp(-jnp.abs(lam)))
    a = jnp.exp((-C_RG) * r * softplus_neg)
    return a, jnp.sqrt(1.0 - a * a) * (i * xc)


def _rglru_seg_body(xb_ref, gb_ref, conv0_ref, h0_ref, cw_ref, cb_ref, wg_ref, gbias_ref, lam_ref,
                    rec_ref, convn_ref, hl_ref, prev_sc, h_sc, *, tt, w):
    seg = tt // SUBL
    hist = B_CONV - 1
    step = pl.program_id(1)

    @pl.when(step == 0)
    def _():
        c0 = conv0_ref[0]
        for i in range(hist):
            prev_sc[i * SUBL:(i + 1) * SUBL, :] = jnp.broadcast_to(c0[i:i + 1, :], (SUBL, w))
        h_sc[...] = h0_ref[0]

    cur = jnp.concatenate([xb_ref[c, 0] for c in range(w // LANES)], axis=-1)
    gb = jnp.concatenate([gb_ref[c, 0] for c in range(w // LANES)], axis=-1)
    row = lax.broadcasted_iota(jnp.int32, (SUBL, w), 0)
    tail = cur[(seg - hist) * SUBL:, :]
    halo = [pltpu.roll(jnp.where(row == SUBL - 1, prev_sc[i * SUBL:(i + 1) * SUBL, :],
                                 tail[i * SUBL:(i + 1) * SUBL, :]), 1, 0) for i in range(hist)]
    ext = jnp.concatenate(halo + [cur], axis=0)
    prev_sc[...] = tail
    convn_ref[0] = jnp.concatenate(
        [tail[i * SUBL + SUBL - 1:(i + 1) * SUBL, :] for i in range(hist)], axis=0)

    cw = cw_ref[...]
    xc = cb_ref[...] + jnp.zeros((tt, w), F32)
    for k in range(B_CONV):
        xc = xc + ext[k * SUBL:k * SUBL + tt, :] * cw[k:k + 1, :]
    a, b = _rglru_coeffs(xc, wg_ref[...], gbias_ref[...], lam_ref[...], w)

    grp = lambda v, j: v[j * SUBL:(j + 1) * SUBL, :]
    e, prod = grp(b, 0), grp(a, 0)
    for j in range(1, seg):
        e, prod = grp(a, j) * e + grp(b, j), grp(a, j) * prod
    c = h_sc[...]
    h_in = jnp.zeros((SUBL, w), F32)
    for k in range(SUBL):
        h_in = jnp.where(row == k, c, h_in)
        c = prod[k:k + 1, :] * c + e[k:k + 1, :]
    h_sc[...] = c
    hl_ref[0] = c
    hs, h = [], h_in
    for j in range(seg):
        h = grp(a, j) * h + grp(b, j)
        hs.append(h)
    rec = _gelu_tanh(gb) * jnp.concatenate(hs, axis=0)
    for cc in range(w // LANES):
        rec_ref[cc, 0] = rec[:, cc * LANES:(cc + 1) * LANES]


def _prep_gate_w(gate_w):
    _, nb, bs, _ = gate_w.shape
    eye = jnp.eye(nb, dtype=gate_w.dtype)
    dense = jnp.einsum('gbij,bc->gbicj', gate_w, eye).reshape(2, nb * bs, nb * bs)
    return jnp.concatenate([dense[0], dense[1]], axis=1).astype(BF16)


def _rglru(xb, gb, conv0, h0, conv_w, conv_b, gate_wd, gate_b, lam_p, n, t, seg):
    wc = xb.shape[0]
    w = wc * LANES
    per_n = lambda rows: pl.BlockSpec((1, rows, w), lambda b, i: (b, 0, 0))
    tt = SUBL * seg if seg > 1 else _row_tile(t, 256)
    cols = pl.BlockSpec((wc, 1, tt, LANES), lambda b, i: (0, b, i, 0))
    in_specs = [cols, cols, per_n(B_CONV - 1), per_n(1), _const_spec((B_CONV, w)), _const_spec((1, w)),
                _const_spec((w, 2 * w)), _const_spec((1, 2 * w)), _const_spec((1, w))]
    args = (xb.reshape(wc, n, t, LANES), gb.reshape(wc, n, t, LANES), conv0, h0.reshape(n, 1, w), conv_w,
            conv_b.reshape(1, w), gate_wd, gate_b.reshape(1, 2 * w), lam_p.reshape(1, w))
    state_shapes = [jax.ShapeDtypeStruct((n, B_CONV - 1, w), F32), jax.ShapeDtypeStruct((n, 1, w), F32)]
    if seg == 1:
        rec, cb, hb = pl.pallas_call(
            functools.partial(_rglru_body, tt=tt, w=w),
            grid=(n, t // tt),
            in_specs=in_specs,
            out_specs=[pl.BlockSpec((1, tt, w), lambda b, i: (b, i, 0)), per_n(B_CONV - 1), per_n(1)],
            out_shape=[jax.ShapeDtypeStruct((n, t, w), BF16)] + state_shapes,
            scratch_shapes=[pltpu.VMEM((tt + 8, w), F32), pltpu.VMEM((1, w), F32)],
            compiler_params=_cparams(2),
            name="rglru",
        )(*args)
        return rec.reshape(n * t, w), cb, hb
    assert seg >= B_CONV - 1
    rec, cb, hb = pl.pallas_call(
        functools.partial(_rglru_seg_body, tt=tt, w=w),
        grid=(n, t // tt),
        in_specs=in_specs,
        out_specs=[cols, per_n(B_CONV - 1), per_n(1)],
        out_shape=[jax.ShapeDtypeStruct((wc, n, t, LANES), F32)] + state_shapes,
        scratch_shapes=[pltpu.VMEM(((B_CONV - 1) * SUBL, w), F32), pltpu.VMEM((1, w), F32)],
        compiler_params=_cparams(2),
        name="rglru_seg",
    )(*args)
    return rec.reshape(wc, n * t, LANES), cb, hb


def _inproj_odd_body(x_ref, g_ref, w_ref, u_ref, gin_ref, *, hw, seg):
    h = _rms(x_ref[...], g_ref[...]).astype(BF16)
    proj = _dot(h, w_ref[...])
    _store_col_tiles(u_ref, proj[:, :hw], seg)
    _store_col_tiles(gin_ref, proj[:, hw:2 * hw] * jax.nn.sigmoid(proj[:, 2 * hw:]), seg)


def _inproj_odd(x, g, w_in, seg):
    m, d = x.shape
    hw = w_in.shape[1] // 3
    tm = SUBL * seg if seg > 1 else _row_tile(m, 512)
    row = pl.BlockSpec((tm, d), lambda i: (i, 0))
    cols = pl.BlockSpec((hw // LANES, tm, LANES), lambda i: (0, i, 0))
    return pl.pallas_call(
        functools.partial(_inproj_odd_body, hw=hw, seg=seg),
        grid=(m // tm,),
        in_specs=[row, _const_spec((1, d)), _const_spec(w_in.shape)],
        out_specs=[cols, cols],
        out_shape=[jax.ShapeDtypeStruct((hw // LANES, m, LANES), F32)] * 2,
        compiler_params=_cparams(1),
        name="inproj_odd",
    )(x, g.reshape(1, d), w_in)


def _cmul(ar, ai, br, bi):
    return ar * br - ai * bi, ar * bi + ai * br


def _s5_discretise(a_re, a_im, log_dt):
    dt = jnp.exp(log_dt)
    mag = jnp.exp(dt * a_re)
    ab_re = mag * jnp.cos(dt * a_im)
    ab_im = mag * jnp.sin(dt * a_im)
    den = a_re * a_re + a_im * a_im
    nr = ab_re - 1.0
    return ab_re, ab_im, (nr * a_re + ab_im * a_im) / den, (ab_im * a_re - nr * a_im) / den


def _s5_body(u_ref, re0_ref, im0_ref, are_ref, aim_ref, ldt_ref,
             bre_ref, bim_ref, cre_ref, cim_ref, d_ref, gw_ref, gbias_ref, o_ref, sre_ref, sim_ref,
             st_re, st_im, bf_re, bf_im, lam_re, lam_im, lseg_re, lseg_im, v_re, v_im,
             *, tt, w, sw):
    seg = tt // SUBL

    def step_rows(j):
        return pl.ds(pl.multiple_of(j * SUBL, SUBL), SUBL)

    @pl.when((pl.program_id(0) == 0) & (pl.program_id(1) == 0))
    def _():
        ab_re, ab_im, coef_re, coef_im = _s5_discretise(are_ref[...], aim_ref[...], ldt_ref[...])
        f_re, f_im = _cmul(coef_re, coef_im, bre_ref[...], bim_ref[...])
        bf_re[...] = f_re.astype(BF16)
        bf_im[...] = f_im.astype(BF16)
        lam_re[...] = jnp.broadcast_to(ab_re, lam_re.shape)
        lam_im[...] = jnp.broadcast_to(ab_im, lam_im.shape)
        pr, pi = ab_re, ab_im
        for _ in range(seg.bit_length() - 1):
            pr, pi = _cmul(pr, pi, pr, pi)
        lseg_re[...] = pr
        lseg_im[...] = pi

    @pl.when(pl.program_id(1) == 0)
    def _():
        st_re[...] = re0_ref[0]
        st_im[...] = im0_ref[0]

    u = jnp.concatenate([u_ref[c, 0] for c in range(w // LANES)], axis=-1)
    ub = u.astype(BF16)
    gpt = V7X_MXU_DIM // C_STATE
    kpt = V7X_MXU_DIM // C_GROUP
    for nt in range(sw // V7X_MXU_DIM):
        kt = (nt * gpt) // kpt
        lhs = ub[:, kt * V7X_MXU_DIM:(kt + 1) * V7X_MXU_DIM]
        rows = slice(kt * V7X_MXU_DIM, (kt + 1) * V7X_MXU_DIM)
        cols = slice(nt * V7X_MXU_DIM, (nt + 1) * V7X_MXU_DIM)
        v_re[:, cols] = _dot(lhs, bf_re[rows, cols])
        v_im[:, cols] = _dot(lhs, bf_im[rows, cols])

    for ch in range(sw // S5_SCAN_LANES):
        lanes = slice(ch * S5_SCAN_LANES, (ch + 1) * S5_SCAN_LANES)
        l_re, l_im = lam_re[:, lanes], lam_im[:, lanes]

        def advance(j, sr, si, lanes=lanes, l_re=l_re, l_im=l_im):
            pr, pi = _cmul(l_re, l_im, sr, si)
            return pr + v_re[step_rows(j), lanes], pi + v_im[step_rows(j), lanes]

        zero = jnp.zeros((SUBL, S5_SCAN_LANES), F32)
        e_re, e_im = lax.fori_loop(0, seg, lambda j, c, adv=advance: adv(j, *c), (zero, zero),
                                   unroll=min(seg, S5_SCAN_UNROLL))

        row = lax.broadcasted_iota(jnp.int32, zero.shape, 0)
        c_re, c_im = st_re[:, lanes], st_im[:, lanes]
        in_re, in_im = zero, zero
        for k in range(SUBL):
            in_re = jnp.where(row == k, c_re, in_re)
            in_im = jnp.where(row == k, c_im, in_im)
            p_re, p_im = _cmul(lseg_re[:, lanes], lseg_im[:, lanes], c_re, c_im)
            c_re, c_im = p_re + e_re[k:k + 1, :], p_im + e_im[k:k + 1, :]
        st_re[:, lanes] = c_re
        st_im[:, lanes] = c_im
        sre_ref[0, :, lanes] = c_re
        sim_ref[0, :, lanes] = c_im

        def emit(j, c, lanes=lanes, adv=advance):
            sr, si = adv(j, *c)
            v_re[step_rows(j), lanes] = sr
            v_im[step_rows(j), lanes] = si
            return sr, si

        lax.fori_loop(0, seg, emit, (in_re, in_im), unroll=min(seg, S5_SCAN_UNROLL))

    ys = []
    for ot in range(w // V7X_MXU_DIM):
        acc = None
        for nt in range(ot * kpt // gpt, (ot + 1) * kpt // gpt):
            rows = slice(nt * V7X_MXU_DIM, (nt + 1) * V7X_MXU_DIM)
            cols = slice(ot * V7X_MXU_DIM, (ot + 1) * V7X_MXU_DIM)
            part = (_dot(v_re[:, rows].astype(BF16), cre_ref[rows, cols])
                    - _dot(v_im[:, rows].astype(BF16), cim_ref[rows, cols]))
            acc = part if acc is None else acc + part
        ys.append(acc)
    y = jnp.concatenate(ys, axis=-1) + d_ref[...] * u
    z = _gelu_tanh(y)
    res = z * jax.nn.sigmoid(_dot(z.astype(BF16), gw_ref[...]) + gbias_ref[...])
    for c in range(w // LANES):
        o_ref[c, 0] = res[:, c * LANES:(c + 1) * LANES]


def _prep_s5(b_re, b_im, c_re, c_im):
    g = b_re.shape[0]
    eye = jnp.eye(g, dtype=b_re.dtype)
    bd_in = lambda b: jnp.einsum('gpc,gh->gchp', b, eye).reshape(g * C_GROUP, g * C_STATE)
    bd_out = lambda c: jnp.einsum('gcp,gh->gphc', c, eye).reshape(g * C_STATE, g * C_GROUP).astype(BF16)
    return bd_in(b_re), bd_in(b_im), bd_out(c_re), bd_out(c_im)


def _s5(u, re0, im0, a_re, a_im, log_dt, mats, d_skip, glu_w, glu_b, n, t):
    w = u.shape[0] * LANES
    groups = w // C_GROUP
    sw = groups * C_STATE
    tt = _row_tile(t, 512)
    bre, bim, cre, cim = mats
    wc = w // LANES
    seq = pl.BlockSpec((wc, 1, tt, LANES), lambda b, i: (0, b, i, 0))
    st = pl.BlockSpec((1, 1, sw), lambda b, i: (b, 0, 0))
    vm = lambda shape, dt: pltpu.VMEM(shape, dt)
    flat = lambda a: a.reshape(1, sw)
    out, s_re, s_im = pl.pallas_call(
        functools.partial(_s5_body, tt=tt, w=w, sw=sw),
        grid=(n, t // tt),
        in_specs=[seq, st, st] + [_const_spec((1, sw))] * 3
                 + [_const_spec((w, sw)), _const_spec((w, sw)), _const_spec((sw, w)), _const_spec((sw, w)),
                    _const_spec((1, w)), _const_spec((w, w)), _const_spec((1, w))],
        out_specs=[seq, st, st],
        out_shape=[jax.ShapeDtypeStruct((wc, n, t, LANES), F32),
                   jax.ShapeDtypeStruct((n, 1, sw), F32), jax.ShapeDtypeStruct((n, 1, sw), F32)],
        scratch_shapes=[vm((1, sw), F32), vm((1, sw), F32), vm((w, sw), BF16), vm((w, sw), BF16),
                        vm((SUBL, sw), F32), vm((SUBL, sw), F32), vm((1, sw), F32), vm((1, sw), F32),
                        vm((tt, sw), F32), vm((tt, sw), F32)],
        compiler_params=_cparams(2),
        name="s5",
    )(u.reshape(wc, n, t, LANES), re0.reshape(n, 1, sw), im0.reshape(n, 1, sw),
      flat(a_re), flat(a_im), flat(jnp.repeat(log_dt, C_STATE)), bre, bim, cre, cim,
      d_skip.reshape(1, w), glu_w.astype(BF16), glu_b.reshape(1, w))
    return out.reshape(wc, n * t, LANES), s_re, s_im


def _conf_body(g_ref, conv0_ref, cw_ref, cb_ref, lng_ref, lnb_ref, o_ref, convn_ref, ext_sc, *, tt, w):
    halo = 32
    hist = D_CONV - 1
    step = pl.program_id(1)

    @pl.when(step == 0)
    def _():
        ext_sc[0:halo, :] = jnp.zeros((halo, w), F32)
        ext_sc[halo - hist:halo, :] = conv0_ref[0]

    ext_sc[halo:halo + tt, :] = jnp.concatenate([g_ref[c, 0] for c in range(w // LANES)], axis=-1)
    cw = cw_ref[...]
    c = cb_ref[...] + jnp.zeros((tt, w), F32)
    for j in range(D_CONV):
        c = c + ext_sc[halo - hist + j:halo - hist + j + tt, :] * cw[j:j + 1, :]
    o_ref[0] = _ln_silu(c, lng_ref[...], lnb_ref[...]).astype(o_ref.dtype)
    convn_ref[0] = ext_sc[halo + tt - hist:halo + tt, :]
    ext_sc[0:halo, :] = ext_sc[tt:tt + halo, :]


def _ln_silu(c, g, b):
    cc = c - jnp.mean(c, axis=-1, keepdims=True)
    y = cc * lax.rsqrt(jnp.mean(cc * cc, axis=-1, keepdims=True) + EPS) * g + b
    return y * jax.nn.sigmoid(y)


def _conf_seg_body(g_ref, conv0_ref, cw_ref, cb_ref, lng_ref, lnb_ref, o_ref, convn_ref,
                   ext_sc, prev_sc, conv_sc, *, tt, w, rows_per_chunk):
    seg = tt // SUBL
    hist = D_CONV - 1
    step = pl.program_id(1)

    @pl.when(step == 0)
    def _():
        c0 = conv0_ref[0]
        for i in range(hist):
            prev_sc[i * SUBL:(i + 1) * SUBL, :] = jnp.broadcast_to(c0[i:i + 1, :], (SUBL, w))

    cur = jnp.concatenate([g_ref[c, 0] for c in range(w // LANES)], axis=-1)
    ext_sc[hist * SUBL:, :] = cur
    row = lax.broadcasted_iota(jnp.int32, (SUBL, w), 0)
    tail = cur[(seg - hist) * SUBL:, :]
    for i in range(hist):
        grp = jnp.where(row == SUBL - 1, prev_sc[i * SUBL:(i + 1) * SUBL, :],
                        tail[i * SUBL:(i + 1) * SUBL, :])
        ext_sc[i * SUBL:(i + 1) * SUBL, :] = pltpu.roll(grp, 1, 0)
    prev_sc[...] = tail
    convn_ref[0] = jnp.concatenate(
        [tail[i * SUBL + SUBL - 1:(i + 1) * SUBL, :] for i in range(hist)], axis=0)

    cw = cw_ref[...]
    gpc = rows_per_chunk // SUBL
    for c in range(w // LANES):
        lanes = slice(c * LANES, (c + 1) * LANES)
        taps = [jnp.broadcast_to(cw[k:k + 1, lanes], (SUBL, LANES)) for k in range(D_CONV)]
        bias = jnp.broadcast_to(cb_ref[:, lanes], (SUBL, LANES))

        def chunk(i, _, lanes=lanes, taps=taps, bias=bias):
            base = pl.multiple_of(i * rows_per_chunk, rows_per_chunk)
            accs = [bias] * gpc
            for g in range(gpc + D_CONV - 1):
                xg = ext_sc[pl.ds(base + g * SUBL, SUBL), lanes]
                for jj in range(gpc):
                    if 0 <= g - jj < D_CONV:
                        accs[jj] = accs[jj] + xg * taps[g - jj]
            conv_sc[pl.ds(base, rows_per_chunk), lanes] = jnp.concatenate(accs, axis=0)
            return 0

        lax.fori_loop(0, tt // rows_per_chunk, chunk, 0, unroll=2)

    y = _ln_silu(conv_sc[...], lng_ref[...], lnb_ref[...])
    for c in range(w // LANES):
        o_ref[c, 0] = y[:, c * LANES:(c + 1) * LANES]


def _conf(g_in, conv0, conv_w, conv_b, ln_g, ln_b, n, t, seg):
    wc = g_in.shape[0]
    w = wc * LANES
    hist = pl.BlockSpec((1, D_CONV - 1, w), lambda b, i: (b, 0, 0))
    params = (conv0, conv_w, conv_b.reshape(1, w), ln_g.reshape(1, w), ln_b.reshape(1, w))
    param_specs = [hist, _const_spec((D_CONV, w)), _const_spec((1, w)), _const_spec((1, w)),
                   _const_spec((1, w))]
    hist_shape = jax.ShapeDtypeStruct((n, D_CONV - 1, w), F32)
    if seg == 1:
        tt = _row_tile(t, 256)
        cols = pl.BlockSpec((wc, 1, tt, LANES), lambda b, i: (0, b, i, 0))
        out, cd = pl.pallas_call(
            functools.partial(_conf_body, tt=tt, w=w),
            grid=(n, t // tt),
            in_specs=[cols] + param_specs,
            out_specs=[pl.BlockSpec((1, tt, w), lambda b, i: (b, i, 0)), hist],
            out_shape=[jax.ShapeDtypeStruct((n, t, w), BF16), hist_shape],
            scratch_shapes=[pltpu.VMEM((tt + 32, w), F32)],
            compiler_params=_cparams(2),
            name="conformer_conv",
        )(g_in.reshape(wc, n, t, LANES), *params)
        return out.reshape(n * t, w), cd
    assert seg >= D_CONV - 1
    tt = SUBL * seg
    cols = pl.BlockSpec((wc, 1, tt, LANES), lambda b, i: (0, b, i, 0))
    out, cd = pl.pallas_call(
        functools.partial(_conf_seg_body, tt=tt, w=w, rows_per_chunk=64),
        grid=(n, t // tt),
        in_specs=[cols] + param_specs,
        out_specs=[cols, hist],
        out_shape=[jax.ShapeDtypeStruct((wc, n, t, LANES), F32), hist_shape],
        scratch_shapes=[pltpu.VMEM(((seg + D_CONV - 1) * SUBL, w), F32),
                        pltpu.VMEM(((D_CONV - 1) * SUBL, w), F32), pltpu.VMEM((tt, w), F32)],
        compiler_params=_cparams(2),
        name="conformer_conv_seg",
    )(g_in.reshape(wc, n, t, LANES), *params)
    return out.reshape(wc, n * t, LANES), cd


def _run_group(x3, cache, conv_b0, h_b0, c_re0, c_im0, conv_d0, p, prepped):
    n, t, d = x3.shape
    depth = p['ffn1_g'].shape[0]
    x = x3.reshape(n * t, d)
    ks, vs, cbs, hbs, cres, cims, cds = [], [], [], [], [], [], []
    for l in range(depth):
        x = _ffn(x, p['ffn1_g'][l], prepped['ffn1'], l)
        if l % 2 == 0:
            e = l // 2
            hw = p['even_w_in'].shape[2] // 5
            heads = hw // A_V
            seg = SEG_BLOCK // SUBL if t % SEG_BLOCK == 0 else 1
            q, k, kb, v, vb, xb, gb = _inproj_even(x, p['mix_g'][l], prepped['even_w_in'][e],
                                                   p['a_q_g'][e], p['a_k_g'][e], seg)
            lam_init = 0.8 - 0.6 * math.exp(-0.3 * l)
            if cache is None:
                att = _attn_prompt(q, kb, vb, p['a_lambda'][e], p['a_head_g'][e], n, t, lam_init)
            else:
                cache_k, cache_v, page_table = cache
                att = _attn_sample(q, k, v, cache_k, cache_v, page_table, e, p['a_lambda'][e],
                                   p['a_head_g'][e], n, t, lam_init)
            rec, cb, hb = _rglru(xb, gb, conv_b0[e], h_b0[e], p['b_conv_w'][e], p['b_conv_b'][e],
                                 prepped['b_gate_w'][e], p['b_gate_b'][e], p['b_lambda'][e], n, t, seg)
            mixer, mixer_seg = (att, rec, prepped['even_w_out'][e]), seg
            ks.append(k.reshape(n, t, heads, 2 * A_QK))
            vs.append(v.reshape(n, t, heads, A_V))
            cbs.append(cb)
            hbs.append(hb.reshape(n, hw))
        else:
            o = l // 2
            seg = SEG_BLOCK // SUBL if t % SEG_BLOCK == 0 else 1
            u, g_in = _inproj_odd(x, p['mix_g'][l], prepped['odd_w_in'][o], seg)
            groups = u.shape[0] * LANES // C_GROUP
            c_out, cre, cim = _s5(u, c_re0[o], c_im0[o], p['c_a_re'][o], p['c_a_im'][o], p['c_log_dt'][o],
                                  prepped['s5'][o], p['c_d'][o], p['c_glu_w'][o], p['c_glu_b'][o], n, t)
            d_out, cd = _conf(g_in, conv_d0[o], p['d_conv_w'][o], p['d_conv_b'][o],
                              p['d_ln_g'][o], p['d_ln_b'][o], n, t, seg)
            mixer, mixer_seg = (c_out, d_out, prepped['odd_w_out'][o]), seg
            cres.append(cre.reshape(n, groups, C_STATE))
            cims.append(cim.reshape(n, groups, C_STATE))
            cds.append(cd)
        x = _ffn(x, p['ffn2_g'][l], prepped['ffn2'], l, mixer, mixer_seg)
    stack = lambda xs: xs[0][None] if len(xs) == 1 else jnp.stack(xs)
    return (x.reshape(n, t, d), stack(ks), stack(vs), stack(cbs), stack(hbs),
            stack(cres), stack(cims), stack(cds))


def kernel(x_prompt, x_sample, cache_k, cache_v, state_conv_b, state_h_b, state_c_re, state_c_im,
           state_conv_d, page_table, ffn1_g, ffn1_w_gu, ffn1_w_down, mix_g, ffn2_g, ffn2_w_gu,
           ffn2_w_down, even_w_in, even_w_out, a_q_g, a_k_g, a_lambda, a_head_g, b_conv_w, b_conv_b,
           b_gate_w, b_gate_b, b_lambda, odd_w_in, odd_w_out, c_a_re, c_a_im, c_log_dt, c_b_re, c_b_im,
           c_c_re, c_c_im, c_d, c_glu_w, c_glu_b, d_conv_w, d_conv_b, d_ln_g, d_ln_b):
    p = dict(ffn1_g=ffn1_g, mix_g=mix_g, ffn2_g=ffn2_g, even_w_in=even_w_in, a_q_g=a_q_g, a_k_g=a_k_g,
             a_lambda=a_lambda, a_head_g=a_head_g, b_conv_w=b_conv_w, b_conv_b=b_conv_b,
             b_gate_b=b_gate_b, b_lambda=b_lambda, c_a_re=c_a_re, c_a_im=c_a_im, c_log_dt=c_log_dt,
             c_d=c_d, c_glu_w=c_glu_w, c_glu_b=c_glu_b, d_conv_w=d_conv_w, d_conv_b=d_conv_b,
             d_ln_g=d_ln_g, d_ln_b=d_ln_b)
    depth = ffn1_g.shape[0]
    n_even, n_odd = even_w_in.shape[0], odd_w_in.shape[0]
    prepped = dict(
        ffn1=_prep_ffn(ffn1_w_gu, ffn1_w_down),
        ffn2=_prep_ffn(ffn2_w_gu, ffn2_w_down),
        even_w_in=[even_w_in[e].astype(BF16) for e in range(n_even)],
        even_w_out=[even_w_out[e].astype(BF16) for e in range(n_even)],
        odd_w_in=[odd_w_in[o].astype(BF16) for o in range(n_odd)],
        odd_w_out=[odd_w_out[o].astype(BF16) for o in range(n_odd)],
        b_gate_w=[_prep_gate_w(b_gate_w[e]) for e in range(n_even)],
        s5=[_prep_s5(c_b_re[o], c_b_im[o], c_c_re[o], c_c_im[o]) for o in range(n_odd)],
    )
    b = x_prompt.shape[0]
    dt = x_prompt.dtype
    hw = even_w_in.shape[2] // 5
    groups, states = state_c_re.shape[2], state_c_re.shape[3]
    prompt = _run_group(
        x_prompt, None,
        jnp.zeros((n_even, b, B_CONV - 1, hw), dt), jnp.zeros((n_even, b, hw), dt),
        jnp.zeros((n_odd, b, groups, states), dt), jnp.zeros((n_odd, b, groups, states), dt),
        jnp.zeros((n_odd, b, D_CONV - 1, hw), dt), p, prepped)
    sample = _run_group(
        x_sample, (cache_k, cache_v, page_table), state_conv_b, state_h_b, state_c_re, state_c_im,
        state_conv_d, p, prepped)
    return (prompt[0], sample[0]) + prompt[1:] + sample[1:]
```

```python
import functools
import math

import jax
import jax.numpy as jnp
from jax import lax
from jax.experimental import pallas as pl
from jax.experimental.pallas import tpu as pltpu

F32 = jnp.float32
BF16 = jnp.bfloat16
EPS = 1e-6
NEG = -1e30
LOG2E = 1.4426950408889634
ATTN_SCORE_BOUND = 64.0

LANES = 128
SUBL = 8
V7X_MXU_DIM = 256
V7X_VMEM_LIMIT = 56 * 1024 * 1024

A_HEADS = 4
A_QK = 64
A_V = 128
B_BLOCKS = 8
B_CONV = 4
C_RG = 8.0
C_GROUP = 16
C_STATE = 64
D_CONV = 31
SEG_BLOCK = 512
S5_SCAN_LANES = 1024
S5_SCAN_UNROLL = 8


def _cparams(n_axes):
    return pltpu.CompilerParams(dimension_semantics=("arbitrary",) * n_axes,
                                vmem_limit_bytes=V7X_VMEM_LIMIT)


def _const_spec(shape):
    nd = len(shape)
    return pl.BlockSpec(shape, lambda *_: (0,) * nd)


def _rms(x, g):
    return x * lax.rsqrt(jnp.mean(x * x, axis=-1, keepdims=True) + EPS) * g


def _gelu_tanh(x):
    return 0.5 * x * (1.0 + jnp.tanh(math.sqrt(2.0 / math.pi) * (x + 0.044715 * (x * x * x))))


def _dot(a, b):
    return jnp.dot(a, b, preferred_element_type=F32)


def _row_tile(m, want):
    t = min(m, want)
    while m % t:
        t //= 2
    return t


def _store_col_tiles(ref, val, seg):
    for c in range(val.shape[1] // LANES):
        tile = val[:, c * LANES:(c + 1) * LANES]
        if seg == 1:
            ref[c] = tile
        else:
            for s in range(SUBL):
                ref[c, pl.ds(s, seg, stride=SUBL), :] = tile[s * seg:(s + 1) * seg, :]


def _load_col_tiles(ref, seg):
    cols = []
    for c in range(ref.shape[0]):
        if seg == 1:
            cols.append(ref[c])
        else:
            cols.append(jnp.concatenate(
                [ref[c, pl.ds(s, seg, stride=SUBL), :] for s in range(SUBL)], axis=0))
    return jnp.concatenate(cols, axis=-1)


def _mixer_half(ref, seg):
    if len(ref.shape) == 2:
        return ref[...]
    return _load_col_tiles(ref, seg).astype(BF16)


def _ffn_body(x_ref, g_ref, wg_ref, wu_ref, wd_ref, *rest, ck, with_mixer, seg):
    x = x_ref[...]
    if with_mixer:
        a_ref, b_ref, wo_ref, o_ref = rest
        hw = wo_ref.shape[0] // 2
        x = (x + _dot(_mixer_half(a_ref, seg), wo_ref[:hw, :])
             + _dot(_mixer_half(b_ref, seg), wo_ref[hw:, :]))
    else:
        o_ref, = rest
    h = _rms(x, g_ref[...]).astype(BF16)
    acc = jnp.zeros(x.shape, F32)
    for c in range(wg_ref.shape[1] // ck):
        sl = slice(c * ck, (c + 1) * ck)
        gate = _dot(h, wg_ref[:, sl])
        up = _dot(h, wu_ref[:, sl])
        a = (gate * jax.nn.sigmoid(gate) * up).astype(BF16)
        acc = acc + _dot(a, wd_ref[sl, :])
    o_ref[...] = x + 0.5 * acc


def _split_gu_body(w_ref, g_ref, u_ref, *, f, fp):
    x = w_ref[0]
    lane = lax.broadcasted_iota(jnp.int32, (x.shape[0], fp), 1)
    g_ref[0] = jnp.where(lane < f, x[:, :fp], 0.0).astype(BF16)
    start = f // LANES * LANES
    up = pltpu.roll(x[:, start:start + fp], fp - (f - start), 1)
    u_ref[0] = jnp.where(lane < f, up, 0.0).astype(BF16)


def _pad_rows_body(w_ref, o_ref, *, f, rows):
    r = lax.broadcasted_iota(jnp.int32, w_ref.shape[1:], 0) + pl.program_id(1) * rows
    o_ref[0] = jnp.where(r < f, w_ref[0], 0.0).astype(BF16)


def _prep_ffn(w_gu, w_down):
    layers, d, f2 = w_gu.shape
    f = f2 // 2
    fp = -(-f // V7X_MXU_DIM) * V7X_MXU_DIM
    start = f // LANES * LANES
    if start + fp > f2 or d % V7X_MXU_DIM:
        pad_c = ((0, 0), (0, 0), (0, fp - f))
        return (jnp.pad(w_gu[..., :f], pad_c).astype(BF16), jnp.pad(w_gu[..., f:], pad_c).astype(BF16),
                jnp.pad(w_down, ((0, 0), (0, fp - f), (0, 0))).astype(BF16))
    rows = V7X_MXU_DIM
    half = pl.BlockSpec((1, rows, fp), lambda l, i: (l, i, 0))
    wg, wu = pl.pallas_call(
        functools.partial(_split_gu_body, f=f, fp=fp),
        grid=(layers, d // rows),
        in_specs=[pl.BlockSpec((1, rows, f2), lambda l, i: (l, i, 0))],
        out_specs=[half, half],
        out_shape=[jax.ShapeDtypeStruct((layers, d, fp), BF16)] * 2,
        compiler_params=_cparams(2),
        name="ffn_split_gate_up",
    )(w_gu)
    rows = fp // 2 if fp % (2 * SUBL) == 0 else rows
    blk = pl.BlockSpec((1, rows, d), lambda l, i: (l, i, 0))
    wd = pl.pallas_call(
        functools.partial(_pad_rows_body, f=f, rows=rows),
        grid=(layers, fp // rows),
        in_specs=[blk],
        out_specs=blk,
        out_shape=jax.ShapeDtypeStruct((layers, fp, d), BF16),
        compiler_params=_cparams(2),
        name="ffn_pad_down",
    )(w_down)
    return wg, wu, wd


def _ffn(x, g, w, layer, mixer=None, seg=1):
    wg, wu, wd = w
    m, d = x.shape
    fp = wg.shape[2]
    tm = SUBL * seg if seg > 1 else _row_tile(m, 512)
    row = pl.BlockSpec((tm, d), lambda i: (i, 0))
    of_layer = lambda r, c: pl.BlockSpec((None, r, c), lambda i: (layer, 0, 0))
    in_specs = [row, _const_spec((1, d)), of_layer(d, fp), of_layer(d, fp), of_layer(fp, d)]
    args = [x, g.reshape(1, d), wg, wu, wd]
    if mixer is not None:
        a, b, w_out = mixer
        hw = w_out.shape[0] // 2
        spec = lambda v: (pl.BlockSpec((tm, hw), lambda i: (i, 0)) if v.ndim == 2 else
                          pl.BlockSpec((v.shape[0], tm, LANES), lambda i: (0, i, 0)))
        in_specs += [spec(a), spec(b), _const_spec((2 * hw, d))]
        args += [a, b, w_out]
    return pl.pallas_call(
        functools.partial(_ffn_body, ck=V7X_MXU_DIM, with_mixer=mixer is not None, seg=seg),
        grid=(m // tm,),
        in_specs=in_specs,
        out_specs=row,
        out_shape=jax.ShapeDtypeStruct((m, d), F32),
        compiler_params=_cparams(1),
        name="ffn_mix" if mixer is not None else "ffn",
    )(*args)


def _inproj_even_body(x_ref, g_ref, w_ref, gq_ref, gk_ref, ones_ref,
                      q_ref, k_ref, kb_ref, v_ref, vb_ref, xb_ref, gb_ref, *, hw, tm, seg):
    x = x_ref[...]
    h = _rms(x, g_ref[...]).astype(BF16)
    proj = _dot(h, w_ref[...])

    def group_rms(t, gain):
        ss = (t * t).astype(BF16)
        gs = jnp.concatenate(
            [_dot(ss[:, c * V7X_MXU_DIM:(c + 1) * V7X_MXU_DIM], ones_ref[...])
             for c in range(hw // V7X_MXU_DIM)], axis=-1)
        return t * lax.rsqrt(gs * (1.0 / A_QK) + EPS) * gain

    q = group_rms(proj[:, 0:hw], gq_ref[...]) * (A_QK ** -0.5 * LOG2E)
    k = group_rms(proj[:, hw:2 * hw], gk_ref[...])
    v = proj[:, 2 * hw:3 * hw]
    q_ref[...] = q.astype(BF16)
    kb_ref[...] = k.astype(BF16)
    vb_ref[...] = v.astype(BF16)
    heads = hw // A_V
    for hd in range(heads):
        k_ref[pl.ds(hd, tm, stride=heads), :] = k[:, hd * A_V:(hd + 1) * A_V]
        v_ref[pl.ds(hd, tm, stride=heads), :] = v[:, hd * A_V:(hd + 1) * A_V]
    _store_col_tiles(xb_ref, proj[:, 3 * hw:4 * hw], seg)
    _store_col_tiles(gb_ref, proj[:, 4 * hw:5 * hw], seg)


def _inproj_even(x, g, w_in, gq, gk, seg):
    m, d = x.shape
    hw = w_in.shape[1] // 5
    heads = hw // A_V
    tm = SUBL * seg if seg > 1 else _row_tile(m, 512)
    row = pl.BlockSpec((tm, d), lambda i: (i, 0))
    orow = pl.BlockSpec((tm, hw), lambda i: (i, 0))
    hrow = pl.BlockSpec((tm * heads, A_V), lambda i: (i, 0))
    cols = pl.BlockSpec((hw // LANES, tm, LANES), lambda i: (0, i, 0))
    idx = jnp.arange(V7X_MXU_DIM) // A_QK
    ones_bd = (idx[:, None] == idx[None, :]).astype(BF16)
    tile = lambda t: jnp.tile(t, hw // A_QK).reshape(1, hw)
    sds = lambda dt: jax.ShapeDtypeStruct((m, hw), dt)
    hsds = jax.ShapeDtypeStruct((m * heads, A_V), F32)
    csds = jax.ShapeDtypeStruct((hw // LANES, m, LANES), F32)
    return pl.pallas_call(
        functools.partial(_inproj_even_body, hw=hw, tm=tm, seg=seg),
        grid=(m // tm,),
        in_specs=[row, _const_spec((1, d)), _const_spec(w_in.shape), _const_spec((1, hw)),
                  _const_spec((1, hw)), _const_spec(ones_bd.shape)],
        out_specs=[orow, hrow, orow, hrow, orow, cols, cols],
        out_shape=[sds(BF16), hsds, sds(BF16), hsds, sds(BF16), csds, csds],
        compiler_params=_cparams(1),
        name="inproj_even",
    )(x, g.reshape(1, d), w_in, tile(gq), tile(gk), ones_bd)


def _diff_lambda(lv, lam_init):
    s01 = jnp.sum(lv[0:1, :] * lv[1:2, :], axis=-1, keepdims=True)
    s23 = jnp.sum(lv[2:3, :] * lv[3:4, :], axis=-1, keepdims=True)
    return jnp.exp(s01) - jnp.exp(s23) + lam_init


def _split_maps(q):
    lane = lax.broadcasted_iota(jnp.int32, q.shape, 1)
    zero = jnp.zeros_like(q)
    return jnp.where(lane < A_QK, q, zero), jnp.where(lane >= A_QK, q, zero)


def _scores(q, k):
    return lax.dot_general(q, k, (((1,), (1,)), ((), ())), preferred_element_type=F32)


def _attn_prompt_body(lam_ref, q_ref, k_ref, v_ref, hg_ref, o_ref, vext_sc, kn2_sc,
                      *, tq, tk, hps, lam_init):
    qi = pl.program_id(2)

    def max_row_norm2(x):
        xf = x.astype(F32)
        per_head = [jnp.sum(xf[:, hh * A_V:(hh + 1) * A_V] ** 2, axis=-1, keepdims=True)
                    for hh in range(hps)]
        return jnp.max(functools.reduce(jnp.maximum, per_head), axis=0, keepdims=True)

    @pl.when(qi == 0)
    def _():
        for hh in range(hps):
            vext_sc[:, 2 * hh * A_V:(2 * hh + 1) * A_V] = v_ref[:, hh * A_V:(hh + 1) * A_V]
            vext_sc[:, (2 * hh + 1) * A_V:(2 * hh + 2) * A_V] = jnp.ones((v_ref.shape[0], A_V), BF16)
        kn2_sc[...] = max_row_norm2(k_ref[...])

    lam = _diff_lambda(lam_ref[...], lam_init)
    q = q_ref[...]
    qs = [m for hh in range(hps) for m in _split_maps(q[:, hh * A_V:(hh + 1) * A_V])]

    per_q = tq // tk

    def chain_inputs(kb, c):
        rows = pl.ds(pl.multiple_of(kb * tk, tk), tk)
        hh = c // 2
        s = _scores(qs[c], k_ref[rows, hh * A_V:(hh + 1) * A_V])
        return s, vext_sc[rows, 2 * hh * A_V:(2 * hh + 2) * A_V]

    def causal(shape, d):
        col = lax.broadcasted_iota(jnp.int32, shape, 1)
        return col + d * tk <= lax.broadcasted_iota(jnp.int32, shape, 0)

    def finish(accs):
        for hh in range(hps):
            a0, a1 = accs[2 * hh], accs[2 * hh + 1]
            o = a0[:, :A_V] / a0[:, A_V:] - lam * (a1[:, :A_V] / a1[:, A_V:])
            o_ref[:, hh * A_V:(hh + 1) * A_V] = (
                _rms(o, hg_ref[...]) * (1.0 - lam_init)).astype(o_ref.dtype)

    def bounded():
        def step(kb, accs, diag):
            new = []
            for c in range(2 * hps):
                s, v = chain_inputs(kb, c)
                p = jnp.exp2(s)
                if diag is not None:
                    p = jnp.where(causal(p.shape, diag), p, 0.0)
                new.append(accs[c] + _dot(p.astype(BF16), v))
            return tuple(new)

        def diag_halves(accs):
            hk = tk // 2
            base = qi * tk
            new = []
            for c in range(2 * hps):
                hh = c // 2
                kcol = slice(hh * A_V, (hh + 1) * A_V)
                vcol = slice(2 * hh * A_V, (2 * hh + 2) * A_V)
                lo = pl.ds(pl.multiple_of(base, hk), hk)
                hi = pl.ds(pl.multiple_of(base + hk, hk), hk)
                p_lo = jnp.exp2(_scores(qs[c], k_ref[lo, kcol]))
                p_lo = jnp.where(causal(p_lo.shape, 0), p_lo, 0.0)
                p_hi = jnp.exp2(_scores(qs[c][hk:, :], k_ref[hi, kcol]))
                p_hi = jnp.where(causal(p_hi.shape, 0), p_hi, 0.0)
                pv_hi = _dot(p_hi.astype(BF16), vext_sc[hi, vcol])
                pv = _dot(p_lo.astype(BF16), vext_sc[lo, vcol])
                new.append(accs[c] + pv + jnp.concatenate([jnp.zeros_like(pv_hi), pv_hi], axis=0))
            return new

        zero = jnp.zeros((tq, 2 * A_V), F32)
        accs = lax.fori_loop(0, qi * per_q, lambda kb, a: step(kb, a, None), (zero,) * (2 * hps))
        if per_q == 1 and tk % (2 * V7X_MXU_DIM) == 0:
            accs = diag_halves(accs)
        else:
            for d in range(per_q):
                accs = step(qi * per_q + d, accs, d)
        finish(accs)

    def general():
        def step(kb, carry, diag):
            new = []
            for c in range(2 * hps):
                m, acc = carry[c]
                s, v = chain_inputs(kb, c)
                if diag is not None:
                    s = jnp.where(causal(s.shape, diag), s, NEG)
                m_new = jnp.maximum(m, jnp.max(s, axis=-1, keepdims=True))
                p = jnp.exp2(s - m_new)
                new.append((m_new, jnp.exp2(m - m_new) * acc + _dot(p.astype(BF16), v)))
            return tuple(new)

        init_one = (jnp.full((tq, 1), NEG, F32), jnp.zeros((tq, 2 * A_V), F32))
        carry = lax.fori_loop(0, qi * per_q, lambda kb, c: step(kb, c, None), (init_one,) * (2 * hps))
        for d in range(per_q):
            carry = step(qi * per_q + d, carry, d)
        finish([acc for _, acc in carry])

    bound2 = max_row_norm2(q) * kn2_sc[...]
    lax.cond(bound2[0, 0] <= ATTN_SCORE_BOUND ** 2, bounded, general)


def _attn_prompt(q, k, v, lam_p, head_g, n, t, lam_init):
    hw = q.shape[1]
    heads = hw // A_V
    hps = next(c for c in (4, 2, 1) if heads % c == 0)
    tk = _row_tile(t, 512)
    tq = tk
    nq = t // tq
    qspec = pl.BlockSpec((tq, hps * A_V), lambda b, h, i: (b * nq + i, h))
    kspec = pl.BlockSpec((t, hps * A_V), lambda b, h, i: (b, h))
    return pl.pallas_call(
        functools.partial(_attn_prompt_body, tq=tq, tk=tk, hps=hps, lam_init=lam_init),
        grid=(n, heads // hps, nq),
        in_specs=[_const_spec(lam_p.shape), qspec, kspec, kspec, _const_spec((1, A_V))],
        out_specs=qspec,
        out_shape=jax.ShapeDtypeStruct((n * t, hw), BF16),
        scratch_shapes=[pltpu.VMEM((t, hps * 2 * A_V), BF16), pltpu.VMEM((1, 1), F32)],
        compiler_params=_cparams(3),
        name="attn_prompt",
    )(lam_p, q, k, v, head_g.reshape(1, A_V))


def _attn_sample_body(pt_ref, lam_ref, q_ref, kn_ref, vn_ref, hg_ref, ck_ref, cv_ref, o_ref,
                      kbuf, vbuf, sem, bias_sc, m_sc, l_sc, acc_sc,
                      *, pps, prow, base, heads, t, lam_init):
    seq, step = pl.program_id(0), pl.program_id(1)
    n_seq, n_step = pl.num_programs(0), pl.num_programs(1)
    flat = seq * n_step + step
    slot = flat % 2

    def page_copies(b, i, s):
        cps = []
        for r in range(pps):
            src = pl.ds(pl.multiple_of((base + pt_ref[b, i * pps + r]) * prow, prow), prow)
            dst = pl.ds(r * prow, prow)
            cps.append(pltpu.make_async_copy(ck_ref.at[src, :], kbuf.at[s, dst, :], sem.at[s, 0]))
            cps.append(pltpu.make_async_copy(cv_ref.at[src, :], vbuf.at[s, dst, :], sem.at[s, 1]))
        return cps

    def start_all(cps):
        for j, cp in enumerate(cps):
            cp.start(priority=j % 2)

    @pl.when(flat == 0)
    def _():
        start_all(page_copies(seq, step, slot))

    last = flat == n_seq * n_step - 1
    wrap = step == n_step - 1
    nb = jnp.where(last, seq, jnp.where(wrap, seq + 1, seq))
    ni = jnp.where(last, step, jnp.where(wrap, 0, step + 1))
    start_all(page_copies(nb, ni, 1 - slot))
    for cp in page_copies(seq, step, slot):
        cp.wait()

    qrows = 2 * t
    q = q_ref[0]
    qall = jnp.concatenate(
        [piece for h in range(heads) for piece in _split_maps(q[:, h * A_V:(h + 1) * A_V])], axis=0)

    def head_match(shape):
        row = lax.broadcasted_iota(jnp.int32, shape, 0)
        col = lax.broadcasted_iota(jnp.int32, shape, 1)
        return row, col, (col % heads) == (row // qrows)

    def update(s, v):
        m = m_sc[...]
        m_new = jnp.maximum(m, jnp.max(s, axis=-1, keepdims=True))
        alpha = jnp.exp2(m - m_new)
        p = jnp.exp2(s - m_new)
        l_sc[...] = alpha * l_sc[...] + jnp.sum(p, axis=-1, keepdims=True)
        acc_sc[...] = alpha * acc_sc[...] + _dot(p.astype(BF16), v)
        m_sc[...] = m_new

    @pl.when((seq == 0) & (step == 0))
    def _():
        _, _, ok = head_match(bias_sc.shape)
        bias_sc[...] = jnp.where(ok, 0.0, NEG)

    @pl.when(step == 0)
    def _():
        m_sc[...] = jnp.full(m_sc.shape, NEG, F32)
        l_sc[...] = jnp.zeros(l_sc.shape, F32)
        acc_sc[...] = jnp.zeros(acc_sc.shape, F32)
        s = _scores(qall, kn_ref[0])
        row, col, ok = head_match(s.shape)
        ok = ok & ((col // heads) <= (row % t))
        update(jnp.where(ok, s, NEG), vn_ref[0])

    update(_scores(qall, kbuf[slot].astype(BF16)) + bias_sc[...], vbuf[slot].astype(BF16))

    @pl.when(last)
    def _():
        for cp in page_copies(seq, step, 1 - slot):
            cp.wait()

    @pl.when(step == n_step - 1)
    def _():
        lam = _diff_lambda(lam_ref[...], lam_init)
        o = acc_sc[...] / l_sc[...]
        for h in range(heads):
            oh = o[h * qrows:h * qrows + t, :] - lam * o[h * qrows + t:(h + 1) * qrows, :]
            o_ref[0, :, h * A_V:(h + 1) * A_V] = (
                _rms(oh, hg_ref[...]) * (1.0 - lam_init)).astype(o_ref.dtype)


def _attn_sample(q, k_new, v_new, cache_k, cache_v, page_table, layer, lam_p, head_g, n, t, lam_init):
    hw = q.shape[1]
    heads = hw // A_V
    n_layers, n_pool, page = cache_k.shape[:3]
    n_pages = page_table.shape[1]
    prow = page * heads
    pps = 16
    while n_pages % pps:
        pps //= 2
    ck = cache_k.reshape(n_layers * n_pool * prow, A_V)
    cv = cache_v.reshape(n_layers * n_pool * prow, A_V)
    pad = ((0, 0), (0, prow - t * heads), (0, 0))
    kn = jnp.pad(k_new.astype(BF16).reshape(n, t * heads, A_V), pad)
    vn = jnp.pad(v_new.astype(BF16).reshape(n, t * heads, A_V), pad)
    qr = heads * 2 * t
    grid_spec = pltpu.PrefetchScalarGridSpec(
        num_scalar_prefetch=1,
        grid=(n, n_pages // pps),
        in_specs=[pl.BlockSpec(lam_p.shape, lambda b, i, pt: (0, 0)),
                  pl.BlockSpec((1, t, hw), lambda b, i, pt: (b, 0, 0)),
                  pl.BlockSpec((1, prow, A_V), lambda b, i, pt: (b, 0, 0)),
                  pl.BlockSpec((1, prow, A_V), lambda b, i, pt: (b, 0, 0)),
                  pl.BlockSpec((1, A_V), lambda b, i, pt: (0, 0)),
                  pl.BlockSpec(memory_space=pl.ANY), pl.BlockSpec(memory_space=pl.ANY)],
        out_specs=pl.BlockSpec((1, t, hw), lambda b, i, pt: (b, 0, 0)),
        scratch_shapes=[pltpu.VMEM((2, pps * prow, A_V), F32), pltpu.VMEM((2, pps * prow, A_V), F32),
                        pltpu.SemaphoreType.DMA((2, 2)),
                        pltpu.VMEM((qr, pps * prow), F32),
                        pltpu.VMEM((qr, 1), F32), pltpu.VMEM((qr, 1), F32), pltpu.VMEM((qr, A_V), F32)],
    )
    out = pl.pallas_call(
        functools.partial(_attn_sample_body, pps=pps, prow=prow, base=layer * n_pool, heads=heads, t=t,
                          lam_init=lam_init),
        grid_spec=grid_spec,
        out_shape=jax.ShapeDtypeStruct((n, t, hw), BF16),
        compiler_params=_cparams(2),
        name="attn_sample",
    )(page_table, lam_p, q.reshape(n, t, hw), kn, vn, head_g.reshape(1, A_V), ck, cv)
    return out.reshape(n * t, hw)


def _shift_rows(x, d, fill):
    row = lax.broadcasted_iota(jnp.int32, x.shape, 0)
    return jnp.where(row >= d, pltpu.roll(x, d, 0), fill)


def _linear_scan(a, b):
    d = 1
    while d < a.shape[0]:
        b = b + a * _shift_rows(b, d, 0.0)
        a = a * _shift_rows(a, d, 1.0)
        d *= 2
    return a, b


def _rglru_body(xb_ref, gb_ref, conv0_ref, h0_ref, cw_ref, cb_ref, wg_ref, gbias_ref, lam_ref,
                rec_ref, convn_ref, hl_ref, ext_sc, h_sc, *, tt, w):
    halo = 8
    step = pl.program_id(1)

    @pl.when(step == 0)
    def _():
        ext_sc[0:halo, :] = jnp.zeros((halo, w), F32)
        ext_sc[halo - (B_CONV - 1):halo, :] = conv0_ref[0]
        h_sc[...] = h0_ref[0]

    x = jnp.concatenate([xb_ref[c, 0] for c in range(w // LANES)], axis=-1)
    gb = jnp.concatenate([gb_ref[c, 0] for c in range(w // LANES)], axis=-1)
    ext_sc[halo:halo + tt, :] = x
    cw = cw_ref[...]
    xc = cb_ref[...] + x * cw[B_CONV - 1:B_CONV, :]
    for j in range(1, B_CONV):
        xc = xc + ext_sc[halo - j:halo - j + tt, :] * cw[B_CONV - 1 - j:B_CONV - j, :]

    a, b = _rglru_coeffs(xc, wg_ref[...], gbias_ref[...], lam_ref[...], w)
    a_cum, h_loc = _linear_scan(a, b)
    h = h_loc + a_cum * h_sc[...]
    rec_ref[0] = (_gelu_tanh(gb) * h).astype(rec_ref.dtype)

    h_sc[...] = h[tt - 1:tt, :]
    hl_ref[0] = h[tt - 1:tt, :]
    convn_ref[0] = ext_sc[halo + tt - (B_CONV - 1):halo + tt, :]
    ext_sc[0:halo, :] = ext_sc[tt:tt + halo, :]


def _rglru_coeffs(xc, gate_w, gate_b, lam, w):
    pre = _dot(xc.astype(BF16), gate_w) + gate_b
    r = jax.nn.sigmoid(pre[:, :w])
    i = jax.nn.sigmoid(pre[:, w:])
    softplus_neg = jnp.maximum(-lam, 0.0) + jnp.log1p(jnp.exp(-jnp.abs(lam)))
    a = jnp.exp((-C_RG) * r * softplus_neg)
    return a, jnp.sqrt(1.0 - a * a) * (i * xc)


def _rglru_seg_body(xb_ref, gb_ref, conv0_ref, h0_ref, cw_ref, cb_ref, wg_ref, gbias_ref, lam_ref,
                    rec_ref, convn_ref, hl_ref, prev_sc, h_sc, *, tt, w):
    seg = tt // SUBL
    hist = B_CONV - 1
    step = pl.program_id(1)

    @pl.when(step == 0)
    def _():
        c0 = conv0_ref[0]
        for i in range(hist):
            prev_sc[i * SUBL:(i + 1) * SUBL, :] = jnp.broadcast_to(c0[i:i + 1, :], (SUBL, w))
        h_sc[...] = h0_ref[0]

    cur = jnp.concatenate([xb_ref[c, 0] for c in range(w // LANES)], axis=-1)
    gb = jnp.concatenate([gb_ref[c, 0] for c in range(w // LANES)], axis=-1)
    row = lax.broadcasted_iota(jnp.int32, (SUBL, w), 0)
    tail = cur[(seg - hist) * SUBL:, :]
    halo = [pltpu.roll(jnp.where(row == SUBL - 1, prev_sc[i * SUBL:(i + 1) * SUBL, :],
                                 tail[i * SUBL:(i + 1) * SUBL, :]), 1, 0) for i in range(hist)]
    ext = jnp.concatenate(halo + [cur], axis=0)
    prev_sc[...] = tail
    convn_ref[0] = jnp.concatenate(
        [tail[i * SUBL + SUBL - 1:(i + 1) * SUBL, :] for i in range(hist)], axis=0)

    cw = cw_ref[...]
    xc = cb_ref[...] + jnp.zeros((tt, w), F32)
    for k in range(B_CONV):
        xc = xc + ext[k * SUBL:k * SUBL + tt, :] * cw[k:k + 1, :]
    a, b = _rglru_coeffs(xc, wg_ref[...], gbias_ref[...], lam_ref[...], w)

    grp = lambda v, j: v[j * SUBL:(j + 1) * SUBL, :]
    e, prod = grp(b, 0), grp(a, 0)
    for j in range(1, seg):
        e, prod = grp(a, j) * e + grp(b, j), grp(a, j) * prod
    c = h_sc[...]
    h_in = jnp.zeros((SUBL, w), F32)
    for k in range(SUBL):
        h_in = jnp.where(row == k, c, h_in)
        c = prod[k:k + 1, :] * c + e[k:k + 1, :]
    h_sc[...] = c
    hl_ref[0] = c
    hs, h = [], h_in
    for j in range(seg):
        h = grp(a, j) * h + grp(b, j)
        hs.append(h)
    rec = _gelu_tanh(gb) * jnp.concatenate(hs, axis=0)
    for cc in range(w // LANES):
        rec_ref[cc, 0] = rec[:, cc * LANES:(cc + 1) * LANES]


def _prep_gate_w(gate_w):
    _, nb, bs, _ = gate_w.shape
    eye = jnp.eye(nb, dtype=gate_w.dtype)
    dense = jnp.einsum('gbij,bc->gbicj', gate_w, eye).reshape(2, nb * bs, nb * bs)
    return jnp.concatenate([dense[0], dense[1]], axis=1).astype(BF16)


def _rglru(xb, gb, conv0, h0, conv_w, conv_b, gate_wd, gate_b, lam_p, n, t, seg):
    wc = xb.shape[0]
    w = wc * LANES
    per_n = lambda rows: pl.BlockSpec((1, rows, w), lambda b, i: (b, 0, 0))
    tt = SUBL * seg if seg > 1 else _row_tile(t, 256)
    cols = pl.BlockSpec((wc, 1, tt, LANES), lambda b, i: (0, b, i, 0))
    in_specs = [cols, cols, per_n(B_CONV - 1), per_n(1), _const_spec((B_CONV, w)), _const_spec((1, w)),
                _const_spec((w, 2 * w)), _const_spec((1, 2 * w)), _const_spec((1, w))]
    args = (xb.reshape(wc, n, t, LANES), gb.reshape(wc, n, t, LANES), conv0, h0.reshape(n, 1, w), conv_w,
            conv_b.reshape(1, w), gate_wd, gate_b.reshape(1, 2 * w), lam_p.reshape(1, w))
    state_shapes = [jax.ShapeDtypeStruct((n, B_CONV - 1, w), F32), jax.ShapeDtypeStruct((n, 1, w), F32)]
    if seg == 1:
        rec, cb, hb = pl.pallas_call(
            functools.partial(_rglru_body, tt=tt, w=w),
            grid=(n, t // tt),
            in_specs=in_specs,
            out_specs=[pl.BlockSpec((1, tt, w), lambda b, i: (b, i, 0)), per_n(B_CONV - 1), per_n(1)],
            out_shape=[jax.ShapeDtypeStruct((n, t, w), BF16)] + state_shapes,
            scratch_shapes=[pltpu.VMEM((tt + 8, w), F32), pltpu.VMEM((1, w), F32)],
            compiler_params=_cparams(2),
            name="rglru",
        )(*args)
        return rec.reshape(n * t, w), cb, hb
    assert seg >= B_CONV - 1
    rec, cb, hb = pl.pallas_call(
        functools.partial(_rglru_seg_body, tt=tt, w=w),
        grid=(n, t // tt),
        in_specs=in_specs,
        out_specs=[cols, per_n(B_CONV - 1), per_n(1)],
        out_shape=[jax.ShapeDtypeStruct((wc, n, t, LANES), F32)] + state_shapes,
        scratch_shapes=[pltpu.VMEM(((B_CONV - 1) * SUBL, w), F32), pltpu.VMEM((1, w), F32)],
        compiler_params=_cparams(2),
        name="rglru_seg",
    )(*args)
    return rec.reshape(wc, n * t, LANES), cb, hb


def _inproj_odd_body(x_ref, g_ref, w_ref, u_ref, gin_ref, *, hw, seg):
    h = _rms(x_ref[...], g_ref[...]).astype(BF16)
    proj = _dot(h, w_ref[...])
    _store_col_tiles(u_ref, proj[:, :hw], seg)
    _store_col_tiles(gin_ref, proj[:, hw:2 * hw] * jax.nn.sigmoid(proj[:, 2 * hw:]), seg)


def _inproj_odd(x, g, w_in, seg):
    m, d = x.shape
    hw = w_in.shape[1] // 3
    tm = SUBL * seg if seg > 1 else _row_tile(m, 512)
    row = pl.BlockSpec((tm, d), lambda i: (i, 0))
    cols = pl.BlockSpec((hw // LANES, tm, LANES), lambda i: (0, i, 0))
    return pl.pallas_call(
        functools.partial(_inproj_odd_body, hw=hw, seg=seg),
        grid=(m // tm,),
        in_specs=[row, _const_spec((1, d)), _const_spec(w_in.shape)],
        out_specs=[cols, cols],
        out_shape=[jax.ShapeDtypeStruct((hw // LANES, m, LANES), F32)] * 2,
        compiler_params=_cparams(1),
        name="inproj_odd",
    )(x, g.reshape(1, d), w_in)


def _cmul(ar, ai, br, bi):
    return ar * br - ai * bi, ar * bi + ai * br


def _s5_discretise(a_re, a_im, log_dt):
    dt = jnp.exp(log_dt)
    mag = jnp.exp(dt * a_re)
    ab_re = mag * jnp.cos(dt * a_im)
    ab_im = mag * jnp.sin(dt * a_im)
    den = a_re * a_re + a_im * a_im
    nr = ab_re - 1.0
    return ab_re, ab_im, (nr * a_re + ab_im * a_im) / den, (ab_im * a_re - nr * a_im) / den


def _s5_body(u_ref, re0_ref, im0_ref, are_ref, aim_ref, ldt_ref,
             bre_ref, bim_ref, cre_ref, cim_ref, d_ref, gw_ref, gbias_ref, o_ref, sre_ref, sim_ref,
             st_re, st_im, bf_re, bf_im, lam_re, lam_im, lseg_re, lseg_im, v_re, v_im,
             *, tt, w, sw):
    seg = tt // SUBL

    def step_rows(j):
        return pl.ds(pl.multiple_of(j * SUBL, SUBL), SUBL)

    @pl.when((pl.program_id(0) == 0) & (pl.program_id(1) == 0))
    def _():
        ab_re, ab_im, coef_re, coef_im = _s5_discretise(are_ref[...], aim_ref[...], ldt_ref[...])
        f_re, f_im = _cmul(coef_re, coef_im, bre_ref[...], bim_ref[...])
        bf_re[...] = f_re.astype(BF16)
        bf_im[...] = f_im.astype(BF16)
        lam_re[...] = jnp.broadcast_to(ab_re, lam_re.shape)
        lam_im[...] = jnp.broadcast_to(ab_im, lam_im.shape)
        pr, pi = ab_re, ab_im
        for _ in range(seg.bit_length() - 1):
            pr, pi = _cmul(pr, pi, pr, pi)
        lseg_re[...] = pr
        lseg_im[...] = pi

    @pl.when(pl.program_id(1) == 0)
    def _():
        st_re[...] = re0_ref[0]
        st_im[...] = im0_ref[0]

    u = jnp.concatenate([u_ref[c, 0] for c in range(w // LANES)], axis=-1)
    ub = u.astype(BF16)
    gpt = V7X_MXU_DIM // C_STATE
    kpt = V7X_MXU_DIM // C_GROUP
    for nt in range(sw // V7X_MXU_DIM):
        kt = (nt * gpt) // kpt
        lhs = ub[:, kt * V7X_MXU_DIM:(kt + 1) * V7X_MXU_DIM]
        rows = slice(kt * V7X_MXU_DIM, (kt + 1) * V7X_MXU_DIM)
        cols = slice(nt * V7X_MXU_DIM, (nt + 1) * V7X_MXU_DIM)
        v_re[:, cols] = _dot(lhs, bf_re[rows, cols])
        v_im[:, cols] = _dot(lhs, bf_im[rows, cols])

    for ch in range(sw // S5_SCAN_LANES):
        lanes = slice(ch * S5_SCAN_LANES, (ch + 1) * S5_SCAN_LANES)
        l_re, l_im = lam_re[:, lanes], lam_im[:, lanes]

        def advance(j, sr, si, lanes=lanes, l_re=l_re, l_im=l_im):
            pr, pi = _cmul(l_re, l_im, sr, si)
            return pr + v_re[step_rows(j), lanes], pi + v_im[step_rows(j), lanes]

        zero = jnp.zeros((SUBL, S5_SCAN_LANES), F32)
        e_re, e_im = lax.fori_loop(0, seg, lambda j, c, adv=advance: adv(j, *c), (zero, zero),
                                   unroll=min(seg, S5_SCAN_UNROLL))

        row = lax.broadcasted_iota(jnp.int32, zero.shape, 0)
        c_re, c_im = st_re[:, lanes], st_im[:, lanes]
        in_re, in_im = zero, zero
        for k in range(SUBL):
            in_re = jnp.where(row == k, c_re, in_re)
            in_im = jnp.where(row == k, c_im, in_im)
            p_re, p_im = _cmul(lseg_re[:, lanes], lseg_im[:, lanes], c_re, c_im)
            c_re, c_im = p_re + e_re[k:k + 1, :], p_im + e_im[k:k + 1, :]
        st_re[:, lanes] = c_re
        st_im[:, lanes] = c_im
        sre_ref[0, :, lanes] = c_re
        sim_ref[0, :, lanes] = c_im

        def emit(j, c, lanes=lanes, adv=advance):
            sr, si = adv(j, *c)
            v_re[step_rows(j), lanes] = sr
            v_im[step_rows(j), lanes] = si
            return sr, si

        lax.fori_loop(0, seg, emit, (in_re, in_im), unroll=min(seg, S5_SCAN_UNROLL))

    ys = []
    for ot in range(w // V7X_MXU_DIM):
        acc = None
        for nt in range(ot * kpt // gpt, (ot + 1) * kpt // gpt):
            rows = slice(nt * V7X_MXU_DIM, (nt + 1) * V7X_MXU_DIM)
            cols = slice(ot * V7X_MXU_DIM, (ot + 1) * V7X_MXU_DIM)
            part = (_dot(v_re[:, rows].astype(BF16), cre_ref[rows, cols])
                    - _dot(v_im[:, rows].astype(BF16), cim_ref[rows, cols]))
            acc = part if acc is None else acc + part
        ys.append(acc)
    y = jnp.concatenate(ys, axis=-1) + d_ref[...] * u
    z = _gelu_tanh(y)
    res = z * jax.nn.sigmoid(_dot(z.astype(BF16), gw_ref[...]) + gbias_ref[...])
    for c in range(w // LANES):
        o_ref[c, 0] = res[:, c * LANES:(c + 1) * LANES]


def _prep_s5(b_re, b_im, c_re, c_im):
    g = b_re.shape[0]
    eye = jnp.eye(g, dtype=b_re.dtype)
    bd_in = lambda b: jnp.einsum('gpc,gh->gchp', b, eye).reshape(g * C_GROUP, g * C_STATE)
    bd_out = lambda c: jnp.einsum('gcp,gh->gphc', c, eye).reshape(g * C_STATE, g * C_GROUP).astype(BF16)
    return bd_in(b_re), bd_in(b_im), bd_out(c_re), bd_out(c_im)


def _s5(u, re0, im0, a_re, a_im, log_dt, mats, d_skip, glu_w, glu_b, n, t):
    w = u.shape[0] * LANES
    groups = w // C_GROUP
    sw = groups * C_STATE
    tt = _row_tile(t, 512)
    bre, bim, cre, cim = mats
    wc = w // LANES
    seq = pl.BlockSpec((wc, 1, tt, LANES), lambda b, i: (0, b, i, 0))
    st = pl.BlockSpec((1, 1, sw), lambda b, i: (b, 0, 0))
    vm = lambda shape, dt: pltpu.VMEM(shape, dt)
    flat = lambda a: a.reshape(1, sw)
    out, s_re, s_im = pl.pallas_call(
        functools.partial(_s5_body, tt=tt, w=w, sw=sw),
        grid=(n, t // tt),
        in_specs=[seq, st, st] + [_const_spec((1, sw))] * 3
                 + [_const_spec((w, sw)), _const_spec((w, sw)), _const_spec((sw, w)), _const_spec((sw, w)),
                    _const_spec((1, w)), _const_spec((w, w)), _const_spec((1, w))],
        out_specs=[seq, st, st],
        out_shape=[jax.ShapeDtypeStruct((wc, n, t, LANES), F32),
                   jax.ShapeDtypeStruct((n, 1, sw), F32), jax.ShapeDtypeStruct((n, 1, sw), F32)],
        scratch_shapes=[vm((1, sw), F32), vm((1, sw), F32), vm((w, sw), BF16), vm((w, sw), BF16),
                        vm((SUBL, sw), F32), vm((SUBL, sw), F32), vm((1, sw), F32), vm((1, sw), F32),
                        vm((tt, sw), F32), vm((tt, sw), F32)],
        compiler_params=_cparams(2),
        name="s5",
    )(u.reshape(wc, n, t, LANES), re0.reshape(n, 1, sw), im0.reshape(n, 1, sw),
      flat(a_re), flat(a_im), flat(jnp.repeat(log_dt, C_STATE)), bre, bim, cre, cim,
      d_skip.reshape(1, w), glu_w.astype(BF16), glu_b.reshape(1, w))
    return out.reshape(wc, n * t, LANES), s_re, s_im


def _conf_body(g_ref, conv0_ref, cw_ref, cb_ref, lng_ref, lnb_ref, o_ref, convn_ref, ext_sc, *, tt, w):
    halo = 32
    hist = D_CONV - 1
    step = pl.program_id(1)

    @pl.when(step == 0)
    def _():
        ext_sc[0:halo, :] = jnp.zeros((halo, w), F32)
        ext_sc[halo - hist:halo, :] = conv0_ref[0]

    ext_sc[halo:halo + tt, :] = jnp.concatenate([g_ref[c, 0] for c in range(w // LANES)], axis=-1)
    cw = cw_ref[...]
    c = cb_ref[...] + jnp.zeros((tt, w), F32)
    for j in range(D_CONV):
        c = c + ext_sc[halo - hist + j:halo - hist + j + tt, :] * cw[j:j + 1, :]
    o_ref[0] = _ln_silu(c, lng_ref[...], lnb_ref[...]).astype(o_ref.dtype)
    convn_ref[0] = ext_sc[halo + tt - hist:halo + tt, :]
    ext_sc[0:halo, :] = ext_sc[tt:tt + halo, :]


def _ln_silu(c, g, b):
    cc = c - jnp.mean(c, axis=-1, keepdims=True)
    y = cc * lax.rsqrt(jnp.mean(cc * cc, axis=-1, keepdims=True) + EPS) * g + b
    return y * jax.nn.sigmoid(y)


def _conf_seg_body(g_ref, conv0_ref, cw_ref, cb_ref, lng_ref, lnb_ref, o_ref, convn_ref,
                   ext_sc, prev_sc, conv_sc, *, tt, w, rows_per_chunk):
    seg = tt // SUBL
    hist = D_CONV - 1
    step = pl.program_id(1)

    @pl.when(step == 0)
    def _():
        c0 = conv0_ref[0]
        for i in range(hist):
            prev_sc[i * SUBL:(i + 1) * SUBL, :] = jnp.broadcast_to(c0[i:i + 1, :], (SUBL, w))

    cur = jnp.concatenate([g_ref[c, 0] for c in range(w // LANES)], axis=-1)
    ext_sc[hist * SUBL:, :] = cur
    row = lax.broadcasted_iota(jnp.int32, (SUBL, w), 0)
    tail = cur[(seg - hist) * SUBL:, :]
    for i in range(hist):
        grp = jnp.where(row == SUBL - 1, prev_sc[i * SUBL:(i + 1) * SUBL, :],
                        tail[i * SUBL:(i + 1) * SUBL, :])
        ext_sc[i * SUBL:(i + 1) * SUBL, :] = pltpu.roll(grp, 1, 0)
    prev_sc[...] = tail
    convn_ref[0] = jnp.concatenate(
        [tail[i * SUBL + SUBL - 1:(i + 1) * SUBL, :] for i in range(hist)], axis=0)

    cw = cw_ref[...]
    gpc = rows_per_chunk // SUBL
    for c in range(w // LANES):
        lanes = slice(c * LANES, (c + 1) * LANES)
        taps = [jnp.broadcast_to(cw[k:k + 1, lanes], (SUBL, LANES)) for k in range(D_CONV)]
        bias = jnp.broadcast_to(cb_ref[:, lanes], (SUBL, LANES))

        def chunk(i, _, lanes=lanes, taps=taps, bias=bias):
            base = pl.multiple_of(i * rows_per_chunk, rows_per_chunk)
            accs = [bias] * gpc
            for g in range(gpc + D_CONV - 1):
                xg = ext_sc[pl.ds(base + g * SUBL, SUBL), lanes]
                for jj in range(gpc):
                    if 0 <= g - jj < D_CONV:
                        accs[jj] = accs[jj] + xg * taps[g - jj]
            conv_sc[pl.ds(base, rows_per_chunk), lanes] = jnp.concatenate(accs, axis=0)
            return 0

        lax.fori_loop(0, tt // rows_per_chunk, chunk, 0, unroll=2)

    y = _ln_silu(conv_sc[...], lng_ref[...], lnb_ref[...])
    for c in range(w // LANES):
        o_ref[c, 0] = y[:, c * LANES:(c + 1) * LANES]


def _conf(g_in, conv0, conv_w, conv_b, ln_g, ln_b, n, t, seg):
    wc = g_in.shape[0]
    w = wc * LANES
    hist = pl.BlockSpec((1, D_CONV - 1, w), lambda b, i: (b, 0, 0))
    params = (conv0, conv_w, conv_b.reshape(1, w), ln_g.reshape(1, w), ln_b.reshape(1, w))
    param_specs = [hist, _const_spec((D_CONV, w)), _const_spec((1, w)), _const_spec((1, w)),
                   _const_spec((1, w))]
    hist_shape = jax.ShapeDtypeStruct((n, D_CONV - 1, w), F32)
    if seg == 1:
        tt = _row_tile(t, 256)
        cols = pl.BlockSpec((wc, 1, tt, LANES), lambda b, i: (0, b, i, 0))
        out, cd = pl.pallas_call(
            functools.partial(_conf_body, tt=tt, w=w),
            grid=(n, t // tt),
            in_specs=[cols] + param_specs,
            out_specs=[pl.BlockSpec((1, tt, w), lambda b, i: (b, i, 0)), hist],
            out_shape=[jax.ShapeDtypeStruct((n, t, w), BF16), hist_shape],
            scratch_shapes=[pltpu.VMEM((tt + 32, w), F32)],
            compiler_params=_cparams(2),
            name="conformer_conv",
        )(g_in.reshape(wc, n, t, LANES), *params)
        return out.reshape(n * t, w), cd
    assert seg >= D_CONV - 1
    tt = SUBL * seg
    cols = pl.BlockSpec((wc, 1, tt, LANES), lambda b, i: (0, b, i, 0))
    out, cd = pl.pallas_call(
        functools.partial(_conf_seg_body, tt=tt, w=w, rows_per_chunk=64),
        grid=(n, t // tt),
        in_specs=[cols] + param_specs,
        out_specs=[cols, hist],
        out_shape=[jax.ShapeDtypeStruct((wc, n, t, LANES), F32), hist_shape],
        scratch_shapes=[pltpu.VMEM(((seg + D_CONV - 1) * SUBL, w), F32),
                        pltpu.VMEM(((D_CONV - 1) * SUBL, w), F32), pltpu.VMEM((tt, w), F32)],
        compiler_params=_cparams(2),
        name="conformer_conv_seg",
    )(g_in.reshape(wc, n, t, LANES), *params)
    return out.reshape(wc, n * t, LANES), cd


def _run_group(x3, cache, conv_b0, h_b0, c_re0, c_im0, conv_d0, p, prepped):
    n, t, d = x3.shape
    depth = p['ffn1_g'].shape[0]
    x = x3.reshape(n * t, d)
    ks, vs, cbs, hbs, cres, cims, cds = [], [], [], [], [], [], []
    for l in range(depth):
        x = _ffn(x, p['ffn1_g'][l], prepped['ffn1'], l)
        if l % 2 == 0:
            e = l // 2
            hw = p['even_w_in'].shape[2] // 5
            heads = hw // A_V
            seg = SEG_BLOCK // SUBL if t % SEG_BLOCK == 0 else 1
            q, k, kb, v, vb, xb, gb = _inproj_even(x, p['mix_g'][l], prepped['even_w_in'][e],
                                                   p['a_q_g'][e], p['a_k_g'][e], seg)
            lam_init = 0.8 - 0.6 * math.exp(-0.3 * l)
            if cache is None:
                att = _attn_prompt(q, kb, vb, p['a_lambda'][e], p['a_head_g'][e], n, t, lam_init)
            else:
                cache_k, cache_v, page_table = cache
                att = _attn_sample(q, k, v, cache_k, cache_v, page_table, e, p['a_lambda'][e],
                                   p['a_head_g'][e], n, t, lam_init)
            rec, cb, hb = _rglru(xb, gb, conv_b0[e], h_b0[e], p['b_conv_w'][e], p['b_conv_b'][e],
                                 prepped['b_gate_w'][e], p['b_gate_b'][e], p['b_lambda'][e], n, t, seg)
            mixer, mixer_seg = (att, rec, prepped['even_w_out'][e]), seg
            ks.append(k.reshape(n, t, heads, 2 * A_QK))
            vs.append(v.reshape(n, t, heads, A_V))
            cbs.append(cb)
            hbs.append(hb.reshape(n, hw))
        else:
            o = l // 2
            seg = SEG_BLOCK // SUBL if t % SEG_BLOCK == 0 else 1
            u, g_in = _inproj_odd(x, p['mix_g'][l], prepped['odd_w_in'][o], seg)
            groups = u.shape[0] * LANES // C_GROUP
            c_out, cre, cim = _s5(u, c_re0[o], c_im0[o], p['c_a_re'][o], p['c_a_im'][o], p['c_log_dt'][o],
                                  prepped['s5'][o], p['c_d'][o], p['c_glu_w'][o], p['c_glu_b'][o], n, t)
            d_out, cd = _conf(g_in, conv_d0[o], p['d_conv_w'][o], p['d_conv_b'][o],
                              p['d_ln_g'][o], p['d_ln_b'][o], n, t, seg)
            mixer, mixer_seg = (c_out, d_out, prepped['odd_w_out'][o]), seg
            cres.append(cre.reshape(n, groups, C_STATE))
            cims.append(cim.reshape(n, groups, C_STATE))
            cds.append(cd)
        x = _ffn(x, p['ffn2_g'][l], prepped['ffn2'], l, mixer, mixer_seg)
    stack = lambda xs: xs[0][None] if len(xs) == 1 else jnp.stack(xs)
    return (x.reshape(n, t, d), stack(ks), stack(vs), stack(cbs), stack(hbs),
            stack(cres), stack(cims), stack(cds))


def kernel(x_prompt, x_sample, cache_k, cache_v, state_conv_b, state_h_b, state_c_re, state_c_im,
           state_conv_d, page_table, ffn1_g, ffn1_w_gu, ffn1_w_down, mix_g, ffn2_g, ffn2_w_gu,
           ffn2_w_down, even_w_in, even_w_out, a_q_g, a_k_g, a_lambda, a_head_g, b_conv_w, b_conv_b,
           b_gate_w, b_gate_b, b_lambda, odd_w_in, odd_w_out, c_a_re, c_a_im, c_log_dt, c_b_re, c_b_im,
           c_c_re, c_c_im, c_d, c_glu_w, c_glu_b, d_conv_w, d_conv_b, d_ln_g, d_ln_b):
    p = dict(ffn1_g=ffn1_g, mix_g=mix_g, ffn2_g=ffn2_g, even_w_in=even_w_in, a_q_g=a_q_g, a_k_g=a_k_g,
             a_lambda=a_lambda, a_head_g=a_head_g, b_conv_w=b_conv_w, b_conv_b=b_conv_b,
             b_gate_b=b_gate_b, b_lambda=b_lambda, c_a_re=c_a_re, c_a_im=c_a_im, c_log_dt=c_log_dt,
             c_d=c_d, c_glu_w=c_glu_w, c_glu_b=c_glu_b, d_conv_w=d_conv_w, d_conv_b=d_conv_b,
             d_ln_g=d_ln_g, d_ln_b=d_ln_b)
    depth = ffn1_g.shape[0]
    n_even, n_odd = even_w_in.shape[0], odd_w_in.shape[0]
    prepped = dict(
        ffn1=_prep_ffn(ffn1_w_gu, ffn1_w_down),
        ffn2=_prep_ffn(ffn2_w_gu, ffn2_w_down),
        even_w_in=[even_w_in[e].astype(BF16) for e in range(n_even)],
        even_w_out=[even_w_out[e].astype(BF16) for e in range(n_even)],
        odd_w_in=[odd_w_in[o].astype(BF16) for o in range(n_odd)],
        odd_w_out=[odd_w_out[o].astype(BF16) for o in range(n_odd)],
        b_gate_w=[_prep_gate_w(b_gate_w[e]) for e in range(n_even)],
        s5=[_prep_s5(c_b_re[o], c_b_im[o], c_c_re[o], c_c_im[o]) for o in range(n_odd)],
    )
    b = x_prompt.shape[0]
    dt = x_prompt.dtype
    hw = even_w_in.shape[2] // 5
    groups, states = state_c_re.shape[2], state_c_re.shape[3]
    prompt = _run_group(
        x_prompt, None,
        jnp.zeros((n_even, b, B_CONV - 1, hw), dt), jnp.zeros((n_even, b, hw), dt),
        jnp.zeros((n_odd, b, groups, states), dt), jnp.zeros((n_odd, b, groups, states), dt),
        jnp.zeros((n_odd, b, D_CONV - 1, hw), dt), p, prepped)
    sample = _run_group(
        x_sample, (cache_k, cache_v, page_table), state_conv_b, state_h_b, state_c_re, state_c_im,
        state_conv_d, p, prepped)
    return (prompt[0], sample[0]) + prompt[1:] + sample[1:]
```
